```python
import math
import jax, jax.numpy as jnp
from jax import lax
import numpy as np

D_MODEL = 1024
BATCH = 16
SEQ = 2048
DEPTH = 1

N_META = 16
POOL_WIDTH = D_MODEL // 2
POOL_WINDOWS = (2, 4, 8, 16)
POOL_GROUP = POOL_WIDTH // len(POOL_WINDOWS)
ATTN_WIDTH = D_MODEL - POOL_WIDTH
DIFF_HEAD_DIM = 64
N_ATTN_HEADS = ATTN_WIDTH // (2 * DIFF_HEAD_DIM)
V_HEAD_DIM = 2 * DIFF_HEAD_DIM
IN_WIDTH = POOL_WIDTH + 3 * ATTN_WIDTH
MIX_WIDTH = POOL_WIDTH + N_ATTN_HEADS * V_HEAD_DIM
REL_BUCKETS = 32
REL_MAX_DIST = 128
D_FF = int(math.ceil(math.ceil(8 * D_MODEL / 3) / 256) * 256)
Q_BLOCK = 128
NORM_EPS = 1e-6
SUBLN_EPS = 1e-5

kernel_name = "hymba_pool_diffattn_block"


def _rmsnorm(x, g, eps=NORM_EPS):
    xf = x.astype(jnp.float32)
    y = xf * lax.rsqrt(jnp.mean(xf * xf, axis=-1, keepdims=True) + eps)
    return (y * g.astype(jnp.float32)).astype(x.dtype)


def _t5_bucket(rel):
    nb = REL_BUCKETS // 2
    ret = jnp.where(rel > 0, nb, 0)
    n = jnp.abs(rel)
    max_exact = nb // 2
    nf = jnp.maximum(n, 1).astype(jnp.float32)
    large = max_exact + (jnp.log(nf / max_exact) / math.log(REL_MAX_DIST / max_exact)
                         * (nb - max_exact)).astype(jnp.int32)
    large = jnp.minimum(large, nb - 1)
    return ret + jnp.where(n < max_exact, n, large)


def _multiscale_pool(u, pool_w, pool_scale):
    B, L, _ = u.shape
    uf = u.astype(jnp.float32)
    csum = jnp.concatenate([jnp.zeros((B, 1, POOL_WIDTH), jnp.float32),
                            jnp.cumsum(uf, axis=1)], axis=1)
    pos = jnp.arange(L)
    diffs = []
    for g, w in enumerate(POOL_WINDOWS):
        left = w // 2
        right = w - 1 - left
        lo = jnp.clip(pos - left, 0, L - 1)
        hi = jnp.clip(pos + right, 0, L - 1)
        cg = csum[..., g * POOL_GROUP:(g + 1) * POOL_GROUP]
        window_sum = cg[:, hi + 1] - cg[:, lo]
        cnt = (hi - lo + 1).astype(jnp.float32)[None, :, None]
        diffs.append(window_sum / cnt - uf[..., g * POOL_GROUP:(g + 1) * POOL_GROUP])
    d = jnp.stack(diffs, axis=2)
    y = jnp.einsum('blgc,gcd->blgd', d, pool_w.astype(jnp.float32)).reshape(B, L, POOL_WIDTH)
    return (y * pool_scale.astype(jnp.float32)).astype(u.dtype)


def _diff_attention(q, k, v, rel_bias, lam, subln_g, lambda_init):
    B, L = q.shape[0], q.shape[1]
    n_blocks = -(-L // Q_BLOCK)
    Lp = n_blocks * Q_BLOCK
    qp = jnp.pad(q, ((0, 0), (0, Lp - L), (0, 0), (0, 0), (0, 0)))
    qb = qp.reshape(B, n_blocks, Q_BLOCK, N_ATTN_HEADS, 2, DIFF_HEAD_DIM).transpose(1, 0, 2, 3, 4, 5)
    kpos = jnp.arange(L)
    scale = DIFF_HEAD_DIM ** -0.5

    def block(args):
        qblk, start = args
        qpos = start + jnp.arange(Q_BLOCK)
        bucket = _t5_bucket(kpos[None, :] - qpos[:, None])
        bias = jnp.transpose(rel_bias[bucket], (2, 0, 1)).astype(jnp.float32)
        s = jnp.einsum('bqhcd,bkhcd->bchqk', qblk, k).astype(jnp.float32) * scale + bias[None, None]
        p = jax.nn.softmax(s, axis=-1)
        a = p[:, 0] - lam * p[:, 1]
        return jnp.einsum('bhqk,bkhe->bqhe', a.astype(v.dtype), v)

    starts = jnp.arange(n_blocks, dtype=jnp.int32) * Q_BLOCK
    o = lax.map(block, (qb, starts))
    o = o.transpose(1, 0, 2, 3, 4).reshape(B, Lp, N_ATTN_HEADS, V_HEAD_DIM)[:, :L]
    o = _rmsnorm(o, subln_g, eps=SUBLN_EPS) * (1.0 - lambda_init)
    return o.reshape(B, L, N_ATTN_HEADS * V_HEAD_DIM).astype(v.dtype)


def setup_inputs(seed: int = 0) -> dict:
    key = jax.random.key(seed)
    ks = jax.random.split(key, 20)
    f32 = jnp.float32
    nrm = lambda k, shape, s: jax.random.normal(k, shape, f32) * s
    return {
        "x": nrm(ks[0], (BATCH, SEQ, D_MODEL), 1.0),
        "meta_tokens": nrm(ks[1], (N_META, D_MODEL), 1.0),
        "rel_bias": nrm(ks[2], (REL_BUCKETS, N_ATTN_HEADS), 0.5),
        "norm1_g": 1.0 + nrm(ks[3], (DEPTH, D_MODEL), 0.02),
        "w_in": nrm(ks[4], (DEPTH, D_MODEL, IN_WIDTH), D_MODEL ** -0.5),
        "pool_w": nrm(ks[5], (DEPTH, len(POOL_WINDOWS), POOL_GROUP, POOL_GROUP), POOL_GROUP ** -0.5),
        "pool_scale": 1.0 + nrm(ks[6], (DEPTH, POOL_WIDTH), 0.1),
        "lambda_q1": nrm(ks[7], (DEPTH, DIFF_HEAD_DIM), 0.1),
        "lambda_k1": nrm(ks[8], (DEPTH, DIFF_HEAD_DIM), 0.1),
        "lambda_q2": nrm(ks[9], (DEPTH, DIFF_HEAD_DIM), 0.1),
        "lambda_k2": nrm(ks[10], (DEPTH, DIFF_HEAD_DIM), 0.1),
        "subln_g": 1.0 + nrm(ks[11], (DEPTH, V_HEAD_DIM), 0.02),
        "w_o": nrm(ks[12], (DEPTH, MIX_WIDTH, D_MODEL), MIX_WIDTH ** -0.5),
        "norm2_g": 1.0 + nrm(ks[13], (DEPTH, D_MODEL), 0.02),
        "w_gate": nrm(ks[14], (DEPTH, D_MODEL, D_FF), D_MODEL ** -0.5),
        "w_up": nrm(ks[15], (DEPTH, D_MODEL, D_FF), D_MODEL ** -0.5),
        "w_down": nrm(ks[16], (DEPTH, D_FF, D_MODEL), D_FF ** -0.5),
        "final_g": 1.0 + nrm(ks[17], (D_MODEL,), 0.02),
    }


def reference(x, meta_tokens, rel_bias, norm1_g, w_in, pool_w, pool_scale, lambda_q1, lambda_k1,
              lambda_q2, lambda_k2, subln_g, w_o, norm2_g, w_gate, w_up, w_down, final_g):
    B = x.shape[0]
    meta = jnp.broadcast_to(meta_tokens[None].astype(x.dtype), (B, N_META, D_MODEL))
    h = jnp.concatenate([meta, x], axis=1)
    L = h.shape[1]
    o_q = POOL_WIDTH
    o_k = o_q + ATTN_WIDTH
    o_v = o_k + ATTN_WIDTH
    for layer in range(DEPTH):
        lambda_init = 0.8 - 0.6 * math.exp(-0.3 * layer)
        lam = (jnp.exp(jnp.sum(lambda_q1[layer].astype(jnp.float32) * lambda_k1[layer].astype(jnp.float32)))
               - jnp.exp(jnp.sum(lambda_q2[layer].astype(jnp.float32) * lambda_k2[layer].astype(jnp.float32)))
               + lambda_init)
        u = _rmsnorm(h, norm1_g[layer])
        z = u @ w_in[layer]
        y_pool = _multiscale_pool(z[..., :o_q], pool_w[layer], pool_scale[layer])
        q = z[..., o_q:o_k].reshape(B, L, N_ATTN_HEADS, 2, DIFF_HEAD_DIM)
        k = z[..., o_k:o_v].reshape(B, L, N_ATTN_HEADS, 2, DIFF_HEAD_DIM)
        v = z[..., o_v:].reshape(B, L, N_ATTN_HEADS, V_HEAD_DIM)
        y_attn = _diff_attention(q, k, v, rel_bias, lam, subln_g[layer], lambda_init)
        h = h + jnp.concatenate([y_pool, y_attn], axis=-1) @ w_o[layer]
        f = _rmsnorm(h, norm2_g[layer])
        h = h + (jax.nn.silu(f @ w_gate[layer]) * (f @ w_up[layer])) @ w_down[layer]
    return _rmsnorm(h, final_g)[:, N_META:]
```

```python
import functools
import math

import jax
import jax.numpy as jnp
from jax import lax
from jax.experimental import pallas as pl
from jax.experimental.pallas import tpu as pltpu

F32 = jnp.float32
BF16 = jnp.bfloat16

D_MODEL = 1024
N_META = 16
POOL_WIDTH = 512
POOL_WINDOWS = (2, 4, 8, 16)
POOL_GROUP = 128
ATTN_WIDTH = 512
HEAD_DIM = 64
N_HEADS = 4
V_DIM = 128
IN_WIDTH = POOL_WIDTH + 3 * ATTN_WIDTH
REL_BUCKETS = 32
REL_MAX_DIST = 128
D_FF = 2816
NORM_EPS = 1e-6
SUBLN_EPS = 1e-5
LAMBDA_INIT = 0.8 - 0.6 * math.exp(-0.3 * 0)
HALO = 8

VMEM_LIMIT_BYTES = 56 * 1024 * 1024

TQ = 256
TK = 256
N_BIAS_TILES = 5
BIAS_ROWS = N_BIAS_TILES * TK + 2 * N_META
FF_CHUNK = 256


def _inproj_kernel(x_ref, g_ref, w_ref, zp_ref, q_ref, k_ref, v_ref):
    x = x_ref[...]
    ms = jnp.mean(x * x, axis=-1, keepdims=True)
    u = ((x * lax.rsqrt(ms + NORM_EPS)) * g_ref[...]).astype(BF16)

    def proj(lo, hi):
        return jnp.dot(u, w_ref[:, lo:hi], preferred_element_type=F32)

    o_q = POOL_WIDTH
    o_k = o_q + ATTN_WIDTH
    o_v = o_k + ATTN_WIDTH
    zp_ref[...] = proj(0, o_q)
    q_ref[...] = proj(o_q, o_k).astype(BF16)
    k_ref[...] = proj(o_k, o_v).astype(BF16)
    v_ref[...] = proj(o_v, IN_WIDTH).astype(BF16)


def _inproj(x2d, g, w_bf16, tm):
    rows = x2d.shape[0]
    row_spec = lambda width: pl.BlockSpec((tm, width), lambda i: (i, 0))
    return pl.pallas_call(
        _inproj_kernel,
        grid=(rows // tm,),
        in_specs=[
            row_spec(D_MODEL),
            pl.BlockSpec((1, D_MODEL), lambda i: (0, 0)),
            pl.BlockSpec((D_MODEL, IN_WIDTH), lambda i: (0, 0)),
        ],
        out_specs=[row_spec(POOL_WIDTH), row_spec(ATTN_WIDTH), row_spec(ATTN_WIDTH), row_spec(ATTN_WIDTH)],
        out_shape=[
            jax.ShapeDtypeStruct((rows, POOL_WIDTH), F32),
            jax.ShapeDtypeStruct((rows, ATTN_WIDTH), BF16),
            jax.ShapeDtypeStruct((rows, ATTN_WIDTH), BF16),
            jax.ShapeDtypeStruct((rows, ATTN_WIDTH), BF16),
        ],
        compiler_params=pltpu.CompilerParams(
            dimension_semantics=("arbitrary",), vmem_limit_bytes=VMEM_LIMIT_BYTES),
        name="inproj",
    )(x2d, g, w_bf16)


def _t5_bucket(rel):
    nb = REL_BUCKETS // 2
    ret = jnp.where(rel > 0, nb, 0)
    n = jnp.abs(rel)
    max_exact = nb // 2
    nf = jnp.maximum(n, 1).astype(F32)
    large = max_exact + (jnp.log(nf / max_exact) / math.log(REL_MAX_DIST / max_exact)
                         * (nb - max_exact)).astype(jnp.int32)
    large = jnp.minimum(large, nb - 1)
    return ret + jnp.where(n < max_exact, n, large)


def _bias_buckets():
    r = lax.broadcasted_iota(jnp.int32, (TK, TQ), 0)
    c = lax.broadcasted_iota(jnp.int32, (TK, TQ), 1)
    tiles = [(t - 2) * TK + r - c for t in range(N_BIAS_TILES)]
    rm = lax.broadcasted_iota(jnp.int32, (N_META, TQ), 0)
    cm = lax.broadcasted_iota(jnp.int32, (N_META, TQ), 1)
    metas = [rm - N_META - m * TQ - cm for m in range(2)]
    return _t5_bucket(jnp.concatenate(tiles + metas, axis=0))


def _bias_kernel(rb_ref, bkt_ref, o_ref):
    h = pl.program_id(0)
    bkt = bkt_ref[...]
    acc = jnp.zeros(bkt.shape, F32)
    for b in range(REL_BUCKETS):
        acc = jnp.where(bkt == b, rb_ref[b, h], acc)
    o_ref[...] = acc


def _bias_tiles(rel_bias):
    return pl.pallas_call(
        _bias_kernel,
        grid=(N_HEADS,),
        in_specs=[
            pl.BlockSpec(memory_space=pltpu.SMEM),
            pl.BlockSpec((BIAS_ROWS, TQ), lambda h: (0, 0)),
        ],
        out_specs=pl.BlockSpec((None, BIAS_ROWS, TQ), lambda h: (h, 0, 0)),
        out_shape=jax.ShapeDtypeStruct((N_HEADS, BIAS_ROWS, TQ), F32),
        compiler_params=pltpu.CompilerParams(dimension_semantics=("arbitrary",)),
        name="bias_tiles",
    )(rel_bias, _bias_buckets())


_NT = (((1,), (1,)), ((), ()))


def _attn_kernel(lam_ref, q_ref, k_ref, v_ref, km_ref, vm_ref, bias_ref, g_ref, o_ref,
                 s_scr, p_scr, vt_scr, *, seq):
    n_kc = seq // TK
    n_qt = seq // TQ
    vt_scr[...] = v_ref[...].astype(F32).T.astype(BF16)
    vm_pad = jnp.concatenate(
        [vm_ref[...].astype(F32), jnp.zeros((V_DIM - N_META, V_DIM), F32)], axis=0)
    vmt = vm_pad.T[:, :N_META].astype(BF16)
    km = km_ref[...]
    lam = lam_ref[0, 0]
    lane = lax.broadcasted_iota(jnp.int32, (TQ, 2 * HEAD_DIM), 1)

    def tile(qi, carry):
        r0 = pl.multiple_of(qi * TQ, TQ)
        qf = q_ref[pl.ds(r0, TQ), :].astype(F32)
        meta_row = pl.multiple_of(N_BIAS_TILES * TK + jnp.minimum(qi, 1) * N_META, N_META)
        parts = []
        for c in range(2):
            keep = (lane < HEAD_DIM) if c == 0 else (lane >= HEAD_DIM)
            qm = jnp.where(keep, qf, 0.0).astype(BF16)
            s_all = lax.dot_general(k_ref[...], qm, _NT, preferred_element_type=F32)
            m_acc = jnp.full((8, TQ), -jnp.inf, F32)
            for kc in range(n_kc):
                t = jnp.clip(kc - qi, -2, 2) + 2
                b0 = pl.multiple_of(t * TK, TK)
                sc = s_all[kc * TK:(kc + 1) * TK] + bias_ref[pl.ds(b0, TK), :]
                s_scr[c, kc * TK:(kc + 1) * TK, :] = sc
                m_acc = jnp.maximum(m_acc, jnp.max(sc.reshape(TK // 8, 8, TQ), axis=0))
            sm = (lax.dot_general(km, qm, _NT, preferred_element_type=F32)
                  + bias_ref[pl.ds(meta_row, N_META), :])
            m = jnp.maximum(jnp.max(m_acc, axis=0, keepdims=True),
                            jnp.max(sm, axis=0, keepdims=True))
            l_acc = jnp.zeros((8, TQ), F32)
            for kc in range(n_kc):
                p = jnp.exp(s_scr[c, kc * TK:(kc + 1) * TK, :] - m)
                l_acc = l_acc + jnp.sum(p.reshape(TK // 8, 8, TQ), axis=0)
                p_scr[c, kc * TK:(kc + 1) * TK, :] = p.astype(BF16)
            pm = jnp.exp(sm - m)
            l = jnp.sum(l_acc, axis=0, keepdims=True) + jnp.sum(pm, axis=0, keepdims=True)
            ot = (jnp.dot(vt_scr[...], p_scr[c], preferred_element_type=F32)
                  + jnp.dot(vmt, pm.astype(BF16), preferred_element_type=F32))
            parts.append((ot, l))
        (o0, l0), (o1, l1) = parts
        o = o0 * (1.0 / l0) - o1 * (lam / l1)
        ms = jnp.mean(o * o, axis=0, keepdims=True)
        y = ((o * lax.rsqrt(ms + SUBLN_EPS)) * g_ref[...]) * (1.0 - LAMBDA_INIT)
        o_ref[pl.ds(r0, TQ), :] = y.T.astype(BF16)
        return carry

    lax.fori_loop(0, n_qt, tile, 0)


def _attention(lam, q, k, v, k_meta, v_meta, bias, g2d, batch, seq):
    head_rows = lambda: pl.BlockSpec((None, seq, V_DIM), lambda b, h: (b, 0, h))
    meta_spec = lambda: pl.BlockSpec((N_META, V_DIM), lambda b, h: (0, h))
    return pl.pallas_call(
        functools.partial(_attn_kernel, seq=seq),
        grid=(batch, N_HEADS),
        in_specs=[
            pl.BlockSpec(memory_space=pltpu.SMEM),
            head_rows(), head_rows(), head_rows(),
            meta_spec(), meta_spec(),
            pl.BlockSpec((None, BIAS_ROWS, TQ), lambda b, h: (h, 0, 0)),
            pl.BlockSpec((V_DIM, TQ), lambda b, h: (0, 0)),
        ],
        out_specs=head_rows(),
        out_shape=jax.ShapeDtypeStruct((batch, seq, N_HEADS * V_DIM), BF16),
        scratch_shapes=[
            pltpu.VMEM((2, seq, TQ), F32),
            pltpu.VMEM((2, seq, TQ), BF16),
            pltpu.VMEM((V_DIM, seq), BF16),
        ],
        compiler_params=pltpu.CompilerParams(
            dimension_semantics=("arbitrary", "arbitrary"), vmem_limit_bytes=VMEM_LIMIT_BYTES),
        name="diff_attn",
    )(lam, q, k, v, k_meta, v_meta, bias, g2d)


def _rms(x, g):
    ms = jnp.mean(x * x, axis=-1, keepdims=True)
    return (x * lax.rsqrt(ms + NORM_EPS)) * g


def _tail_kernel(x_ref, zp_ref, zprev_ref, znext_ref, zmeta_ref, ya_ref, pw_ref, ps_ref, wo_ref,
                 g2_ref, wg_ref, wu_ref, wd_ref, gf_ref, o_ref, e_scr, *, tm, seq):
    tiles_per_seq = seq // tm
    t = lax.rem(pl.program_id(0), tiles_per_seq)
    p0 = t * tm
    prev = jnp.where(t == 0, zmeta_ref[...], zprev_ref[...])
    nxt = jnp.where(t == tiles_per_seq - 1, 0.0, znext_ref[...])
    zp = zp_ref[...]
    for g in range(len(POOL_WINDOWS)):
        lanes = slice(g * POOL_GROUP, (g + 1) * POOL_GROUP)
        e_scr[g, 0:HALO, :] = prev[:, lanes]
        e_scr[g, HALO:HALO + tm, :] = zp[:, lanes]
        e_scr[g, HALO + tm:2 * HALO + tm, :] = nxt[:, lanes]

    row = lax.broadcasted_iota(jnp.int32, (tm, POOL_GROUP), 0)
    y_groups = []
    for g, w in enumerate(POOL_WINDOWS):
        left = w // 2
        right = w - 1 - left
        win = e_scr[g, HALO - left:HALO - left + tm, :]
        for k in range(-left + 1, right + 1):
            win = win + e_scr[g, HALO + k:HALO + k + tm, :]
        over = jnp.maximum(p0 + row + (right - (seq - 1)), 0)
        cnt = (w - over).astype(F32)
        d = win / cnt - zp[:, g * POOL_GROUP:(g + 1) * POOL_GROUP]
        yg = jnp.dot(d.astype(BF16), pw_ref[g], preferred_element_type=F32)
        y_groups.append((yg * ps_ref[:, g * POOL_GROUP:(g + 1) * POOL_GROUP]).astype(BF16))
    ycat = jnp.concatenate(y_groups + [ya_ref[...]], axis=-1)

    h1 = x_ref[...] + jnp.dot(ycat, wo_ref[...], preferred_element_type=F32)
    f = _rms(h1, g2_ref[...]).astype(BF16)
    acc = h1
    for c in range(D_FF // FF_CHUNK):
        cols = slice(c * FF_CHUNK, (c + 1) * FF_CHUNK)
        gate = jnp.dot(f, wg_ref[:, cols], preferred_element_type=F32)
        up = jnp.dot(f, wu_ref[:, cols], preferred_element_type=F32)
        act = (gate * jax.nn.sigmoid(gate) * up).astype(BF16)
        acc = acc + jnp.dot(act, wd_ref[cols, :], preferred_element_type=F32)
    o_ref[...] = _rms(acc, gf_ref[...])


def _tail(x2d, zp, zp_meta, y_attn, pool_w, pool_scale, w_o, g2, w_gate, w_up, w_down, g_final,
          tm, seq):
    rows = x2d.shape[0]
    halo_blocks = tm // HALO
    last_halo_block = rows // HALO - 1
    const = lambda shape: pl.BlockSpec(shape, lambda i: (0,) * len(shape),
                                       pipeline_mode=pl.Buffered(1))
    return pl.pallas_call(
        functools.partial(_tail_kernel, tm=tm, seq=seq),
        grid=(rows // tm,),
        in_specs=[
            pl.BlockSpec((tm, D_MODEL), lambda i: (i, 0)),
            pl.BlockSpec((tm, POOL_WIDTH), lambda i: (i, 0)),
            pl.BlockSpec((HALO, POOL_WIDTH), lambda i: (jnp.maximum(i * halo_blocks - 1, 0), 0)),
            pl.BlockSpec((HALO, POOL_WIDTH),
                         lambda i: (jnp.minimum((i + 1) * halo_blocks, last_halo_block), 0)),
            pl.BlockSpec((HALO, POOL_WIDTH), lambda i: (N_META // HALO - 1, 0)),
            pl.BlockSpec((tm, N_HEADS * V_DIM), lambda i: (i, 0)),
            const((len(POOL_WINDOWS), POOL_GROUP, POOL_GROUP)),
            const((1, POOL_WIDTH)),
            const((D_MODEL, D_MODEL)),
            const((1, D_MODEL)),
            const((D_MODEL, D_FF)),
            const((D_MODEL, D_FF)),
            const((D_FF, D_MODEL)),
            const((1, D_MODEL)),
        ],
        out_specs=pl.BlockSpec((tm, D_MODEL), lambda i: (i, 0)),
        out_shape=jax.ShapeDtypeStruct((rows, D_MODEL), F32),
        scratch_shapes=[pltpu.VMEM((len(POOL_WINDOWS), tm + 2 * HALO, POOL_GROUP), F32)],
        compiler_params=pltpu.CompilerParams(
            dimension_semantics=("arbitrary",), vmem_limit_bytes=VMEM_LIMIT_BYTES),
        name="pool_oproj_ffn",
    )(x2d, zp, zp, zp, zp_meta, y_attn, pool_w, pool_scale, w_o, g2, w_gate, w_up, w_down, g_final)


def kernel(x, meta_tokens, rel_bias, norm1_g, w_in, pool_w, pool_scale, lambda_q1, lambda_k1,
           lambda_q2, lambda_k2, subln_g, w_o, norm2_g, w_gate, w_up, w_down, final_g):
    batch, seq, _ = x.shape
    layer = 0
    lam = (jnp.exp(jnp.sum(lambda_q1[layer].astype(F32) * lambda_k1[layer].astype(F32)))
           - jnp.exp(jnp.sum(lambda_q2[layer].astype(F32) * lambda_k2[layer].astype(F32)))
           + LAMBDA_INIT).reshape(1, 1)

    col_scale = jnp.concatenate([
        jnp.ones((POOL_WIDTH,), F32), jnp.full((ATTN_WIDTH,), HEAD_DIM ** -0.5, F32),
        jnp.ones((2 * ATTN_WIDTH,), F32)])
    w_in_b = (w_in[layer] * col_scale).astype(BF16)
    g1 = norm1_g[layer].reshape(1, D_MODEL)

    x2d = x.reshape(batch * seq, D_MODEL)
    zp, q, k, v = _inproj(x2d, g1, w_in_b, tm=512)
    zp_meta, _, k_meta, v_meta = _inproj(meta_tokens, g1, w_in_b, tm=N_META)

    bias = _bias_tiles(rel_bias)
    g2d = jnp.broadcast_to(subln_g[layer].astype(F32)[:, None], (V_DIM, TQ))
    y_attn = _attention(
        lam, q.reshape(batch, seq, ATTN_WIDTH), k.reshape(batch, seq, ATTN_WIDTH),
        v.reshape(batch, seq, ATTN_WIDTH), k_meta, v_meta, bias, g2d, batch, seq)

    out = _tail(
        x2d, zp, zp_meta, y_attn.reshape(batch * seq, N_HEADS * V_DIM),
        pool_w[layer].astype(BF16), pool_scale[layer].reshape(1, POOL_WIDTH),
        w_o[layer].astype(BF16), norm2_g[layer].reshape(1, D_MODEL),
        w_gate[layer].astype(BF16), w_up[layer].astype(BF16), w_down[layer].astype(BF16),
        final_g.reshape(1, D_MODEL), tm=512, seq=seq)
    return out.reshape(batch, seq, D_MODEL)
```

```python
import functools
import math

import jax
import jax.numpy as jnp
from jax import lax
from jax.experimental import pallas as pl
from jax.experimental.pallas import tpu as pltpu

F32 = jnp.float32
BF16 = jnp.bfloat16

D_MODEL = 1024
N_META = 16
POOL_WIDTH = 512
POOL_WINDOWS = (2, 4, 8, 16)
POOL_GROUP = 128
ATTN_WIDTH = 512
HEAD_DIM = 64
N_HEADS = 4
V_DIM = 128
IN_WIDTH = POOL_WIDTH + 3 * ATTN_WIDTH
REL_BUCKETS = 32
REL_MAX_DIST = 128
D_FF = 2816
NORM_EPS = 1e-6
SUBLN_EPS = 1e-5
LAMBDA_INIT = 0.8 - 0.6 * math.exp(-0.3 * 0)
HALO = 8

VMEM_LIMIT_BYTES = 56 * 1024 * 1024

TQ = 256
TK = 256
N_BIAS_TILES = 5
BIAS_ROWS = N_BIAS_TILES * TK + 2 * N_META
FF_CHUNK = 256
ONES_ROWS = 16
LOG2E = math.log2(math.e)


def _inproj_kernel(x_ref, g_ref, w_ref, zp_ref, q_ref, k_ref, v_ref):
    x = x_ref[...]
    ms = jnp.mean(x * x, axis=-1, keepdims=True)
    u = ((x * lax.rsqrt(ms + NORM_EPS)) * g_ref[...]).astype(BF16)

    def proj(lo, hi):
        return jnp.dot(u, w_ref[:, lo:hi], preferred_element_type=F32)

    o_q = POOL_WIDTH
    o_k = o_q + ATTN_WIDTH
    o_v = o_k + ATTN_WIDTH
    zp_ref[...] = proj(0, o_q)
    q_ref[...] = proj(o_q, o_k).astype(BF16)
    k_ref[...] = proj(o_k, o_v).astype(BF16)
    v_ref[...] = proj(o_v, IN_WIDTH).astype(BF16)


def _inproj(x2d, g, w_bf16, tm):
    rows = x2d.shape[0]
    row_spec = lambda width: pl.BlockSpec((tm, width), lambda i: (i, 0))
    return pl.pallas_call(
        _inproj_kernel,
        grid=(rows // tm,),
        in_specs=[
            row_spec(D_MODEL),
            pl.BlockSpec((1, D_MODEL), lambda i: (0, 0)),
            pl.BlockSpec((D_MODEL, IN_WIDTH), lambda i: (0, 0)),
        ],
        out_specs=[row_spec(POOL_WIDTH), row_spec(ATTN_WIDTH), row_spec(ATTN_WIDTH), row_spec(ATTN_WIDTH)],
        out_shape=[
            jax.ShapeDtypeStruct((rows, POOL_WIDTH), F32),
            jax.ShapeDtypeStruct((rows, ATTN_WIDTH), BF16),
            jax.ShapeDtypeStruct((rows, ATTN_WIDTH), BF16),
            jax.ShapeDtypeStruct((rows, ATTN_WIDTH), BF16),
        ],
        compiler_params=pltpu.CompilerParams(
            dimension_semantics=("arbitrary",), vmem_limit_bytes=VMEM_LIMIT_BYTES),
        name="inproj",
    )(x2d, g, w_bf16)


def _t5_bucket(rel):
    nb = REL_BUCKETS // 2
    ret = jnp.where(rel > 0, nb, 0)
    n = jnp.abs(rel)
    max_exact = nb // 2
    nf = jnp.maximum(n, 1).astype(F32)
    large = max_exact + (jnp.log(nf / max_exact) / math.log(REL_MAX_DIST / max_exact)
                         * (nb - max_exact)).astype(jnp.int32)
    large = jnp.minimum(large, nb - 1)
    return ret + jnp.where(n < max_exact, n, large)


def _bias_buckets():
    r = lax.broadcasted_iota(jnp.int32, (TK, TQ), 0)
    c = lax.broadcasted_iota(jnp.int32, (TK, TQ), 1)
    tiles = [(t - 2) * TK + r - c for t in range(N_BIAS_TILES)]
    rm = lax.broadcasted_iota(jnp.int32, (N_META, TQ), 0)
    cm = lax.broadcasted_iota(jnp.int32, (N_META, TQ), 1)
    metas = [rm - N_META - m * TQ - cm for m in range(2)]
    return _t5_bucket(jnp.concatenate(tiles + metas, axis=0))


def _bias_kernel(rb_ref, bkt_ref, o_ref):
    h = pl.program_id(0)
    bkt = bkt_ref[...]
    acc = jnp.zeros(bkt.shape, F32)
    for b in range(REL_BUCKETS):
        acc = jnp.where(bkt == b, rb_ref[b, h], acc)
    o_ref[...] = acc * LOG2E


def _bias_tiles(rel_bias):
    return pl.pallas_call(
        _bias_kernel,
        grid=(N_HEADS,),
        in_specs=[
            pl.BlockSpec(memory_space=pltpu.SMEM),
            pl.BlockSpec((BIAS_ROWS, TQ), lambda h: (0, 0)),
        ],
        out_specs=pl.BlockSpec((None, BIAS_ROWS, TQ), lambda h: (h, 0, 0)),
        out_shape=jax.ShapeDtypeStruct((N_HEADS, BIAS_ROWS, TQ), F32),
        compiler_params=pltpu.CompilerParams(dimension_semantics=("arbitrary",)),
        name="bias_tiles",
    )(rel_bias, _bias_buckets())


_NT = (((1,), (1,)), ((), ()))


def _attn_kernel(lam_ref, q_ref, k_ref, v_ref, km_ref, vm_ref, bias_ref, g_ref, o_ref,
                 s0_scr, s1_scr, p0_scr, p1_scr, vt_scr, vmt_scr, o0_scr, *, seq):
    n_kc = seq // TK
    n_qt = seq // TQ
    s_scr = (s0_scr, s1_scr)
    p_scr = (p0_scr, p1_scr)

    vt_scr[0:V_DIM, :] = v_ref[...].astype(F32).T.astype(BF16)
    vt_scr[V_DIM:, :] = jnp.ones((ONES_ROWS, seq), BF16)
    vm_pad = jnp.concatenate(
        [vm_ref[...].astype(F32), jnp.zeros((V_DIM - N_META, V_DIM), F32)], axis=0)
    vmt_scr[0:V_DIM, :] = vm_pad.T[:, :N_META].astype(BF16)
    vmt_scr[V_DIM:, :] = jnp.ones((ONES_ROWS, N_META), BF16)
    lam = lam_ref[0, 0]
    lane = lax.broadcasted_iota(jnp.int32, (TQ, 2 * HEAD_DIM), 1)

    def scores(qi, c):
        r0 = pl.multiple_of(qi * TQ, TQ)
        qf = q_ref[pl.ds(r0, TQ), :].astype(F32)
        keep = (lane < HEAD_DIM) if c == 0 else (lane >= HEAD_DIM)
        qm = jnp.where(keep, qf, 0.0).astype(BF16)
        s_scr[c][0:seq, :] = lax.dot_general(k_ref[...], qm, _NT, preferred_element_type=F32)
        s_scr[c][seq:, :] = lax.dot_general(km_ref[...], qm, _NT, preferred_element_type=F32)

    def softmax(qi, c):
        meta_row = pl.multiple_of(N_BIAS_TILES * TK + jnp.minimum(qi, 1) * N_META, N_META)

        def biased(kc):
            t = jnp.clip(kc - qi, -2, 2) + 2
            b0 = pl.multiple_of(t * TK, TK)
            return s_scr[c][kc * TK:(kc + 1) * TK, :] + bias_ref[pl.ds(b0, TK), :]

        sm = s_scr[c][seq:, :] + bias_ref[pl.ds(meta_row, N_META), :]
        m_acc = jnp.full((8, TQ), -jnp.inf, F32)
        for kc in range(n_kc):
            m_acc = jnp.maximum(m_acc, jnp.max(biased(kc).reshape(TK // 8, 8, TQ), axis=0))
        m = jnp.maximum(jnp.max(m_acc, axis=0, keepdims=True),
                        jnp.max(sm, axis=0, keepdims=True))
        for kc in range(n_kc):
            p_scr[c][kc * TK:(kc + 1) * TK, :] = jnp.exp2(biased(kc) - m).astype(BF16)
        p_scr[c][seq:, :] = jnp.exp2(sm - m).astype(BF16)

    def values(c):
        return (jnp.dot(vt_scr[...], p_scr[c][0:seq, :], preferred_element_type=F32)
                + jnp.dot(vmt_scr[...], p_scr[c][seq:, :], preferred_element_type=F32))

    def finalize(qi, oe0, oe1):
        r0 = pl.multiple_of(qi * TQ, TQ)
        o = (oe0[0:V_DIM] * (1.0 / oe0[V_DIM:V_DIM + 1])
             - oe1[0:V_DIM] * (lam / oe1[V_DIM:V_DIM + 1]))
        ms = jnp.mean(o * o, axis=0, keepdims=True)
        y = ((o * lax.rsqrt(ms + SUBLN_EPS)) * g_ref[...]) * (1.0 - LAMBDA_INIT)
        o_ref[pl.ds(r0, TQ), :] = y.T.astype(BF16)

    def step(qi, first, last):
        if not first:
            finalize(qi - 1, o0_scr[...], values(1))
        scores(qi, 1)
        softmax(qi, 0)
        o0_scr[...] = values(0)
        if not last:
            scores(qi + 1, 0)
        softmax(qi, 1)

    scores(0, 0)
    step(0, True, False)

    def loop_body(qi, carry):
        step(qi, False, False)
        return carry

    lax.fori_loop(1, n_qt - 1, loop_body, 0)
    step(n_qt - 1, False, True)
    finalize(n_qt - 1, o0_scr[...], values(1))


def _attention(lam, q, k, v, k_meta, v_meta, bias, g2d, batch, seq):
    head_rows = lambda: pl.BlockSpec((None, seq, V_DIM), lambda b, h: (b, 0, h))
    meta_spec = lambda: pl.BlockSpec((N_META, V_DIM), lambda b, h: (0, h))
    return pl.pallas_call(
        functools.partial(_attn_kernel, seq=seq),
        grid=(batch, N_HEADS),
        in_specs=[
            pl.BlockSpec(memory_space=pltpu.SMEM),
            head_rows(), head_rows(), head_rows(),
            meta_spec(), meta_spec(),
            pl.BlockSpec((None, BIAS_ROWS, TQ), lambda b, h: (h, 0, 0)),
            pl.BlockSpec((V_DIM, TQ), lambda b, h: (0, 0)),
        ],
        out_specs=head_rows(),
        out_shape=jax.ShapeDtypeStruct((batch, seq, N_HEADS * V_DIM), BF16),
        scratch_shapes=[
            pltpu.VMEM((seq + N_META, TQ), F32),
            pltpu.VMEM((seq + N_META, TQ), F32),
            pltpu.VMEM((seq + N_META, TQ), BF16),
            pltpu.VMEM((seq + N_META, TQ), BF16),
            pltpu.VMEM((V_DIM + ONES_ROWS, seq), BF16),
            pltpu.VMEM((V_DIM + ONES_ROWS, N_META), BF16),
            pltpu.VMEM((V_DIM + ONES_ROWS, TQ), F32),
        ],
        compiler_params=pltpu.CompilerParams(
            dimension_semantics=("arbitrary", "arbitrary"), vmem_limit_bytes=VMEM_LIMIT_BYTES),
        name="diff_attn",
    )(lam, q, k, v, k_meta, v_meta, bias, g2d)


def _rms(x, g):
    ms = jnp.mean(x * x, axis=-1, keepdims=True)
    return (x * lax.rsqrt(ms + NORM_EPS)) * g


def _tail_kernel(x_ref, zp_ref, zprev_ref, znext_ref, zmeta_ref, ya_ref, pw_ref, ps_ref, wo_ref,
                 g2_ref, wg_ref, wu_ref, wd_ref, gf_ref, o_ref, e_scr, *, tm, seq):
    tiles_per_seq = seq // tm
    t = lax.rem(pl.program_id(0), tiles_per_seq)
    p0 = t * tm
    prev = jnp.where(t == 0, zmeta_ref[...], zprev_ref[...])
    nxt = jnp.where(t == tiles_per_seq - 1, 0.0, znext_ref[...])
    zp = zp_ref[...]
    for g in range(len(POOL_WINDOWS)):
        lanes = slice(g * POOL_GROUP, (g + 1) * POOL_GROUP)
        e_scr[g, 0:HALO, :] = prev[:, lanes]
        e_scr[g, HALO:HALO + tm, :] = zp[:, lanes]
        e_scr[g, HALO + tm:2 * HALO + tm, :] = nxt[:, lanes]

    row = lax.broadcasted_iota(jnp.int32, (tm, POOL_GROUP), 0)
    y_groups = []
    for g, w in enumerate(POOL_WINDOWS):
        left = w // 2
        right = w - 1 - left
        win = e_scr[g, HALO - left:HALO - left + tm, :]
        for k in range(-left + 1, right + 1):
            win = win + e_scr[g, HALO + k:HALO + k + tm, :]
        over = jnp.maximum(p0 + row + (right - (seq - 1)), 0)
        cnt = (w - over).astype(F32)
        d = win / cnt - zp[:, g * POOL_GROUP:(g + 1) * POOL_GROUP]
        yg = jnp.dot(d.astype(BF16), pw_ref[g], preferred_element_type=F32)
        y_groups.append((yg * ps_ref[:, g * POOL_GROUP:(g + 1) * POOL_GROUP]).astype(BF16))
    ycat = jnp.concatenate(y_groups + [ya_ref[...]], axis=-1)

    h1 = x_ref[...] + jnp.dot(ycat, wo_ref[...], preferred_element_type=F32)
    f = _rms(h1, g2_ref[...]).astype(BF16)
    acc = h1
    for c in range(D_FF // FF_CHUNK):
        cols = slice(c * FF_CHUNK, (c + 1) * FF_CHUNK)
        gate = jnp.dot(f, wg_ref[:, cols], preferred_element_type=F32)
        up = jnp.dot(f, wu_ref[:, cols], preferred_element_type=F32)
        act = (gate * jax.nn.sigmoid(gate) * up).astype(BF16)
        acc = acc + jnp.dot(act, wd_ref[cols, :], preferred_element_type=F32)
    o_ref[...] = _rms(acc, gf_ref[...])


def _tail(x2d, zp, zp_meta, y_attn, pool_w, pool_scale, w_o, g2, w_gate, w_up, w_down, g_final,
          tm, seq):
    rows = x2d.shape[0]
    halo_blocks = tm // HALO
    last_halo_block = rows // HALO - 1
    const = lambda shape: pl.BlockSpec(shape, lambda i: (0,) * len(shape),
                                       pipeline_mode=pl.Buffered(1))
    return pl.pallas_call(
        functools.partial(_tail_kernel, tm=tm, seq=seq),
        grid=(rows // tm,),
        in_specs=[
            pl.BlockSpec((tm, D_MODEL), lambda i: (i, 0)),
            pl.BlockSpec((tm, POOL_WIDTH), lambda i: (i, 0)),
            pl.BlockSpec((HALO, POOL_WIDTH), lambda i: (jnp.maximum(i * halo_blocks - 1, 0), 0)),
            pl.BlockSpec((HALO, POOL_WIDTH),
                         lambda i: (jnp.minimum((i + 1) * halo_blocks, last_halo_block), 0)),
            pl.BlockSpec((HALO, POOL_WIDTH), lambda i: (N_META // HALO - 1, 0)),
            pl.BlockSpec((tm, N_HEADS * V_DIM), lambda i: (i, 0)),
            const((len(POOL_WINDOWS), POOL_GROUP, POOL_GROUP)),
            const((1, POOL_WIDTH)),
            const((D_MODEL, D_MODEL)),
            const((1, D_MODEL)),
            const((D_MODEL, D_FF)),
            const((D_MODEL, D_FF)),
            const((D_FF, D_MODEL)),
            const((1, D_MODEL)),
        ],
        out_specs=pl.BlockSpec((tm, D_MODEL), lambda i: (i, 0)),
        out_shape=jax.ShapeDtypeStruct((rows, D_MODEL), F32),
        scratch_shapes=[pltpu.VMEM((len(POOL_WINDOWS), tm + 2 * HALO, POOL_GROUP), F32)],
        compiler_params=pltpu.CompilerParams(
            dimension_semantics=("arbitrary",), vmem_limit_bytes=VMEM_LIMIT_BYTES),
        name="pool_oproj_ffn",
    )(x2d, zp, zp, zp, zp_meta, y_attn, pool_w, pool_scale, w_o, g2, w_gate, w_up, w_down, g_final)


def kernel(x, meta_tokens, rel_bias, norm1_g, w_in, pool_w, pool_scale, lambda_q1, lambda_k1,
           lambda_q2, lambda_k2, subln_g, w_o, norm2_g, w_gate, w_up, w_down, final_g):
    batch, seq, _ = x.shape
    layer = 0
    lam = (jnp.exp(jnp.sum(lambda_q1[layer].astype(F32) * lambda_k1[layer].astype(F32)))
           - jnp.exp(jnp.sum(lambda_q2[layer].astype(F32) * lambda_k2[layer].astype(F32)))
           + LAMBDA_INIT).reshape(1, 1)

    col_scale = jnp.concatenate([
        jnp.ones((POOL_WIDTH,), F32), jnp.full((ATTN_WIDTH,), HEAD_DIM ** -0.5 * LOG2E, F32),
        jnp.ones((2 * ATTN_WIDTH,), F32)])
    w_in_b = (w_in[layer] * col_scale).astype(BF16)
    g1 = norm1_g[layer].reshape(1, D_MODEL)

    x2d = x.reshape(batch * seq, D_MODEL)
    zp, q, k, v = _inproj(x2d, g1, w_in_b, tm=512)
    zp_meta, _, k_meta, v_meta = _inproj(meta_tokens, g1, w_in_b, tm=N_META)

    bias = _bias_tiles(rel_bias)
    g2d = jnp.broadcast_to(subln_g[layer].astype(F32)[:, None], (V_DIM, TQ))
    y_attn = _attention(
        lam, q.reshape(batch, seq, ATTN_WIDTH), k.reshape(batch, seq, ATTN_WIDTH),
        v.reshape(batch, seq, ATTN_WIDTH), k_meta, v_meta, bias, g2d, batch, seq)

    out = _tail(
        x2d, zp, zp_meta, y_attn.reshape(batch * seq, N_HEADS * V_DIM),
        pool_w[layer].astype(BF16), pool_scale[layer].reshape(1, POOL_WIDTH),
        w_o[layer].astype(BF16), norm2_g[layer].reshape(1, D_MODEL),
        w_gate[layer].astype(BF16), w_up[layer].astype(BF16), w_down[layer].astype(BF16),
        final_g.reshape(1, D_MODEL), tm=512, seq=seq)
    return out.reshape(batch, seq, D_MODEL)
```

```python
import functools
import math

import jax
import jax.numpy as jnp
from jax import lax
from jax.experimental import pallas as pl
from jax.experimental.pallas import tpu as pltpu

F32 = jnp.float32
BF16 = jnp.bfloat16

D_MODEL = 1024
N_META = 16
POOL_WIDTH = 512
POOL_WINDOWS = (2, 4, 8, 16)
POOL_GROUP = 128
ATTN_WIDTH = 512
HEAD_DIM = 64
N_HEADS = 4
V_DIM = 128
IN_WIDTH = POOL_WIDTH + 3 * ATTN_WIDTH
REL_BUCKETS = 32
REL_MAX_DIST = 128
D_FF = 2816
NORM_EPS = 1e-6
SUBLN_EPS = 1e-5
LAMBDA_INIT = 0.8 - 0.6 * math.exp(-0.3 * 0)
HALO = 8

VMEM_LIMIT_BYTES = 56 * 1024 * 1024

TQ = 256
TQM = 512
TK = 256
N_BIAS_TILES = 5
BIAS_ROWS = N_BIAS_TILES * TK + 2 * N_META
FF_CHUNK = 256
ONES_ROWS = 16
LOG2E = math.log2(math.e)


def _inproj_kernel(x_ref, g_ref, w_ref, zp_ref, q_ref, k_ref, v_ref):
    x = x_ref[...]
    ms = jnp.mean(x * x, axis=-1, keepdims=True)
    u = ((x * lax.rsqrt(ms + NORM_EPS)) * g_ref[...]).astype(BF16)

    def proj(lo, hi):
        return jnp.dot(u, w_ref[:, lo:hi], preferred_element_type=F32)

    o_q = POOL_WIDTH
    o_k = o_q + ATTN_WIDTH
    o_v = o_k + ATTN_WIDTH
    zp_ref[...] = proj(0, o_q)
    q_ref[...] = proj(o_q, o_k).astype(BF16)
    k_ref[...] = proj(o_k, o_v).astype(BF16)
    v_ref[...] = proj(o_v, IN_WIDTH).astype(BF16)


def _inproj(x2d, g, w_bf16, tm):
    rows = x2d.shape[0]
    row_spec = lambda width: pl.BlockSpec((tm, width), lambda i: (i, 0))
    return pl.pallas_call(
        _inproj_kernel,
        grid=(rows // tm,),
        in_specs=[
            row_spec(D_MODEL),
            pl.BlockSpec((1, D_MODEL), lambda i: (0, 0)),
            pl.BlockSpec((D_MODEL, IN_WIDTH), lambda i: (0, 0)),
        ],
        out_specs=[row_spec(POOL_WIDTH), row_spec(ATTN_WIDTH), row_spec(ATTN_WIDTH), row_spec(ATTN_WIDTH)],
        out_shape=[
            jax.ShapeDtypeStruct((rows, POOL_WIDTH), F32),
            jax.ShapeDtypeStruct((rows, ATTN_WIDTH), BF16),
            jax.ShapeDtypeStruct((rows, ATTN_WIDTH), BF16),
            jax.ShapeDtypeStruct((rows, ATTN_WIDTH), BF16),
        ],
        compiler_params=pltpu.CompilerParams(
            dimension_semantics=("arbitrary",), vmem_limit_bytes=VMEM_LIMIT_BYTES),
        name="inproj",
    )(x2d, g, w_bf16)


def _t5_bucket(rel):
    nb = REL_BUCKETS // 2
    ret = jnp.where(rel > 0, nb, 0)
    n = jnp.abs(rel)
    max_exact = nb // 2
    nf = jnp.maximum(n, 1).astype(F32)
    large = max_exact + (jnp.log(nf / max_exact) / math.log(REL_MAX_DIST / max_exact)
                         * (nb - max_exact)).astype(jnp.int32)
    large = jnp.minimum(large, nb - 1)
    return ret + jnp.where(n < max_exact, n, large)


def _bias_buckets():
    r = lax.broadcasted_iota(jnp.int32, (TK, TQ), 0)
    c = lax.broadcasted_iota(jnp.int32, (TK, TQ), 1)
    tiles = [(t - 2) * TK + r - c for t in range(N_BIAS_TILES)]
    rm = lax.broadcasted_iota(jnp.int32, (N_META, TQ), 0)
    cm = lax.broadcasted_iota(jnp.int32, (N_META, TQ), 1)
    metas = [rm - N_META - m * TQ - cm for m in range(2)]
    return _t5_bucket(jnp.concatenate(tiles + metas, axis=0))


def _bias_kernel(rb_ref, bkt_ref, o_ref):
    h = pl.program_id(0)
    bkt = bkt_ref[...]
    acc = jnp.zeros(bkt.shape, F32)
    for b in range(REL_BUCKETS):
        acc = jnp.where(bkt == b, rb_ref[b, h], acc)
    o_ref[...] = acc * LOG2E


def _bias_tiles(rel_bias):
    return pl.pallas_call(
        _bias_kernel,
        grid=(N_HEADS,),
        in_specs=[
            pl.BlockSpec(memory_space=pltpu.SMEM),
            pl.BlockSpec((BIAS_ROWS, TQ), lambda h: (0, 0)),
        ],
        out_specs=pl.BlockSpec((None, BIAS_ROWS, TQ), lambda h: (h, 0, 0)),
        out_shape=jax.ShapeDtypeStruct((N_HEADS, BIAS_ROWS, TQ), F32),
        compiler_params=pltpu.CompilerParams(dimension_semantics=("arbitrary",)),
        name="bias_tiles",
    )(rel_bias, _bias_buckets())


_NT = (((1,), (1,)), ((), ()))


def _attn_kernel(lam_ref, q_ref, k_ref, v_ref, km_ref, vm_ref, bias_ref, g_ref, o_ref,
                 s0_scr, s1_scr, p0_scr, p1_scr, vt_scr, vmt_scr, o0_scr, m_scr, *, seq):
    n_kc = seq // TK
    n_qt = seq // TQM
    halves = TQM // TQ
    s_scr = (s0_scr, s1_scr)
    p_scr = (p0_scr, p1_scr)

    vt_scr[0:V_DIM, :] = v_ref[...].astype(F32).T.astype(BF16)
    vt_scr[V_DIM:, :] = jnp.ones((ONES_ROWS, seq), BF16)
    vm_pad = jnp.concatenate(
        [vm_ref[...].astype(F32), jnp.zeros((V_DIM - N_META, V_DIM), F32)], axis=0)
    vmt_scr[0:V_DIM, :] = vm_pad.T[:, :N_META].astype(BF16)
    vmt_scr[V_DIM:, :] = jnp.ones((ONES_ROWS, N_META), BF16)
    lam = lam_ref[0, 0]
    lane = lax.broadcasted_iota(jnp.int32, (TQM, 2 * HEAD_DIM), 1)

    def scores(qi, c):
        r0 = qi * TQM
        qf = q_ref[pl.ds(r0, TQM), :].astype(F32)
        keep = (lane < HEAD_DIM) if c == 0 else (lane >= HEAD_DIM)
        qm = jnp.where(keep, qf, 0.0).astype(BF16)
        s_all = lax.dot_general(k_ref[...], qm, _NT, preferred_element_type=F32)
        s_meta = lax.dot_general(km_ref[...], qm, _NT, preferred_element_type=F32)
        for j in range(halves):
            qh = qi * halves + j
            cols = slice(j * TQ, (j + 1) * TQ)
            meta_row = N_BIAS_TILES * TK + min(qh, 1) * N_META
            sm = s_meta[:, cols] + bias_ref[pl.ds(meta_row, N_META), :]
            s_scr[c][seq:, cols] = sm
            m_acc = jnp.max(sm.reshape(N_META // 8, 8, TQ), axis=0)
            for kc in range(n_kc):
                b0 = (min(max(kc - qh, -2), 2) + 2) * TK
                sc = s_all[kc * TK:(kc + 1) * TK, cols] + bias_ref[pl.ds(b0, TK), :]
                s_scr[c][kc * TK:(kc + 1) * TK, cols] = sc
                m_acc = jnp.maximum(m_acc, jnp.max(sc.reshape(TK // 8, 8, TQ), axis=0))
            m_scr[c, :, cols] = m_acc

    def softmax(qi, c):
        m = jnp.max(m_scr[c], axis=0, keepdims=True)
        for kc in range(n_kc):
            rows = slice(kc * TK, (kc + 1) * TK)
            p_scr[c][rows, :] = jnp.exp2(s_scr[c][rows, :] - m).astype(BF16)
        p_scr[c][seq:, :] = jnp.exp2(s_scr[c][seq:, :] - m).astype(BF16)

    def values(c):
        return (jnp.dot(vt_scr[...], p_scr[c][0:seq, :], preferred_element_type=F32)
                + jnp.dot(vmt_scr[...], p_scr[c][seq:, :], preferred_element_type=F32))

    def finalize(qi, oe0, oe1):
        r0 = qi * TQM
        o = (oe0[0:V_DIM] * (1.0 / oe0[V_DIM:V_DIM + 1])
             - oe1[0:V_DIM] * (lam / oe1[V_DIM:V_DIM + 1]))
        ms = jnp.mean(o * o, axis=0, keepdims=True)
        y = ((o * lax.rsqrt(ms + SUBLN_EPS)) * g_ref[...]) * (1.0 - LAMBDA_INIT)
        o_ref[pl.ds(r0, TQM), :] = y.T.astype(BF16)

    def step(qi, first, last):
        if not first:
            finalize(qi - 1, o0_scr[...], values(1))
        scores(qi, 1)
        softmax(qi, 0)
        o0_scr[...] = values(0)
        if not last:
            scores(qi + 1, 0)
        softmax(qi, 1)

    scores(0, 0)
    for qi in range(n_qt):
        step(qi, qi == 0, qi == n_qt - 1)
    finalize(n_qt - 1, o0_scr[...], values(1))


def _attention(lam, q, k, v, k_meta, v_meta, bias, g2d, batch, seq):
    head_rows = lambda: pl.BlockSpec((None, seq, V_DIM), lambda b, h: (b, 0, h))
    meta_spec = lambda: pl.BlockSpec((N_META, V_DIM), lambda b, h: (0, h))
    return pl.pallas_call(
        functools.partial(_attn_kernel, seq=seq),
        grid=(batch, N_HEADS),
        in_specs=[
            pl.BlockSpec(memory_space=pltpu.SMEM),
            head_rows(), head_rows(), head_rows(),
            meta_spec(), meta_spec(),
            pl.BlockSpec((None, BIAS_ROWS, TQ), lambda b, h: (h, 0, 0)),
            pl.BlockSpec((V_DIM, TQM), lambda b, h: (0, 0)),
        ],
        out_specs=head_rows(),
        out_shape=jax.ShapeDtypeStruct((batch, seq, N_HEADS * V_DIM), BF16),
        scratch_shapes=[
            pltpu.VMEM((seq + N_META, TQM), F32),
            pltpu.VMEM((seq + N_META, TQM), F32),
            pltpu.VMEM((seq + N_META, TQM), BF16),
            pltpu.VMEM((seq + N_META, TQM), BF16),
            pltpu.VMEM((V_DIM + ONES_ROWS, seq), BF16),
            pltpu.VMEM((V_DIM + ONES_ROWS, N_META), BF16),
            pltpu.VMEM((V_DIM + ONES_ROWS, TQM), F32),
            pltpu.VMEM((2, 8, TQM), F32),
        ],
        compiler_params=pltpu.CompilerParams(
            dimension_semantics=("arbitrary", "arbitrary"), vmem_limit_bytes=VMEM_LIMIT_BYTES),
        name="diff_attn",
    )(lam, q, k, v, k_meta, v_meta, bias, g2d)


def _rms(x, g):
    ms = jnp.mean(x * x, axis=-1, keepdims=True)
    return (x * lax.rsqrt(ms + NORM_EPS)) * g


def _tail_kernel(x_ref, zp_ref, zprev_ref, znext_ref, zmeta_ref, ya_ref, pw_ref, ps_ref, wo_ref,
                 g2_ref, wg_ref, wu_ref, wd_ref, gf_ref, o_ref, e_scr, *, tm, seq):
    tiles_per_seq = seq // tm
    t = lax.rem(pl.program_id(0), tiles_per_seq)
    p0 = t * tm
    prev = jnp.where(t == 0, zmeta_ref[...], zprev_ref[...])
    nxt = jnp.where(t == tiles_per_seq - 1, 0.0, znext_ref[...])
    zp = zp_ref[...]
    for g in range(len(POOL_WINDOWS)):
        lanes = slice(g * POOL_GROUP, (g + 1) * POOL_GROUP)
        e_scr[g, 0:HALO, :] = prev[:, lanes]
        e_scr[g, HALO:HALO + tm, :] = zp[:, lanes]
        e_scr[g, HALO + tm:2 * HALO + tm, :] = nxt[:, lanes]

    row = lax.broadcasted_iota(jnp.int32, (tm, POOL_GROUP), 0)
    y_groups = []
    for g, w in enumerate(POOL_WINDOWS):
        left = w // 2
        right = w - 1 - left
        win = e_scr[g, HALO - left:HALO - left + tm, :]
        for k in range(-left + 1, right + 1):
            win = win + e_scr[g, HALO + k:HALO + k + tm, :]
        over = jnp.maximum(p0 + row + (right - (seq - 1)), 0)
        cnt = (w - over).astype(F32)
        d = win / cnt - zp[:, g * POOL_GROUP:(g + 1) * POOL_GROUP]
        yg = jnp.dot(d.astype(BF16), pw_ref[g], preferred_element_type=F32)
        y_groups.append((yg * ps_ref[:, g * POOL_GROUP:(g + 1) * POOL_GROUP]).astype(BF16))
    ycat = jnp.concatenate(y_groups + [ya_ref[...]], axis=-1)

    h1 = x_ref[...] + jnp.dot(ycat, wo_ref[...], preferred_element_type=F32)
    f = _rms(h1, g2_ref[...]).astype(BF16)
    acc = h1
    for c in range(D_FF // FF_CHUNK):
        cols = slice(c * FF_CHUNK, (c + 1) * FF_CHUNK)
        gate = jnp.dot(f, wg_ref[:, cols], preferred_element_type=F32)
        up = jnp.dot(f, wu_ref[:, cols], preferred_element_type=F32)
        act = (gate * jax.nn.sigmoid(gate) * up).astype(BF16)
        acc = acc + jnp.dot(act, wd_ref[cols, :], preferred_element_type=F32)
    o_ref[...] = _rms(acc, gf_ref[...])


def _tail(x2d, zp, zp_meta, y_attn, pool_w, pool_scale, w_o, g2, w_gate, w_up, w_down, g_final,
          tm, seq):
    rows = x2d.shape[0]
    halo_blocks = tm // HALO
    last_halo_block = rows // HALO - 1
    const = lambda shape: pl.BlockSpec(shape, lambda i: (0,) * len(shape),
                                       pipeline_mode=pl.Buffered(1))
    return pl.pallas_call(
        functools.partial(_tail_kernel, tm=tm, seq=seq),
        grid=(rows // tm,),
        in_specs=[
            pl.BlockSpec((tm, D_MODEL), lambda i: (i, 0)),
            pl.BlockSpec((tm, POOL_WIDTH), lambda i: (i, 0)),
            pl.BlockSpec((HALO, POOL_WIDTH), lambda i: (jnp.maximum(i * halo_blocks - 1, 0), 0)),
            pl.BlockSpec((HALO, POOL_WIDTH),
                         lambda i: (jnp.minimum((i + 1) * halo_blocks, last_halo_block), 0)),
            pl.BlockSpec((HALO, POOL_WIDTH), lambda i: (N_META // HALO - 1, 0)),
            pl.BlockSpec((tm, N_HEADS * V_DIM), lambda i: (i, 0)),
            const((len(POOL_WINDOWS), POOL_GROUP, POOL_GROUP)),
            const((1, POOL_WIDTH)),
            const((D_MODEL, D_MODEL)),
            const((1, D_MODEL)),
            const((D_MODEL, D_FF)),
            const((D_MODEL, D_FF)),
            const((D_FF, D_MODEL)),
            const((1, D_MODEL)),
        ],
        out_specs=pl.BlockSpec((tm, D_MODEL), lambda i: (i, 0)),
        out_shape=jax.ShapeDtypeStruct((rows, D_MODEL), F32),
        scratch_shapes=[pltpu.VMEM((len(POOL_WINDOWS), tm + 2 * HALO, POOL_GROUP), F32)],
        compiler_params=pltpu.CompilerParams(
            dimension_semantics=("arbitrary",), vmem_limit_bytes=VMEM_LIMIT_BYTES),
        name="pool_oproj_ffn",
    )(x2d, zp, zp, zp, zp_meta, y_attn, pool_w, pool_scale, w_o, g2, w_gate, w_up, w_down, g_final)


def kernel(x, meta_tokens, rel_bias, norm1_g, w_in, pool_w, pool_scale, lambda_q1, lambda_k1,
           lambda_q2, lambda_k2, subln_g, w_o, norm2_g, w_gate, w_up, w_down, final_g):
    batch, seq, _ = x.shape
    layer = 0
    lam = (jnp.exp(jnp.sum(lambda_q1[layer].astype(F32) * lambda_k1[layer].astype(F32)))
           - jnp.exp(jnp.sum(lambda_q2[layer].astype(F32) * lambda_k2[layer].astype(F32)))
           + LAMBDA_INIT).reshape(1, 1)

    col_scale = jnp.concatenate([
        jnp.ones((POOL_WIDTH,), F32), jnp.full((ATTN_WIDTH,), HEAD_DIM ** -0.5 * LOG2E, F32),
        jnp.ones((2 * ATTN_WIDTH,), F32)])
    w_in_b = (w_in[layer] * col_scale).astype(BF16)
    g1 = norm1_g[layer].reshape(1, D_MODEL)

    x2d = x.reshape(batch * seq, D_MODEL)
    zp, q, k, v = _inproj(x2d, g1, w_in_b, tm=512)
    zp_meta, _, k_meta, v_meta = _inproj(meta_tokens, g1, w_in_b, tm=N_META)

    bias = _bias_tiles(rel_bias)
    g2d = jnp.broadcast_to(subln_g[layer].astype(F32)[:, None], (V_DIM, TQM))
    y_attn = _attention(
        lam, q.reshape(batch, seq, ATTN_WIDTH), k.reshape(batch, seq, ATTN_WIDTH),
        v.reshape(batch, seq, ATTN_WIDTH), k_meta, v_meta, bias, g2d, batch, seq)

    out = _tail(
        x2d, zp, zp_meta, y_attn.reshape(batch * seq, N_HEADS * V_DIM),
        pool_w[layer].astype(BF16), pool_scale[layer].reshape(1, POOL_WIDTH),
        w_o[layer].astype(BF16), norm2_g[layer].reshape(1, D_MODEL),
        w_gate[layer].astype(BF16), w_up[layer].astype(BF16), w_down[layer].astype(BF16),
        final_g.reshape(1, D_MODEL), tm=512, seq=seq)
    return out.reshape(batch, seq, D_MODEL)
```

```python
import functools
import math

import jax
import jax.numpy as jnp
from jax import lax
from jax.experimental import pallas as pl
from jax.experimental.pallas import tpu as pltpu

F32 = jnp.float32
BF16 = jnp.bfloat16

D_MODEL = 1024
N_META = 16
META_PAD = 128
POOL_WIDTH = 512
POOL_WINDOWS = (2, 4, 8, 16)
POOL_GROUP = 128
ATTN_WIDTH = 512
HEAD_DIM = 64
N_HEADS = 4
V_DIM = 128
IN_WIDTH = POOL_WIDTH + 3 * ATTN_WIDTH
REL_BUCKETS = 32
REL_MAX_DIST = 128
D_FF = 2816
NORM_EPS = 1e-6
SUBLN_EPS = 1e-5
LAMBDA_INIT = 0.8 - 0.6 * math.exp(-0.3 * 0)
HALO = 8

VMEM_LIMIT_BYTES = 56 * 1024 * 1024

TQ = 256
TQM = 512
TK = 256
N_BIAS_TILES = 5
BIAS_ROWS = N_BIAS_TILES * TK + 2 * N_META
FF_CHUNK = 256
ONES_ROWS = 16
LOG2E = math.log2(math.e)


def _inproj_kernel(x_ref, g_ref, w_ref, zp_ref, qt_ref, k_ref, vt_ref):
    x = x_ref[...]
    ms = jnp.mean(x * x, axis=-1, keepdims=True)
    u = ((x * lax.rsqrt(ms + NORM_EPS)) * g_ref[...]).astype(BF16)

    def proj(lo, hi):
        return jnp.dot(u, w_ref[:, lo:hi], preferred_element_type=F32)

    o_q = POOL_WIDTH
    o_k = o_q + ATTN_WIDTH
    o_v = o_k + ATTN_WIDTH
    zp_ref[...] = proj(0, o_q)
    qt_ref[...] = proj(o_q, o_k).T.astype(BF16)
    k_ref[...] = proj(o_k, o_v).astype(BF16)
    vt_ref[...] = proj(o_v, IN_WIDTH).T.astype(BF16)


def _inproj(x2d, g, w_bf16, tm):
    rows = x2d.shape[0]
    row_spec = lambda width: pl.BlockSpec((tm, width), lambda i: (i, 0))
    col_spec = lambda width: pl.BlockSpec((width, tm), lambda i: (0, i))
    return pl.pallas_call(
        _inproj_kernel,
        grid=(rows // tm,),
        in_specs=[
            row_spec(D_MODEL),
            pl.BlockSpec((1, D_MODEL), lambda i: (0, 0)),
            pl.BlockSpec((D_MODEL, IN_WIDTH), lambda i: (0, 0)),
        ],
        out_specs=[row_spec(POOL_WIDTH), col_spec(ATTN_WIDTH), row_spec(ATTN_WIDTH),
                   col_spec(ATTN_WIDTH)],
        out_shape=[
            jax.ShapeDtypeStruct((rows, POOL_WIDTH), F32),
            jax.ShapeDtypeStruct((ATTN_WIDTH, rows), BF16),
            jax.ShapeDtypeStruct((rows, ATTN_WIDTH), BF16),
            jax.ShapeDtypeStruct((ATTN_WIDTH, rows), BF16),
        ],
        compiler_params=pltpu.CompilerParams(
            dimension_semantics=("arbitrary",), vmem_limit_bytes=VMEM_LIMIT_BYTES),
        name="inproj",
    )(x2d, g, w_bf16)


def _t5_bucket(rel):
    nb = REL_BUCKETS // 2
    ret = jnp.where(rel > 0, nb, 0)
    n = jnp.abs(rel)
    max_exact = nb // 2
    nf = jnp.maximum(n, 1).astype(F32)
    large = max_exact + (jnp.log(nf / max_exact) / math.log(REL_MAX_DIST / max_exact)
                         * (nb - max_exact)).astype(jnp.int32)
    large = jnp.minimum(large, nb - 1)
    return ret + jnp.where(n < max_exact, n, large)


def _bias_buckets():
    r = lax.broadcasted_iota(jnp.int32, (TK, TQ), 0)
    c = lax.broadcasted_iota(jnp.int32, (TK, TQ), 1)
    tiles = [(t - 2) * TK + r - c for t in range(N_BIAS_TILES)]
    rm = lax.broadcasted_iota(jnp.int32, (N_META, TQ), 0)
    cm = lax.broadcasted_iota(jnp.int32, (N_META, TQ), 1)
    metas = [rm - N_META - m * TQ - cm for m in range(2)]
    return _t5_bucket(jnp.concatenate(tiles + metas, axis=0))


def _bias_kernel(rb_ref, bkt_ref, o_ref):
    h = pl.program_id(0)
    bkt = bkt_ref[...]
    acc = jnp.zeros(bkt.shape, F32)
    for b in range(REL_BUCKETS):
        acc = jnp.where(bkt == b, rb_ref[b, h], acc)
    o_ref[...] = acc * LOG2E


def _bias_tiles(rel_bias):
    return pl.pallas_call(
        _bias_kernel,
        grid=(N_HEADS,),
        in_specs=[
            pl.BlockSpec(memory_space=pltpu.SMEM),
            pl.BlockSpec((BIAS_ROWS, TQ), lambda h: (0, 0)),
        ],
        out_specs=pl.BlockSpec((None, BIAS_ROWS, TQ), lambda h: (h, 0, 0)),
        out_shape=jax.ShapeDtypeStruct((N_HEADS, BIAS_ROWS, TQ), F32),
        compiler_params=pltpu.CompilerParams(dimension_semantics=("arbitrary",)),
        name="bias_tiles",
    )(rel_bias, _bias_buckets())


_NT = (((1,), (1,)), ((), ()))


def _attn_kernel(lam_ref, zero_ref, qt_ref, k_ref, vt_ref, km_ref, vmt_ref, bias_ref, g_ref, o_ref,
                 s0_scr, s1_scr, vt_scr, vmt_scr, o0_scr, *, seq):
    n_kc = seq // TK
    n_units = 2 * (seq // TQM)
    halves = TQM // TQ
    s_scr = (s0_scr, s1_scr)

    vt_scr[0:V_DIM, :] = vt_ref[...]
    vt_scr[V_DIM:, :] = jnp.ones((ONES_ROWS, seq), BF16)
    vmt_scr[0:V_DIM, :] = vmt_ref[:, 0:N_META]
    vmt_scr[V_DIM:, :] = jnp.ones((ONES_ROWS, N_META), BF16)
    lam = lam_ref[0, 0]
    z = zero_ref[0, 0]
    zero_half = jnp.zeros((HEAD_DIM, TQM), BF16)

    def masked_qt(u):
        qi, c = divmod(u, 2)
        qt = qt_ref[:, qi * TQM:(qi + 1) * TQM]
        if c == 0:
            return jnp.concatenate([qt[0:HEAD_DIM], zero_half], axis=0)
        return jnp.concatenate([zero_half, qt[HEAD_DIM:]], axis=0)

    def score_chunk(u, kc, qt, m_acc):
        qi = u // 2
        if kc < n_kc:
            rows = pl.ds(pl.multiple_of(z + kc * TK, TK), TK)
            s = jnp.dot(k_ref[kc * TK:(kc + 1) * TK, :], qt, preferred_element_type=F32)
        else:
            rows = pl.ds(pl.multiple_of(z + seq, N_META), N_META)
            s = jnp.dot(km_ref[...], qt, preferred_element_type=F32)
        out = []
        for j in range(halves):
            qh = qi * halves + j
            cols = slice(j * TQ, (j + 1) * TQ)
            if kc < n_kc:
                b0 = (min(max(kc - qh, -2), 2) + 2) * TK
                sc = s[:, cols] + bias_ref[b0:b0 + TK, :]
            else:
                b0 = N_BIAS_TILES * TK + min(qh, 1) * N_META
                sc = s[:, cols] + bias_ref[b0:b0 + N_META, :]
            s_scr[u % 2][rows, cols] = sc
            cm = jnp.max(sc.reshape(sc.shape[0] // 8, 8, TQ), axis=0)
            out.append(cm if m_acc is None else jnp.maximum(m_acc[j], cm))
        return out

    def value_chunk(u, kc, m, oe):
        if kc < n_kc:
            rows = pl.ds(pl.multiple_of(z + kc * TK, TK), TK)
            lhs = vt_scr[:, kc * TK:(kc + 1) * TK]
        else:
            rows = pl.ds(pl.multiple_of(z + seq, N_META), N_META)
            lhs = vmt_scr[...]
        p = jnp.exp2(s_scr[u % 2][rows, :] - m).astype(BF16)
        d = jnp.dot(lhs, p, preferred_element_type=F32)
        return d if oe is None else oe + d

    def finalize(qi, oe0, oe1):
        o = (oe0[0:V_DIM] * (1.0 / oe0[V_DIM:V_DIM + 1])
             - oe1[0:V_DIM] * (lam / oe1[V_DIM:V_DIM + 1]))
        ms = jnp.mean(o * o, axis=0, keepdims=True)
        y = ((o * lax.rsqrt(ms + SUBLN_EPS)) * g_ref[...]) * (1.0 - LAMBDA_INIT)
        o_ref[qi * TQM:(qi + 1) * TQM, :] = y.T.astype(BF16)

    def col_max(m_acc):
        return jnp.concatenate([jnp.max(a, axis=0, keepdims=True) for a in m_acc], axis=1)

    qt = masked_qt(0)
    m_acc = None
    for kc in range(n_kc + 1):
        m_acc = score_chunk(0, kc, qt, m_acc)
    m_next = col_max(m_acc)

    for u in range(n_units):
        m = m_next
        has_next = u + 1 < n_units
        if has_next:
            qt = masked_qt(u + 1)
        m_acc = None
        oe = None
        for kc in range(n_kc + 1):
            if has_next:
                m_acc = score_chunk(u + 1, kc, qt, m_acc)
            oe = value_chunk(u, kc, m, oe)
        if has_next:
            m_next = col_max(m_acc)
        if u % 2 == 0:
            o0_scr[...] = oe
        else:
            finalize(u // 2, o0_scr[...], oe)


def _attention(lam, qt, k, vt, k_meta, vt_meta, bias, g2d, batch, seq):
    head_rows = lambda: pl.BlockSpec((None, seq, V_DIM), lambda b, h: (b, 0, h))
    return pl.pallas_call(
        functools.partial(_attn_kernel, seq=seq),
        grid=(batch, N_HEADS),
        in_specs=[
            pl.BlockSpec(memory_space=pltpu.SMEM),
            pl.BlockSpec(memory_space=pltpu.SMEM),
            pl.BlockSpec((V_DIM, seq), lambda b, h: (h, b)),
            head_rows(),
            pl.BlockSpec((V_DIM, seq), lambda b, h: (h, b)),
            pl.BlockSpec((N_META, V_DIM), lambda b, h: (0, h)),
            pl.BlockSpec((V_DIM, META_PAD), lambda b, h: (h, 0)),
            pl.BlockSpec((None, BIAS_ROWS, TQ), lambda b, h: (h, 0, 0)),
            pl.BlockSpec((V_DIM, TQM), lambda b, h: (0, 0)),
        ],
        out_specs=head_rows(),
        out_shape=jax.ShapeDtypeStruct((batch, seq, N_HEADS * V_DIM), BF16),
        scratch_shapes=[
            pltpu.VMEM((seq + N_META, TQM), F32),
            pltpu.VMEM((seq + N_META, TQM), F32),
            pltpu.VMEM((V_DIM + ONES_ROWS, seq), BF16),
            pltpu.VMEM((V_DIM + ONES_ROWS, N_META), BF16),
            pltpu.VMEM((V_DIM + ONES_ROWS, TQM), F32),
        ],
        compiler_params=pltpu.CompilerParams(
            dimension_semantics=("arbitrary", "arbitrary"), vmem_limit_bytes=VMEM_LIMIT_BYTES),
        name="diff_attn",
    )(lam, jnp.zeros((1, 1), jnp.int32), qt, k, vt, k_meta, vt_meta, bias, g2d)


def _rms(x, g):
    ms = jnp.mean(x * x, axis=-1, keepdims=True)
    return (x * lax.rsqrt(ms + NORM_EPS)) * g


def _tail_kernel(x_ref, zp_ref, zprev_ref, znext_ref, zmeta_ref, ya_ref, pw_ref, ps_ref, wo_ref,
                 g2_ref, wg_ref, wu_ref, wd_ref, gf_ref, o_ref, e_scr, *, tm, seq):
    tiles_per_seq = seq // tm
    t = lax.rem(pl.program_id(0), tiles_per_seq)
    p0 = t * tm
    prev = jnp.where(t == 0, zmeta_ref[...], zprev_ref[...])
    nxt = jnp.where(t == tiles_per_seq - 1, 0.0, znext_ref[...])
    zp = zp_ref[...]
    for g in range(len(POOL_WINDOWS)):
        lanes = slice(g * POOL_GROUP, (g + 1) * POOL_GROUP)
        e_scr[g, 0:HALO, :] = prev[:, lanes]
        e_scr[g, HALO:HALO + tm, :] = zp[:, lanes]
        e_scr[g, HALO + tm:2 * HALO + tm, :] = nxt[:, lanes]

    row = lax.broadcasted_iota(jnp.int32, (tm, POOL_GROUP), 0)
    y_groups = []
    for g, w in enumerate(POOL_WINDOWS):
        left = w // 2
        right = w - 1 - left
        win = e_scr[g, HALO - left:HALO - left + tm, :]
        for k in range(-left + 1, right + 1):
            win = win + e_scr[g, HALO + k:HALO + k + tm, :]
        over = jnp.maximum(p0 + row + (right - (seq - 1)), 0)
        cnt = (w - over).astype(F32)
        d = win / cnt - zp[:, g * POOL_GROUP:(g + 1) * POOL_GROUP]
        yg = jnp.dot(d.astype(BF16), pw_ref[g], preferred_element_type=F32)
        y_groups.append((yg * ps_ref[:, g * POOL_GROUP:(g + 1) * POOL_GROUP]).astype(BF16))
    ycat = jnp.concatenate(y_groups + [ya_ref[...]], axis=-1)

    h1 = x_ref[...] + jnp.dot(ycat, wo_ref[...], preferred_element_type=F32)
    f = _rms(h1, g2_ref[...]).astype(BF16)
    acc = h1
    for c in range(D_FF // FF_CHUNK):
        cols = slice(c * FF_CHUNK, (c + 1) * FF_CHUNK)
        gate = jnp.dot(f, wg_ref[:, cols], preferred_element_type=F32)
        up = jnp.dot(f, wu_ref[:, cols], preferred_element_type=F32)
        act = (gate * jax.nn.sigmoid(gate) * up).astype(BF16)
        acc = acc + jnp.dot(act, wd_ref[cols, :], preferred_element_type=F32)
    o_ref[...] = _rms(acc, gf_ref[...])


def _tail(x2d, zp, zp_meta, y_attn, pool_w, pool_scale, w_o, g2, w_gate, w_up, w_down, g_final,
          tm, seq):
    rows = x2d.shape[0]
    halo_blocks = tm // HALO
    last_halo_block = rows // HALO - 1
    const = lambda shape: pl.BlockSpec(shape, lambda i: (0,) * len(shape),
                                       pipeline_mode=pl.Buffered(1))
    return pl.pallas_call(
        functools.partial(_tail_kernel, tm=tm, seq=seq),
        grid=(rows // tm,),
        in_specs=[
            pl.BlockSpec((tm, D_MODEL), lambda i: (i, 0)),
            pl.BlockSpec((tm, POOL_WIDTH), lambda i: (i, 0)),
            pl.BlockSpec((HALO, POOL_WIDTH), lambda i: (jnp.maximum(i * halo_blocks - 1, 0), 0)),
            pl.BlockSpec((HALO, POOL_WIDTH),
                         lambda i: (jnp.minimum((i + 1) * halo_blocks, last_halo_block), 0)),
            pl.BlockSpec((HALO, POOL_WIDTH), lambda i: (N_META // HALO - 1, 0)),
            pl.BlockSpec((tm, N_HEADS * V_DIM), lambda i: (i, 0)),
            const((len(POOL_WINDOWS), POOL_GROUP, POOL_GROUP)),
            const((1, POOL_WIDTH)),
            const((D_MODEL, D_MODEL)),
            const((1, D_MODEL)),
            const((D_MODEL, D_FF)),
            const((D_MODEL, D_FF)),
            const((D_FF, D_MODEL)),
            const((1, D_MODEL)),
        ],
        out_specs=pl.BlockSpec((tm, D_MODEL), lambda i: (i, 0)),
        out_shape=jax.ShapeDtypeStruct((rows, D_MODEL), F32),
        scratch_shapes=[pltpu.VMEM((len(POOL_WINDOWS), tm + 2 * HALO, POOL_GROUP), F32)],
        compiler_params=pltpu.CompilerParams(
            dimension_semantics=("arbitrary",), vmem_limit_bytes=VMEM_LIMIT_BYTES),
        name="pool_oproj_ffn",
    )(x2d, zp, zp, zp, zp_meta, y_attn, pool_w, pool_scale, w_o, g2, w_gate, w_up, w_down, g_final)


def kernel(x, meta_tokens, rel_bias, norm1_g, w_in, pool_w, pool_scale, lambda_q1, lambda_k1,
           lambda_q2, lambda_k2, subln_g, w_o, norm2_g, w_gate, w_up, w_down, final_g):
    batch, seq, _ = x.shape
    layer = 0
    lam = (jnp.exp(jnp.sum(lambda_q1[layer].astype(F32) * lambda_k1[layer].astype(F32)))
           - jnp.exp(jnp.sum(lambda_q2[layer].astype(F32) * lambda_k2[layer].astype(F32)))
           + LAMBDA_INIT).reshape(1, 1)

    col_scale = jnp.concatenate([
        jnp.ones((POOL_WIDTH,), F32), jnp.full((ATTN_WIDTH,), HEAD_DIM ** -0.5 * LOG2E, F32),
        jnp.ones((2 * ATTN_WIDTH,), F32)])
    w_in_b = (w_in[layer] * col_scale).astype(BF16)
    g1 = norm1_g[layer].reshape(1, D_MODEL)

    x2d = x.reshape(batch * seq, D_MODEL)
    zp, qt, k, vt = _inproj(x2d, g1, w_in_b, tm=512)
    meta_pad = jnp.pad(meta_tokens, ((0, META_PAD - N_META), (0, 0)))
    zp_meta, _, k_meta, vt_meta = _inproj(meta_pad, g1, w_in_b, tm=META_PAD)

    bias = _bias_tiles(rel_bias)
    g2d = jnp.broadcast_to(subln_g[layer].astype(F32)[:, None], (V_DIM, TQM))
    y_attn = _attention(
        lam, qt, k.reshape(batch, seq, ATTN_WIDTH),
        vt, k_meta, vt_meta, bias, g2d, batch, seq)

    out = _tail(
        x2d, zp, zp_meta, y_attn.reshape(batch * seq, N_HEADS * V_DIM),
        pool_w[layer].astype(BF16), pool_scale[layer].reshape(1, POOL_WIDTH),
        w_o[layer].astype(BF16), norm2_g[layer].reshape(1, D_MODEL),
        w_gate[layer].astype(BF16), w_up[layer].astype(BF16), w_down[layer].astype(BF16),
        final_g.reshape(1, D_MODEL), tm=512, seq=seq)
    return out.reshape(batch, seq, D_MODEL)
```

```python
import functools
import math

import jax
import jax.numpy as jnp
from jax import lax
from jax.experimental import pallas as pl
from jax.experimental.pallas import tpu as pltpu

F32 = jnp.float32
BF16 = jnp.bfloat16

D_MODEL = 1024
N_META = 16
META_PAD = 128
POOL_WIDTH = 512
POOL_WINDOWS = (2, 4, 8, 16)
POOL_GROUP = 128
ATTN_WIDTH = 512
HEAD_DIM = 64
N_HEADS = 4
V_DIM = 128
IN_WIDTH = POOL_WIDTH + 3 * ATTN_WIDTH
REL_BUCKETS = 32
REL_MAX_DIST = 128
D_FF = 2816
NORM_EPS = 1e-6
SUBLN_EPS = 1e-5
LAMBDA_INIT = 0.8 - 0.6 * math.exp(-0.3 * 0)
HALO = 8

VMEM_LIMIT_BYTES = 56 * 1024 * 1024

TQ = 256
TQM = 512
TK = 256
N_BIAS_TILES = 5
BIAS_ROWS = N_BIAS_TILES * TK + 2 * N_META
FF_CHUNK = 256
ONES_ROWS = 16
LOG2E = math.log2(math.e)


def _inproj_kernel(x_ref, g_ref, w_ref, zp_ref, qt_ref, k_ref, vt_ref):
    x = x_ref[...]
    ms = jnp.mean(x * x, axis=-1, keepdims=True)
    u = ((x * lax.rsqrt(ms + NORM_EPS)) * g_ref[...]).astype(BF16)

    def proj(lo, hi):
        return jnp.dot(u, w_ref[:, lo:hi], preferred_element_type=F32)

    o_q = POOL_WIDTH
    o_k = o_q + ATTN_WIDTH
    o_v = o_k + ATTN_WIDTH
    zp_ref[...] = proj(0, o_q)
    qt_ref[...] = proj(o_q, o_k).T.astype(BF16)
    k_ref[...] = proj(o_k, o_v).astype(BF16)
    vt_ref[...] = proj(o_v, IN_WIDTH).T.astype(BF16)


def _inproj(x2d, g, w_bf16, tm):
    rows = x2d.shape[0]
    row_spec = lambda width: pl.BlockSpec((tm, width), lambda i: (i, 0))
    col_spec = lambda width: pl.BlockSpec((width, tm), lambda i: (0, i))
    return pl.pallas_call(
        _inproj_kernel,
        grid=(rows // tm,),
        in_specs=[
            row_spec(D_MODEL),
            pl.BlockSpec((1, D_MODEL), lambda i: (0, 0)),
            pl.BlockSpec((D_MODEL, IN_WIDTH), lambda i: (0, 0)),
        ],
        out_specs=[row_spec(POOL_WIDTH), col_spec(ATTN_WIDTH), row_spec(ATTN_WIDTH),
                   col_spec(ATTN_WIDTH)],
        out_shape=[
            jax.ShapeDtypeStruct((rows, POOL_WIDTH), F32),
            jax.ShapeDtypeStruct((ATTN_WIDTH, rows), BF16),
            jax.ShapeDtypeStruct((rows, ATTN_WIDTH), BF16),
            jax.ShapeDtypeStruct((ATTN_WIDTH, rows), BF16),
        ],
        compiler_params=pltpu.CompilerParams(
            dimension_semantics=("arbitrary",), vmem_limit_bytes=VMEM_LIMIT_BYTES),
        name="inproj",
    )(x2d, g, w_bf16)


def _t5_bucket(rel):
    nb = REL_BUCKETS // 2
    ret = jnp.where(rel > 0, nb, 0)
    n = jnp.abs(rel)
    max_exact = nb // 2
    nf = jnp.maximum(n, 1).astype(F32)
    large = max_exact + (jnp.log(nf / max_exact) / math.log(REL_MAX_DIST / max_exact)
                         * (nb - max_exact)).astype(jnp.int32)
    large = jnp.minimum(large, nb - 1)
    return ret + jnp.where(n < max_exact, n, large)


def _bias_buckets():
    r = lax.broadcasted_iota(jnp.int32, (TK, TQ), 0)
    c = lax.broadcasted_iota(jnp.int32, (TK, TQ), 1)
    tiles = [(t - 2) * TK + r - c for t in range(N_BIAS_TILES)]
    rm = lax.broadcasted_iota(jnp.int32, (N_META, TQ), 0)
    cm = lax.broadcasted_iota(jnp.int32, (N_META, TQ), 1)
    metas = [rm - N_META - m * TQ - cm for m in range(2)]
    return _t5_bucket(jnp.concatenate(tiles + metas, axis=0))


def _bias_kernel(rb_ref, bkt_ref, o_ref):
    h = pl.program_id(0)
    bkt = bkt_ref[...]
    acc = jnp.zeros(bkt.shape, F32)
    for b in range(REL_BUCKETS):
        acc = jnp.where(bkt == b, rb_ref[b, h], acc)
    o_ref[...] = acc * LOG2E


def _bias_tiles(rel_bias):
    return pl.pallas_call(
        _bias_kernel,
        grid=(N_HEADS,),
        in_specs=[
            pl.BlockSpec(memory_space=pltpu.SMEM),
            pl.BlockSpec((BIAS_ROWS, TQ), lambda h: (0, 0)),
        ],
        out_specs=pl.BlockSpec((None, BIAS_ROWS, TQ), lambda h: (h, 0, 0)),
        out_shape=jax.ShapeDtypeStruct((N_HEADS, BIAS_ROWS, TQ), F32),
        compiler_params=pltpu.CompilerParams(dimension_semantics=("arbitrary",)),
        name="bias_tiles",
    )(rel_bias, _bias_buckets())


_NT = (((1,), (1,)), ((), ()))


def _attn_kernel(lam_ref, zero_ref, qt_ref, k_ref, vt_ref, km_ref, vmt_ref, bias_ref, g_ref, o_ref,
                 s0_scr, s1_scr, vt_scr, o0_scr, vmt_scr, *, seq):
    n_kc = seq // TK
    n_units = 2 * (seq // TQM)
    halves = TQM // TQ
    s_scr = (s0_scr, s1_scr)

    vt_scr[0:V_DIM, :] = vt_ref[...]
    vt_scr[V_DIM:, :] = jnp.ones((ONES_ROWS, seq), BF16)
    vmt_scr[0:V_DIM, :] = vmt_ref[:, 0:N_META]
    vmt_scr[V_DIM:, :] = jnp.ones((ONES_ROWS, N_META), BF16)
    lam = lam_ref[0, 0]
    z = zero_ref[0, 0]
    zero_half = jnp.zeros((HEAD_DIM, TQM), BF16)

    def masked_qt(u):
        qi, c = divmod(u, 2)
        qt = qt_ref[:, qi * TQM:(qi + 1) * TQM]
        if c == 0:
            return jnp.concatenate([qt[0:HEAD_DIM], zero_half], axis=0)
        return jnp.concatenate([zero_half, qt[HEAD_DIM:]], axis=0)

    def score_chunk(u, kc, qt, m_acc):
        qi = u // 2
        if kc < n_kc:
            rows = pl.ds(pl.multiple_of(z + kc * TK, TK), TK)
            s = jnp.dot(k_ref[kc * TK:(kc + 1) * TK, :], qt, preferred_element_type=F32)
        else:
            rows = pl.ds(pl.multiple_of(z + seq, N_META), N_META)
            s = jnp.dot(km_ref[0:N_META, :], qt, preferred_element_type=F32)
        out = []
        for j in range(halves):
            qh = qi * halves + j
            cols = slice(j * TQ, (j + 1) * TQ)
            if kc < n_kc:
                b0 = (min(max(kc - qh, -2), 2) + 2) * TK
                sc = s[:, cols] + bias_ref[b0:b0 + TK, :]
            else:
                b0 = N_BIAS_TILES * TK + min(qh, 1) * N_META
                sc = s[:, cols] + bias_ref[b0:b0 + N_META, :]
            s_scr[u % 2][rows, cols] = sc
            cm = jnp.max(sc.reshape(sc.shape[0] // 8, 8, TQ), axis=0)
            out.append(cm if m_acc is None else jnp.maximum(m_acc[j], cm))
        return out

    def value_chunk(u, kc, m, oe):
        if kc < n_kc:
            rows = pl.ds(pl.multiple_of(z + kc * TK, TK), TK)
            lhs = vt_scr[:, kc * TK:(kc + 1) * TK]
        else:
            rows = pl.ds(pl.multiple_of(z + seq, N_META), N_META)
            lhs = vmt_scr[...]
        p = jnp.exp2(s_scr[u % 2][rows, :] - m).astype(BF16)
        d = jnp.dot(lhs, p, preferred_element_type=F32)
        return d if oe is None else oe + d

    def finalize(qi, oe0, oe1):
        o = (oe0[0:V_DIM] * (1.0 / oe0[V_DIM:V_DIM + 1])
             - oe1[0:V_DIM] * (lam / oe1[V_DIM:V_DIM + 1]))
        ms = jnp.mean(o * o, axis=0, keepdims=True)
        y = ((o * lax.rsqrt(ms + SUBLN_EPS)) * g_ref[...]) * (1.0 - LAMBDA_INIT)
        o_ref[qi * TQM:(qi + 1) * TQM, :] = y.T.astype(BF16)

    def col_max(m_acc):
        return jnp.concatenate([jnp.max(a, axis=0, keepdims=True) for a in m_acc], axis=1)

    qt = masked_qt(0)
    m_acc = None
    for kc in range(n_kc + 1):
        m_acc = score_chunk(0, kc, qt, m_acc)
    m_next = col_max(m_acc)

    for u in range(n_units):
        m = m_next
        has_next = u + 1 < n_units
        if has_next:
            qt = masked_qt(u + 1)
        m_acc = None
        oe = None
        for kc in range(n_kc + 1):
            if has_next:
                m_acc = score_chunk(u + 1, kc, qt, m_acc)
            oe = value_chunk(u, kc, m, oe)
        if has_next:
            m_next = col_max(m_acc)
        if u % 2 == 0:
            o0_scr[...] = oe
        else:
            finalize(u // 2, o0_scr[...], oe)


def _attention(lam, qt, k, vt, k_meta, vt_meta, bias, g2d, batch, seq):
    head_rows = lambda: pl.BlockSpec((None, seq, V_DIM), lambda b, h: (b, 0, h))
    return pl.pallas_call(
        functools.partial(_attn_kernel, seq=seq),
        grid=(batch, N_HEADS),
        in_specs=[
            pl.BlockSpec(memory_space=pltpu.SMEM),
            pl.BlockSpec(memory_space=pltpu.SMEM),
            pl.BlockSpec((V_DIM, seq), lambda b, h: (h, b)),
            head_rows(),
            pl.BlockSpec((V_DIM, seq), lambda b, h: (h, b)),
            pl.BlockSpec((META_PAD, V_DIM), lambda b, h: (0, h)),
            pl.BlockSpec((V_DIM, META_PAD), lambda b, h: (h, 0)),
            pl.BlockSpec((None, BIAS_ROWS, TQ), lambda b, h: (h, 0, 0)),
            pl.BlockSpec((V_DIM, TQM), lambda b, h: (0, 0)),
        ],
        out_specs=head_rows(),
        out_shape=jax.ShapeDtypeStruct((batch, seq, N_HEADS * V_DIM), BF16),
        scratch_shapes=[
            pltpu.VMEM((seq + N_META, TQM), F32),
            pltpu.VMEM((seq + N_META, TQM), F32),
            pltpu.VMEM((V_DIM + ONES_ROWS, seq), BF16),
            pltpu.VMEM((V_DIM + ONES_ROWS, TQM), F32),
            pltpu.VMEM((V_DIM + ONES_ROWS, N_META), BF16),
        ],
        compiler_params=pltpu.CompilerParams(
            dimension_semantics=("arbitrary", "arbitrary"), vmem_limit_bytes=VMEM_LIMIT_BYTES),
        name="diff_attn",
    )(lam, jnp.zeros((1, 1), jnp.int32), qt, k, vt, k_meta, vt_meta, bias, g2d)


def _rms(x, g):
    ms = jnp.mean(x * x, axis=-1, keepdims=True)
    return (x * lax.rsqrt(ms + NORM_EPS)) * g


def _tail_kernel(x_ref, zp_ref, zprev_ref, znext_ref, zmeta_ref, ya_ref, pw_ref, ps_ref, wo_ref,
                 g2_ref, wg_ref, wu_ref, wd_ref, gf_ref, o_ref, e_scr, *, tm, seq):
    tiles_per_seq = seq // tm
    t = lax.rem(pl.program_id(0), tiles_per_seq)
    p0 = t * tm
    prev = jnp.where(t == 0, zmeta_ref[...], zprev_ref[...])
    nxt = jnp.where(t == tiles_per_seq - 1, 0.0, znext_ref[...])
    zp = zp_ref[...]
    for g in range(len(POOL_WINDOWS)):
        lanes = slice(g * POOL_GROUP, (g + 1) * POOL_GROUP)
        e_scr[g, 0:HALO, :] = prev[:, lanes]
        e_scr[g, HALO:HALO + tm, :] = zp[:, lanes]
        e_scr[g, HALO + tm:2 * HALO + tm, :] = nxt[:, lanes]

    row = lax.broadcasted_iota(jnp.int32, (tm, POOL_GROUP), 0)
    y_groups = []
    for g, w in enumerate(POOL_WINDOWS):
        left = w // 2
        right = w - 1 - left
        win = e_scr[g, HALO - left:HALO - left + tm, :]
        for k in range(-left + 1, right + 1):
            win = win + e_scr[g, HALO + k:HALO + k + tm, :]
        over = jnp.maximum(p0 + row + (right - (seq - 1)), 0)
        cnt = (w - over).astype(F32)
        d = win / cnt - zp[:, g * POOL_GROUP:(g + 1) * POOL_GROUP]
        yg = jnp.dot(d.astype(BF16), pw_ref[g], preferred_element_type=F32)
        y_groups.append((yg * ps_ref[:, g * POOL_GROUP:(g + 1) * POOL_GROUP]).astype(BF16))
    ycat = jnp.concatenate(y_groups + [ya_ref[...]], axis=-1)

    h1 = x_ref[...] + jnp.dot(ycat, wo_ref[...], preferred_element_type=F32)
    f = _rms(h1, g2_ref[...]).astype(BF16)
    acc = h1
    for c in range(D_FF // FF_CHUNK):
        cols = slice(c * FF_CHUNK, (c + 1) * FF_CHUNK)
        gate = jnp.dot(f, wg_ref[:, cols], preferred_element_type=F32)
        up = jnp.dot(f, wu_ref[:, cols], preferred_element_type=F32)
        act = (gate * jax.nn.sigmoid(gate) * up).astype(BF16)
        acc = acc + jnp.dot(act, wd_ref[cols, :], preferred_element_type=F32)
    o_ref[...] = _rms(acc, gf_ref[...])


def _tail(x2d, zp, zp_meta, y_attn, pool_w, pool_scale, w_o, g2, w_gate, w_up, w_down, g_final,
          tm, seq):
    rows = x2d.shape[0]
    halo_blocks = tm // HALO
    last_halo_block = rows // HALO - 1
    const = lambda shape: pl.BlockSpec(shape, lambda i: (0,) * len(shape),
                                       pipeline_mode=pl.Buffered(1))
    return pl.pallas_call(
        functools.partial(_tail_kernel, tm=tm, seq=seq),
        grid=(rows // tm,),
        in_specs=[
            pl.BlockSpec((tm, D_MODEL), lambda i: (i, 0)),
            pl.BlockSpec((tm, POOL_WIDTH), lambda i: (i, 0)),
            pl.BlockSpec((HALO, POOL_WIDTH), lambda i: (jnp.maximum(i * halo_blocks - 1, 0), 0)),
            pl.BlockSpec((HALO, POOL_WIDTH),
                         lambda i: (jnp.minimum((i + 1) * halo_blocks, last_halo_block), 0)),
            pl.BlockSpec((HALO, POOL_WIDTH), lambda i: (N_META // HALO - 1, 0)),
            pl.BlockSpec((tm, N_HEADS * V_DIM), lambda i: (i, 0)),
            const((len(POOL_WINDOWS), POOL_GROUP, POOL_GROUP)),
            const((1, POOL_WIDTH)),
            const((D_MODEL, D_MODEL)),
            const((1, D_MODEL)),
            const((D_MODEL, D_FF)),
            const((D_MODEL, D_FF)),
            const((D_FF, D_MODEL)),
            const((1, D_MODEL)),
        ],
        out_specs=pl.BlockSpec((tm, D_MODEL), lambda i: (i, 0)),
        out_shape=jax.ShapeDtypeStruct((rows, D_MODEL), F32),
        scratch_shapes=[pltpu.VMEM((len(POOL_WINDOWS), tm + 2 * HALO, POOL_GROUP), F32)],
        compiler_params=pltpu.CompilerParams(
            dimension_semantics=("arbitrary",), vmem_limit_bytes=VMEM_LIMIT_BYTES),
        name="pool_oproj_ffn",
    )(x2d, zp, zp, zp, zp_meta, y_attn, pool_w, pool_scale, w_o, g2, w_gate, w_up, w_down, g_final)


def kernel(x, meta_tokens, rel_bias, norm1_g, w_in, pool_w, pool_scale, lambda_q1, lambda_k1,
           lambda_q2, lambda_k2, subln_g, w_o, norm2_g, w_gate, w_up, w_down, final_g):
    batch, seq, _ = x.shape
    layer = 0
    lam = (jnp.exp(jnp.sum(lambda_q1[layer].astype(F32) * lambda_k1[layer].astype(F32)))
           - jnp.exp(jnp.sum(lambda_q2[layer].astype(F32) * lambda_k2[layer].astype(F32)))
           + LAMBDA_INIT).reshape(1, 1)

    col_scale = jnp.concatenate([
        jnp.ones((POOL_WIDTH,), F32), jnp.full((ATTN_WIDTH,), HEAD_DIM ** -0.5 * LOG2E, F32),
        jnp.ones((2 * ATTN_WIDTH,), F32)])
    w_in_b = (w_in[layer] * col_scale).astype(BF16)
    g1 = norm1_g[layer].reshape(1, D_MODEL)

    x2d = x.reshape(batch * seq, D_MODEL)
    zp, qt, k, vt = _inproj(x2d, g1, w_in_b, tm=512)
    meta_pad = jnp.pad(meta_tokens, ((0, META_PAD - N_META), (0, 0)))
    zp_meta, _, k_meta, vt_meta = _inproj(meta_pad, g1, w_in_b, tm=META_PAD)

    bias = _bias_tiles(rel_bias)
    g2d = jnp.broadcast_to(subln_g[layer].astype(F32)[:, None], (V_DIM, TQM))
    y_attn = _attention(
        lam, qt, k.reshape(batch, seq, ATTN_WIDTH),
        vt, k_meta, vt_meta, bias, g2d, batch, seq)

    out = _tail(
        x2d, zp, zp_meta, y_attn.reshape(batch * seq, N_HEADS * V_DIM),
        pool_w[layer].astype(BF16), pool_scale[layer].reshape(1, POOL_WIDTH),
        w_o[layer].astype(BF16), norm2_g[layer].reshape(1, D_MODEL),
        w_gate[layer].astype(BF16), w_up[layer].astype(BF16), w_down[layer].astype(BF16),
        final_g.reshape(1, D_MODEL), tm=512, seq=seq)
    return out.reshape(batch, seq, D_MODEL)
```

```python
import functools
import math

import jax
import jax.numpy as jnp
from jax import lax
from jax.experimental import pallas as pl
from jax.experimental.pallas import tpu as pltpu

F32 = jnp.float32
BF16 = jnp.bfloat16

D_MODEL = 1024
N_META = 16
META_PAD = 128
POOL_WIDTH = 512
POOL_WINDOWS = (2, 4, 8, 16)
POOL_GROUP = 128
ATTN_WIDTH = 512
HEAD_DIM = 64
N_HEADS = 4
V_DIM = 128
IN_WIDTH = POOL_WIDTH + 3 * ATTN_WIDTH
REL_BUCKETS = 32
REL_MAX_DIST = 128
D_FF = 2816
NORM_EPS = 1e-6
SUBLN_EPS = 1e-5
LAMBDA_INIT = 0.8 - 0.6 * math.exp(-0.3 * 0)
HALO = 8

VMEM_LIMIT_BYTES = 56 * 1024 * 1024

TQ = 256
TQM = 512
TK = 256
N_BIAS_TILES = 5
BIAS_ROWS = N_BIAS_TILES * TK + 2 * N_META
FF_CHUNK = 256
ONES_ROWS = 16
LOG2E = math.log2(math.e)
CHUNK_SKEW = 2


def _inproj_kernel(x_ref, g_ref, w_ref, zp_ref, qt_ref, k_ref, vt_ref):
    x = x_ref[...]
    ms = jnp.mean(x * x, axis=-1, keepdims=True)
    u = ((x * lax.rsqrt(ms + NORM_EPS)) * g_ref[...]).astype(BF16)

    def proj(lo, hi):
        return jnp.dot(u, w_ref[:, lo:hi], preferred_element_type=F32)

    o_q = POOL_WIDTH
    o_k = o_q + ATTN_WIDTH
    o_v = o_k + ATTN_WIDTH
    zp_ref[...] = proj(0, o_q)
    qt_ref[...] = proj(o_q, o_k).T.astype(BF16)
    k_ref[...] = proj(o_k, o_v).astype(BF16)
    vt_ref[...] = proj(o_v, IN_WIDTH).T.astype(BF16)


def _inproj(x2d, g, w_bf16, tm):
    rows = x2d.shape[0]
    row_spec = lambda width: pl.BlockSpec((tm, width), lambda i: (i, 0))
    col_spec = lambda width: pl.BlockSpec((width, tm), lambda i: (0, i))
    return pl.pallas_call(
        _inproj_kernel,
        grid=(rows // tm,),
        in_specs=[
            row_spec(D_MODEL),
            pl.BlockSpec((1, D_MODEL), lambda i: (0, 0)),
            pl.BlockSpec((D_MODEL, IN_WIDTH), lambda i: (0, 0)),
        ],
        out_specs=[row_spec(POOL_WIDTH), col_spec(ATTN_WIDTH), row_spec(ATTN_WIDTH),
                   col_spec(ATTN_WIDTH)],
        out_shape=[
            jax.ShapeDtypeStruct((rows, POOL_WIDTH), F32),
            jax.ShapeDtypeStruct((ATTN_WIDTH, rows), BF16),
            jax.ShapeDtypeStruct((rows, ATTN_WIDTH), BF16),
            jax.ShapeDtypeStruct((ATTN_WIDTH, rows), BF16),
        ],
        compiler_params=pltpu.CompilerParams(
            dimension_semantics=("arbitrary",), vmem_limit_bytes=VMEM_LIMIT_BYTES),
        name="inproj",
    )(x2d, g, w_bf16)


def _t5_bucket(rel):
    nb = REL_BUCKETS // 2
    ret = jnp.where(rel > 0, nb, 0)
    n = jnp.abs(rel)
    max_exact = nb // 2
    nf = jnp.maximum(n, 1).astype(F32)
    large = max_exact + (jnp.log(nf / max_exact) / math.log(REL_MAX_DIST / max_exact)
                         * (nb - max_exact)).astype(jnp.int32)
    large = jnp.minimum(large, nb - 1)
    return ret + jnp.where(n < max_exact, n, large)


def _bias_buckets():
    r = lax.broadcasted_iota(jnp.int32, (TK, TQ), 0)
    c = lax.broadcasted_iota(jnp.int32, (TK, TQ), 1)
    tiles = [(t - 2) * TK + r - c for t in range(N_BIAS_TILES)]
    rm = lax.broadcasted_iota(jnp.int32, (N_META, TQ), 0)
    cm = lax.broadcasted_iota(jnp.int32, (N_META, TQ), 1)
    metas = [rm - N_META - m * TQ - cm for m in range(2)]
    return _t5_bucket(jnp.concatenate(tiles + metas, axis=0))


def _bias_kernel(rb_ref, bkt_ref, o_ref):
    h = pl.program_id(0)

    def lookup(bkt):
        level = [rb_ref[b, h] * LOG2E for b in range(REL_BUCKETS)]
        for bit in range(REL_BUCKETS.bit_length() - 1):
            odd = (bkt & (1 << bit)) != 0
            level = [jnp.where(odd, level[2 * j + 1], level[2 * j]) for j in range(len(level) // 2)]
        return level[0]

    def fill(r0, rows, constant):
        if constant:
            slab = lookup(bkt_ref[r0:r0 + 8, :])
            o_ref[r0:r0 + rows, :] = jnp.broadcast_to(slab[None], (rows // 8, 8, TQ)).reshape(rows, TQ)
        else:
            o_ref[r0:r0 + rows, :] = lookup(bkt_ref[r0:r0 + rows, :])

    for t in range(N_BIAS_TILES):
        fill(t * TK, TK, constant=t in (0, N_BIAS_TILES - 1))
    for m in range(2):
        fill(N_BIAS_TILES * TK + m * N_META, N_META, constant=m == 1)


def _bias_tiles(rel_bias):
    return pl.pallas_call(
        _bias_kernel,
        grid=(N_HEADS,),
        in_specs=[
            pl.BlockSpec(memory_space=pltpu.SMEM),
            pl.BlockSpec((BIAS_ROWS, TQ), lambda h: (0, 0)),
        ],
        out_specs=pl.BlockSpec((None, BIAS_ROWS, TQ), lambda h: (h, 0, 0)),
        out_shape=jax.ShapeDtypeStruct((N_HEADS, BIAS_ROWS, TQ), F32),
        compiler_params=pltpu.CompilerParams(dimension_semantics=("arbitrary",)),
        name="bias_tiles",
    )(rel_bias, _bias_buckets())


_NT = (((1,), (1,)), ((), ()))


def _attn_kernel(lam_ref, zero_ref, qt_ref, k_ref, vt_ref, km_ref, vmt_ref, bias_ref, g_ref, o_ref,
                 s0_scr, s1_scr, vt_scr, o0_scr, vmt_scr, *, seq):
    n_kc = seq // TK
    n_units = 2 * (seq // TQM)
    halves = TQM // TQ
    s_scr = (s0_scr, s1_scr)

    vt_scr[0:V_DIM, :] = vt_ref[...]
    vt_scr[V_DIM:, :] = jnp.ones((ONES_ROWS, seq), BF16)
    vmt_scr[0:V_DIM, :] = vmt_ref[:, 0:N_META]
    vmt_scr[V_DIM:, :] = jnp.ones((ONES_ROWS, N_META), BF16)
    lam = lam_ref[0, 0]
    z = zero_ref[0, 0]
    zero_half = jnp.zeros((HEAD_DIM, TQM), BF16)

    def masked_qt(u):
        qi, c = divmod(u, 2)
        qt = qt_ref[:, qi * TQM:(qi + 1) * TQM]
        if c == 0:
            return jnp.concatenate([qt[0:HEAD_DIM], zero_half], axis=0)
        return jnp.concatenate([zero_half, qt[HEAD_DIM:]], axis=0)

    def score_chunk(u, kc, qt, m_acc):
        qi = u // 2
        if kc < n_kc:
            rows = pl.ds(pl.multiple_of(z + kc * TK, TK), TK)
            s = jnp.dot(k_ref[kc * TK:(kc + 1) * TK, :], qt, preferred_element_type=F32)
        else:
            rows = pl.ds(pl.multiple_of(z + seq, N_META), N_META)
            s = jnp.dot(km_ref[0:N_META, :], qt, preferred_element_type=F32)
        out = []
        for j in range(halves):
            qh = qi * halves + j
            cols = slice(j * TQ, (j + 1) * TQ)
            if kc < n_kc:
                b0 = (min(max(kc - qh, -2), 2) + 2) * TK
                sc = s[:, cols] + bias_ref[b0:b0 + TK, :]
            else:
                b0 = N_BIAS_TILES * TK + min(qh, 1) * N_META
                sc = s[:, cols] + bias_ref[b0:b0 + N_META, :]
            s_scr[u % 2][rows, cols] = sc
            cm = jnp.max(sc.reshape(sc.shape[0] // 8, 8, TQ), axis=0)
            out.append(cm if m_acc is None else jnp.maximum(m_acc[j], cm))
        return out

    def value_chunk(u, kc, m, oe):
        if kc < n_kc:
            rows = pl.ds(pl.multiple_of(z + kc * TK, TK), TK)
            lhs = vt_scr[:, kc * TK:(kc + 1) * TK]
        else:
            rows = pl.ds(pl.multiple_of(z + seq, N_META), N_META)
            lhs = vmt_scr[...]
        p = jnp.exp2(s_scr[u % 2][rows, :] - m).astype(BF16)
        d = jnp.dot(lhs, p, preferred_element_type=F32)
        return d if oe is None else oe + d

    def finalize(qi, oe0, oe1):
        o = (oe0[0:V_DIM] * (1.0 / oe0[V_DIM:V_DIM + 1])
             - oe1[0:V_DIM] * (lam / oe1[V_DIM:V_DIM + 1]))
        ms = jnp.mean(o * o, axis=0, keepdims=True)
        y = ((o * lax.rsqrt(ms + SUBLN_EPS)) * g_ref[...]) * (1.0 - LAMBDA_INIT)
        o_ref[qi * TQM:(qi + 1) * TQM, :] = y.T.astype(BF16)

    def col_max(m_acc):
        return jnp.concatenate([jnp.max(a, axis=0, keepdims=True) for a in m_acc], axis=1)

    qt = masked_qt(0)
    m_acc = None
    for kc in range(n_kc + 1):
        m_acc = score_chunk(0, kc, qt, m_acc)
    m_next = col_max(m_acc)

    for u in range(n_units):
        m = m_next
        has_next = u + 1 < n_units
        if has_next:
            qt = masked_qt(u + 1)
        m_acc = None
        oe = None
        for i in range(n_kc + 1 + CHUNK_SKEW):
            if has_next and i <= n_kc:
                m_acc = score_chunk(u + 1, i, qt, m_acc)
            if i >= CHUNK_SKEW:
                oe = value_chunk(u, i - CHUNK_SKEW, m, oe)
        if has_next:
            m_next = col_max(m_acc)
        if u % 2 == 0:
            o0_scr[...] = oe
        else:
            finalize(u // 2, o0_scr[...], oe)


def _attention(lam, qt, k, vt, k_meta, vt_meta, bias, g2d, batch, seq):
    head_rows = lambda: pl.BlockSpec((None, seq, V_DIM), lambda b, h: (b, 0, h))
    return pl.pallas_call(
        functools.partial(_attn_kernel, seq=seq),
        grid=(batch, N_HEADS),
        in_specs=[
            pl.BlockSpec(memory_space=pltpu.SMEM),
            pl.BlockSpec(memory_space=pltpu.SMEM),
            pl.BlockSpec((V_DIM, seq), lambda b, h: (h, b)),
            head_rows(),
            pl.BlockSpec((V_DIM, seq), lambda b, h: (h, b)),
            pl.BlockSpec((META_PAD, V_DIM), lambda b, h: (0, h)),
            pl.BlockSpec((V_DIM, META_PAD), lambda b, h: (h, 0)),
            pl.BlockSpec((None, BIAS_ROWS, TQ), lambda b, h: (h, 0, 0)),
            pl.BlockSpec((V_DIM, TQM), lambda b, h: (0, 0)),
        ],
        out_specs=head_rows(),
        out_shape=jax.ShapeDtypeStruct((batch, seq, N_HEADS * V_DIM), BF16),
        scratch_shapes=[
            pltpu.VMEM((seq + N_META, TQM), F32),
            pltpu.VMEM((seq + N_META, TQM), F32),
            pltpu.VMEM((V_DIM + ONES_ROWS, seq), BF16),
            pltpu.VMEM((V_DIM + ONES_ROWS, TQM), F32),
            pltpu.VMEM((V_DIM + ONES_ROWS, N_META), BF16),
        ],
        compiler_params=pltpu.CompilerParams(
            dimension_semantics=("arbitrary", "arbitrary"), vmem_limit_bytes=VMEM_LIMIT_BYTES),
        name="diff_attn",
    )(lam, jnp.zeros((1, 1), jnp.int32), qt, k, vt, k_meta, vt_meta, bias, g2d)


def _rms(x, g):
    ms = jnp.mean(x * x, axis=-1, keepdims=True)
    return (x * lax.rsqrt(ms + NORM_EPS)) * g


def _tail_kernel(x_ref, zp_ref, zprev_ref, znext_ref, zmeta_ref, ya_ref, pw_ref, ps_ref, wo_ref,
                 g2_ref, wg_ref, wu_ref, wd_ref, gf_ref, o_ref, e_scr, *, tm, seq):
    tiles_per_seq = seq // tm
    t = lax.rem(pl.program_id(0), tiles_per_seq)
    p0 = t * tm
    prev = jnp.where(t == 0, zmeta_ref[...], zprev_ref[...])
    nxt = jnp.where(t == tiles_per_seq - 1, 0.0, znext_ref[...])
    zp = zp_ref[...]
    for g in range(len(POOL_WINDOWS)):
        lanes = slice(g * POOL_GROUP, (g + 1) * POOL_GROUP)
        e_scr[g, 0:HALO, :] = prev[:, lanes]
        e_scr[g, HALO:HALO + tm, :] = zp[:, lanes]
        e_scr[g, HALO + tm:2 * HALO + tm, :] = nxt[:, lanes]

    row = lax.broadcasted_iota(jnp.int32, (tm, POOL_GROUP), 0)
    y_groups = []
    for g, w in enumerate(POOL_WINDOWS):
        left = w // 2
        right = w - 1 - left
        win = e_scr[g, HALO - left:HALO - left + tm, :]
        for k in range(-left + 1, right + 1):
            win = win + e_scr[g, HALO + k:HALO + k + tm, :]
        over = jnp.maximum(p0 + row + (right - (seq - 1)), 0)
        cnt = (w - over).astype(F32)
        d = win / cnt - zp[:, g * POOL_GROUP:(g + 1) * POOL_GROUP]
        yg = jnp.dot(d.astype(BF16), pw_ref[g], preferred_element_type=F32)
        y_groups.append((yg * ps_ref[:, g * POOL_GROUP:(g + 1) * POOL_GROUP]).astype(BF16))
    ycat = jnp.concatenate(y_groups + [ya_ref[...]], axis=-1)

    h1 = x_ref[...] + jnp.dot(ycat, wo_ref[...], preferred_element_type=F32)
    f = _rms(h1, g2_ref[...]).astype(BF16)
    acc = h1
    for c in range(D_FF // FF_CHUNK):
        cols = slice(c * FF_CHUNK, (c + 1) * FF_CHUNK)
        gate = jnp.dot(f, wg_ref[:, cols], preferred_element_type=F32)
        up = jnp.dot(f, wu_ref[:, cols], preferred_element_type=F32)
        act = (gate * jax.nn.sigmoid(gate) * up).astype(BF16)
        acc = acc + jnp.dot(act, wd_ref[cols, :], preferred_element_type=F32)
    o_ref[...] = _rms(acc, gf_ref[...])


def _tail(x2d, zp, zp_meta, y_attn, pool_w, pool_scale, w_o, g2, w_gate, w_up, w_down, g_final,
          tm, seq):
    rows = x2d.shape[0]
    halo_blocks = tm // HALO
    last_halo_block = rows // HALO - 1
    const = lambda shape: pl.BlockSpec(shape, lambda i: (0,) * len(shape),
                                       pipeline_mode=pl.Buffered(1))
    return pl.pallas_call(
        functools.partial(_tail_kernel, tm=tm, seq=seq),
        grid=(rows // tm,),
        in_specs=[
            pl.BlockSpec((tm, D_MODEL), lambda i: (i, 0)),
            pl.BlockSpec((tm, POOL_WIDTH), lambda i: (i, 0)),
            pl.BlockSpec((HALO, POOL_WIDTH), lambda i: (jnp.maximum(i * halo_blocks - 1, 0), 0)),
            pl.BlockSpec((HALO, POOL_WIDTH),
                         lambda i: (jnp.minimum((i + 1) * halo_blocks, last_halo_block), 0)),
            pl.BlockSpec((HALO, POOL_WIDTH), lambda i: (N_META // HALO - 1, 0)),
            pl.BlockSpec((tm, N_HEADS * V_DIM), lambda i: (i, 0)),
            const((len(POOL_WINDOWS), POOL_GROUP, POOL_GROUP)),
            const((1, POOL_WIDTH)),
            const((D_MODEL, D_MODEL)),
            const((1, D_MODEL)),
            const((D_MODEL, D_FF)),
            const((D_MODEL, D_FF)),
            const((D_FF, D_MODEL)),
            const((1, D_MODEL)),
        ],
        out_specs=pl.BlockSpec((tm, D_MODEL), lambda i: (i, 0)),
        out_shape=jax.ShapeDtypeStruct((rows, D_MODEL), F32),
        scratch_shapes=[pltpu.VMEM((len(POOL_WINDOWS), tm + 2 * HALO, POOL_GROUP), F32)],
        compiler_params=pltpu.CompilerParams(
            dimension_semantics=("arbitrary",), vmem_limit_bytes=VMEM_LIMIT_BYTES),
        name="pool_oproj_ffn",
    )(x2d, zp, zp, zp, zp_meta, y_attn, pool_w, pool_scale, w_o, g2, w_gate, w_up, w_down, g_final)


def kernel(x, meta_tokens, rel_bias, norm1_g, w_in, pool_w, pool_scale, lambda_q1, lambda_k1,
           lambda_q2, lambda_k2, subln_g, w_o, norm2_g, w_gate, w_up, w_down, final_g):
    batch, seq, _ = x.shape
    layer = 0
    lam = (jnp.exp(jnp.sum(lambda_q1[layer].astype(F32) * lambda_k1[layer].astype(F32)))
           - jnp.exp(jnp.sum(lambda_q2[layer].astype(F32) * lambda_k2[layer].astype(F32)))
           + LAMBDA_INIT).reshape(1, 1)

    col_scale = jnp.concatenate([
        jnp.ones((POOL_WIDTH,), F32), jnp.full((ATTN_WIDTH,), HEAD_DIM ** -0.5 * LOG2E, F32),
        jnp.ones((2 * ATTN_WIDTH,), F32)])
    w_in_b = (w_in[layer] * col_scale).astype(BF16)
    g1 = norm1_g[layer].reshape(1, D_MODEL)

    x2d = x.reshape(batch * seq, D_MODEL)
    zp, qt, k, vt = _inproj(x2d, g1, w_in_b, tm=512)
    meta_pad = jnp.pad(meta_tokens, ((0, META_PAD - N_META), (0, 0)))
    zp_meta, _, k_meta, vt_meta = _inproj(meta_pad, g1, w_in_b, tm=META_PAD)

    bias = _bias_tiles(rel_bias)
    g2d = jnp.broadcast_to(subln_g[layer].astype(F32)[:, None], (V_DIM, TQM))
    y_attn = _attention(
        lam, qt, k.reshape(batch, seq, ATTN_WIDTH),
        vt, k_meta, vt_meta, bias, g2d, batch, seq)

    out = _tail(
        x2d, zp, zp_meta, y_attn.reshape(batch * seq, N_HEADS * V_DIM),
        pool_w[layer].astype(BF16), pool_scale[layer].reshape(1, POOL_WIDTH),
        w_o[layer].astype(BF16), norm2_g[layer].reshape(1, D_MODEL),
        w_gate[layer].astype(BF16), w_up[layer].astype(BF16), w_down[layer].astype(BF16),
        final_g.reshape(1, D_MODEL), tm=512, seq=seq)
    return out.reshape(batch, seq, D_MODEL)
```

```python
import functools
import math

import jax
import jax.numpy as jnp
from jax import lax
from jax.experimental import pallas as pl
from jax.experimental.pallas import tpu as pltpu

F32 = jnp.float32
BF16 = jnp.bfloat16

D_MODEL = 1024
N_META = 16
META_PAD = 128
POOL_WIDTH = 512
POOL_WINDOWS = (2, 4, 8, 16)
POOL_GROUP = 128
ATTN_WIDTH = 512
HEAD_DIM = 64
N_HEADS = 4
V_DIM = 128
IN_WIDTH = POOL_WIDTH + 3 * ATTN_WIDTH
REL_BUCKETS = 32
REL_MAX_DIST = 128
D_FF = 2816
NORM_EPS = 1e-6
SUBLN_EPS = 1e-5
LAMBDA_INIT = 0.8 - 0.6 * math.exp(-0.3 * 0)
HALO = 8

VMEM_LIMIT_BYTES = 56 * 1024 * 1024

TQ = 256
TQM = 512
TK = 256
N_BIAS_TILES = 5
BIAS_ROWS = N_BIAS_TILES * TK + 2 * N_META
FF_CHUNK = 256
ONES_ROWS = 16
LOG2E = math.log2(math.e)
CHUNK_SKEW = 2


def _inproj_kernel(x_ref, g_ref, w_ref, zp_ref, qt_ref, k_ref, vt_ref):
    x = x_ref[...]
    ms = jnp.mean(x * x, axis=-1, keepdims=True)
    u = ((x * lax.rsqrt(ms + NORM_EPS)) * g_ref[...]).astype(BF16)

    def proj(lo, hi):
        return jnp.dot(u, w_ref[:, lo:hi], preferred_element_type=F32)

    o_q = POOL_WIDTH
    o_k = o_q + ATTN_WIDTH
    o_v = o_k + ATTN_WIDTH
    zp_ref[...] = proj(0, o_q)
    qt_ref[...] = proj(o_q, o_k).T.astype(BF16)
    k_ref[...] = proj(o_k, o_v).astype(BF16)
    vt_ref[...] = proj(o_v, IN_WIDTH).T.astype(BF16)


def _inproj(x2d, g, w_bf16, tm):
    rows = x2d.shape[0]
    row_spec = lambda width: pl.BlockSpec((tm, width), lambda i: (i, 0))
    col_spec = lambda width: pl.BlockSpec((width, tm), lambda i: (0, i))
    return pl.pallas_call(
        _inproj_kernel,
        grid=(rows // tm,),
        in_specs=[
            row_spec(D_MODEL),
            pl.BlockSpec((1, D_MODEL), lambda i: (0, 0)),
            pl.BlockSpec((D_MODEL, IN_WIDTH), lambda i: (0, 0)),
        ],
        out_specs=[row_spec(POOL_WIDTH), col_spec(ATTN_WIDTH), row_spec(ATTN_WIDTH),
                   col_spec(ATTN_WIDTH)],
        out_shape=[
            jax.ShapeDtypeStruct((rows, POOL_WIDTH), F32),
            jax.ShapeDtypeStruct((ATTN_WIDTH, rows), BF16),
            jax.ShapeDtypeStruct((rows, ATTN_WIDTH), BF16),
            jax.ShapeDtypeStruct((ATTN_WIDTH, rows), BF16),
        ],
        compiler_params=pltpu.CompilerParams(
            dimension_semantics=("arbitrary",), vmem_limit_bytes=VMEM_LIMIT_BYTES),
        name="inproj",
    )(x2d, g, w_bf16)


def _t5_bucket(rel):
    nb = REL_BUCKETS // 2
    ret = jnp.where(rel > 0, nb, 0)
    n = jnp.abs(rel)
    max_exact = nb // 2
    nf = jnp.maximum(n, 1).astype(F32)
    large = max_exact + (jnp.log(nf / max_exact) / math.log(REL_MAX_DIST / max_exact)
                         * (nb - max_exact)).astype(jnp.int32)
    large = jnp.minimum(large, nb - 1)
    return ret + jnp.where(n < max_exact, n, large)


def _bias_buckets():
    r = lax.broadcasted_iota(jnp.int32, (TK, TQ), 0)
    c = lax.broadcasted_iota(jnp.int32, (TK, TQ), 1)
    tiles = [(t - 2) * TK + r - c for t in range(N_BIAS_TILES)]
    rm = lax.broadcasted_iota(jnp.int32, (N_META, TQ), 0)
    cm = lax.broadcasted_iota(jnp.int32, (N_META, TQ), 1)
    metas = [rm - N_META - m * TQ - cm for m in range(2)]
    return _t5_bucket(jnp.concatenate(tiles + metas, axis=0))


def _bias_kernel(rb_ref, bkt_ref, o_ref):
    h = pl.program_id(0)

    def lookup(bkt):
        level = [rb_ref[b, h] * LOG2E for b in range(REL_BUCKETS)]
        for bit in range(REL_BUCKETS.bit_length() - 1):
            odd = (bkt & (1 << bit)) != 0
            level = [jnp.where(odd, level[2 * j + 1], level[2 * j]) for j in range(len(level) // 2)]
        return level[0]

    def fill(r0, rows, constant):
        if constant:
            slab = lookup(bkt_ref[r0:r0 + 8, :])
            o_ref[r0:r0 + rows, :] = jnp.broadcast_to(slab[None], (rows // 8, 8, TQ)).reshape(rows, TQ)
        else:
            o_ref[r0:r0 + rows, :] = lookup(bkt_ref[r0:r0 + rows, :])

    for t in range(N_BIAS_TILES):
        fill(t * TK, TK, constant=t in (0, N_BIAS_TILES - 1))
    for m in range(2):
        fill(N_BIAS_TILES * TK + m * N_META, N_META, constant=m == 1)


def _bias_tiles(rel_bias):
    return pl.pallas_call(
        _bias_kernel,
        grid=(N_HEADS,),
        in_specs=[
            pl.BlockSpec(memory_space=pltpu.SMEM),
            pl.BlockSpec((BIAS_ROWS, TQ), lambda h: (0, 0)),
        ],
        out_specs=pl.BlockSpec((None, BIAS_ROWS, TQ), lambda h: (h, 0, 0)),
        out_shape=jax.ShapeDtypeStruct((N_HEADS, BIAS_ROWS, TQ), F32),
        compiler_params=pltpu.CompilerParams(dimension_semantics=("arbitrary",)),
        name="bias_tiles",
    )(rel_bias, _bias_buckets())


_NT = (((1,), (1,)), ((), ()))


def _attn_kernel(lam_ref, zero_ref, qt_ref, k_ref, vt_ref, km_ref, vmt_ref, bias_ref, g_ref, o_ref,
                 s0_scr, s1_scr, s2_scr, vt_scr, o0_scr, vmt_scr, *, seq):
    n_kc = seq // TK
    n_units = 2 * (seq // TQM)
    halves = TQM // TQ
    s_scr = (s0_scr, s1_scr, s2_scr)

    vt_scr[0:V_DIM, :] = vt_ref[...]
    vt_scr[V_DIM:, :] = jnp.ones((ONES_ROWS, seq), BF16)
    vmt_scr[0:V_DIM, :] = vmt_ref[:, 0:N_META]
    vmt_scr[V_DIM:, :] = jnp.ones((ONES_ROWS, N_META), BF16)
    lam = lam_ref[0, 0]
    z = zero_ref[0, 0]
    zero_half = jnp.zeros((HEAD_DIM, TQM), BF16)

    def masked_qt(u):
        qi, c = divmod(u, 2)
        qt = qt_ref[:, qi * TQM:(qi + 1) * TQM]
        if c == 0:
            return jnp.concatenate([qt[0:HEAD_DIM], zero_half], axis=0)
        return jnp.concatenate([zero_half, qt[HEAD_DIM:]], axis=0)

    def score_chunk(u, kc, qt, m_acc):
        qi = u // 2
        if kc < n_kc:
            rows = pl.ds(pl.multiple_of(z + kc * TK, TK), TK)
            s = jnp.dot(k_ref[kc * TK:(kc + 1) * TK, :], qt, preferred_element_type=F32)
        else:
            rows = pl.ds(pl.multiple_of(z + seq, N_META), N_META)
            s = jnp.dot(km_ref[0:N_META, :], qt, preferred_element_type=F32)
        out = []
        for j in range(halves):
            qh = qi * halves + j
            cols = slice(j * TQ, (j + 1) * TQ)
            if kc < n_kc:
                b0 = (min(max(kc - qh, -2), 2) + 2) * TK
                sc = s[:, cols] + bias_ref[b0:b0 + TK, :]
            else:
                b0 = N_BIAS_TILES * TK + min(qh, 1) * N_META
                sc = s[:, cols] + bias_ref[b0:b0 + N_META, :]
            s_scr[u % len(s_scr)][rows, cols] = sc
            cm = jnp.max(sc.reshape(sc.shape[0] // 8, 8, TQ), axis=0)
            out.append(cm if m_acc is None else jnp.maximum(m_acc[j], cm))
        return out

    def value_chunk(u, kc, m, oe):
        if kc < n_kc:
            rows = pl.ds(pl.multiple_of(z + kc * TK, TK), TK)
            lhs = vt_scr[:, kc * TK:(kc + 1) * TK]
        else:
            rows = pl.ds(pl.multiple_of(z + seq, N_META), N_META)
            lhs = vmt_scr[...]
        p = jnp.exp2(s_scr[u % len(s_scr)][rows, :] - m).astype(BF16)
        d = jnp.dot(lhs, p, preferred_element_type=F32)
        return d if oe is None else oe + d

    def finalize(qi, oe0, oe1):
        o = (oe0[0:V_DIM] * (1.0 / oe0[V_DIM:V_DIM + 1])
             - oe1[0:V_DIM] * (lam / oe1[V_DIM:V_DIM + 1]))
        ms = jnp.mean(o * o, axis=0, keepdims=True)
        y = ((o * lax.rsqrt(ms + SUBLN_EPS)) * g_ref[...]) * (1.0 - LAMBDA_INIT)
        o_ref[qi * TQM:(qi + 1) * TQM, :] = y.T.astype(BF16)

    def col_max(m_acc):
        return jnp.concatenate([jnp.max(a, axis=0, keepdims=True) for a in m_acc], axis=1)

    n_chunks = n_kc + 1
    total = n_units * n_chunks
    lag = n_chunks + CHUNK_SKEW
    score_order = [n_kc] + list(range(n_kc))
    meta_pos = -lag % n_chunks
    value_order = list(range(n_kc))
    value_order.insert(meta_pos, n_kc)
    col_max_of = {}
    qt = m_acc = oe = None
    for g in range(total + lag):
        if g < total:
            u, i = divmod(g, n_chunks)
            if i == 0:
                qt, m_acc = masked_qt(u), None
            m_acc = score_chunk(u, score_order[i], qt, m_acc)
            if i == n_chunks - 1:
                col_max_of[u] = col_max(m_acc)
        if g >= lag:
            u, i = divmod(g - lag, n_chunks)
            oe = value_chunk(u, value_order[i], col_max_of[u], None if i == 0 else oe)
            if i == n_chunks - 1:
                if u % 2 == 0:
                    o0_scr[...] = oe
                else:
                    finalize(u // 2, o0_scr[...], oe)


def _attention(lam, qt, k, vt, k_meta, vt_meta, bias, g2d, batch, seq):
    head_rows = lambda: pl.BlockSpec((None, seq, V_DIM), lambda b, h: (b, 0, h))
    return pl.pallas_call(
        functools.partial(_attn_kernel, seq=seq),
        grid=(batch, N_HEADS),
        in_specs=[
            pl.BlockSpec(memory_space=pltpu.SMEM),
            pl.BlockSpec(memory_space=pltpu.SMEM),
            pl.BlockSpec((V_DIM, seq), lambda b, h: (h, b)),
            head_rows(),
            pl.BlockSpec((V_DIM, seq), lambda b, h: (h, b)),
            pl.BlockSpec((META_PAD, V_DIM), lambda b, h: (0, h)),
            pl.BlockSpec((V_DIM, META_PAD), lambda b, h: (h, 0)),
            pl.BlockSpec((None, BIAS_ROWS, TQ), lambda b, h: (h, 0, 0)),
            pl.BlockSpec((V_DIM, TQM), lambda b, h: (0, 0)),
        ],
        out_specs=head_rows(),
        out_shape=jax.ShapeDtypeStruct((batch, seq, N_HEADS * V_DIM), BF16),
        scratch_shapes=[
            pltpu.VMEM((seq + N_META, TQM), F32),
            pltpu.VMEM((seq + N_META, TQM), F32),
            pltpu.VMEM((seq + N_META, TQM), F32),
            pltpu.VMEM((V_DIM + ONES_ROWS, seq), BF16),
            pltpu.VMEM((V_DIM + ONES_ROWS, TQM), F32),
            pltpu.VMEM((V_DIM + ONES_ROWS, N_META), BF16),
        ],
        compiler_params=pltpu.CompilerParams(
            dimension_semantics=("arbitrary", "arbitrary"), vmem_limit_bytes=VMEM_LIMIT_BYTES),
        name="diff_attn",
    )(lam, jnp.zeros((1, 1), jnp.int32), qt, k, vt, k_meta, vt_meta, bias, g2d)


def _rms(x, g):
    ms = jnp.mean(x * x, axis=-1, keepdims=True)
    return (x * lax.rsqrt(ms + NORM_EPS)) * g


def _tail_kernel(x_ref, zp_ref, zprev_ref, znext_ref, zmeta_ref, ya_ref, pw_ref, ps_ref, wo_ref,
                 g2_ref, wg_ref, wu_ref, wd_ref, gf_ref, o_ref, e_scr, *, tm, seq):
    tiles_per_seq = seq // tm
    t = lax.rem(pl.program_id(0), tiles_per_seq)
    p0 = t * tm
    prev = jnp.where(t == 0, zmeta_ref[...], zprev_ref[...])
    nxt = jnp.where(t == tiles_per_seq - 1, 0.0, znext_ref[...])
    zp = zp_ref[...]
    for g in range(len(POOL_WINDOWS)):
        lanes = slice(g * POOL_GROUP, (g + 1) * POOL_GROUP)
        e_scr[g, 0:HALO, :] = prev[:, lanes]
        e_scr[g, HALO:HALO + tm, :] = zp[:, lanes]
        e_scr[g, HALO + tm:2 * HALO + tm, :] = nxt[:, lanes]

    row = lax.broadcasted_iota(jnp.int32, (tm, POOL_GROUP), 0)
    y_groups = []
    for g, w in enumerate(POOL_WINDOWS):
        left = w // 2
        right = w - 1 - left
        win = e_scr[g, HALO - left:HALO - left + tm, :]
        for k in range(-left + 1, right + 1):
            win = win + e_scr[g, HALO + k:HALO + k + tm, :]
        over = jnp.maximum(p0 + row + (right - (seq - 1)), 0)
        cnt = (w - over).astype(F32)
        d = win / cnt - zp[:, g * POOL_GROUP:(g + 1) * POOL_GROUP]
        yg = jnp.dot(d.astype(BF16), pw_ref[g], preferred_element_type=F32)
        y_groups.append((yg * ps_ref[:, g * POOL_GROUP:(g + 1) * POOL_GROUP]).astype(BF16))
    ycat = jnp.concatenate(y_groups + [ya_ref[...]], axis=-1)

    h1 = x_ref[...] + jnp.dot(ycat, wo_ref[...], preferred_element_type=F32)
    f = _rms(h1, g2_ref[...]).astype(BF16)
    acc = h1
    for c in range(D_FF // FF_CHUNK):
        cols = slice(c * FF_CHUNK, (c + 1) * FF_CHUNK)
        gate = jnp.dot(f, wg_ref[:, cols], preferred_element_type=F32)
        up = jnp.dot(f, wu_ref[:, cols], preferred_element_type=F32)
        act = (gate * jax.nn.sigmoid(gate) * up).astype(BF16)
        acc = acc + jnp.dot(act, wd_ref[cols, :], preferred_element_type=F32)
    o_ref[...] = _rms(acc, gf_ref[...])


def _tail(x2d, zp, zp_meta, y_attn, pool_w, pool_scale, w_o, g2, w_gate, w_up, w_down, g_final,
          tm, seq):
    rows = x2d.shape[0]
    halo_blocks = tm // HALO
    last_halo_block = rows // HALO - 1
    const = lambda shape: pl.BlockSpec(shape, lambda i: (0,) * len(shape),
                                       pipeline_mode=pl.Buffered(1))
    return pl.pallas_call(
        functools.partial(_tail_kernel, tm=tm, seq=seq),
        grid=(rows // tm,),
        in_specs=[
            pl.BlockSpec((tm, D_MODEL), lambda i: (i, 0)),
            pl.BlockSpec((tm, POOL_WIDTH), lambda i: (i, 0)),
            pl.BlockSpec((HALO, POOL_WIDTH), lambda i: (jnp.maximum(i * halo_blocks - 1, 0), 0)),
            pl.BlockSpec((HALO, POOL_WIDTH),
                         lambda i: (jnp.minimum((i + 1) * halo_blocks, last_halo_block), 0)),
            pl.BlockSpec((HALO, POOL_WIDTH), lambda i: (N_META // HALO - 1, 0)),
            pl.BlockSpec((tm, N_HEADS * V_DIM), lambda i: (i, 0)),
            const((len(POOL_WINDOWS), POOL_GROUP, POOL_GROUP)),
            const((1, POOL_WIDTH)),
            const((D_MODEL, D_MODEL)),
            const((1, D_MODEL)),
            const((D_MODEL, D_FF)),
            const((D_MODEL, D_FF)),
            const((D_FF, D_MODEL)),
            const((1, D_MODEL)),
        ],
        out_specs=pl.BlockSpec((tm, D_MODEL), lambda i: (i, 0)),
        out_shape=jax.ShapeDtypeStruct((rows, D_MODEL), F32),
        scratch_shapes=[pltpu.VMEM((len(POOL_WINDOWS), tm + 2 * HALO, POOL_GROUP), F32)],
        compiler_params=pltpu.CompilerParams(
            dimension_semantics=("arbitrary",), vmem_limit_bytes=VMEM_LIMIT_BYTES),
        name="pool_oproj_ffn",
    )(x2d, zp, zp, zp, zp_meta, y_attn, pool_w, pool_scale, w_o, g2, w_gate, w_up, w_down, g_final)


def kernel(x, meta_tokens, rel_bias, norm1_g, w_in, pool_w, pool_scale, lambda_q1, lambda_k1,
           lambda_q2, lambda_k2, subln_g, w_o, norm2_g, w_gate, w_up, w_down, final_g):
    batch, seq, _ = x.shape
    layer = 0
    lam = (jnp.exp(jnp.sum(lambda_q1[layer].astype(F32) * lambda_k1[layer].astype(F32)))
           - jnp.exp(jnp.sum(lambda_q2[layer].astype(F32) * lambda_k2[layer].astype(F32)))
           + LAMBDA_INIT).reshape(1, 1)

    col_scale = jnp.concatenate([
        jnp.ones((POOL_WIDTH,), F32), jnp.full((ATTN_WIDTH,), HEAD_DIM ** -0.5 * LOG2E, F32),
        jnp.ones((2 * ATTN_WIDTH,), F32)])
    w_in_b = (w_in[layer] * col_scale).astype(BF16)
    g1 = norm1_g[layer].reshape(1, D_MODEL)

    x2d = x.reshape(batch * seq, D_MODEL)
    zp, qt, k, vt = _inproj(x2d, g1, w_in_b, tm=512)
    meta_pad = jnp.pad(meta_tokens, ((0, META_PAD - N_META), (0, 0)))
    zp_meta, _, k_meta, vt_meta = _inproj(meta_pad, g1, w_in_b, tm=META_PAD)

    bias = _bias_tiles(rel_bias)
    g2d = jnp.broadcast_to(subln_g[layer].astype(F32)[:, None], (V_DIM, TQM))
    y_attn = _attention(
        lam, qt, k.reshape(batch, seq, ATTN_WIDTH),
        vt, k_meta, vt_meta, bias, g2d, batch, seq)

    out = _tail(
        x2d, zp, zp_meta, y_attn.reshape(batch * seq, N_HEADS * V_DIM),
        pool_w[layer].astype(BF16), pool_scale[layer].reshape(1, POOL_WIDTH),
        w_o[layer].astype(BF16), norm2_g[layer].reshape(1, D_MODEL),
        w_gate[layer].astype(BF16), w_up[layer].astype(BF16), w_down[layer].astype(BF16),
        final_g.reshape(1, D_MODEL), tm=512, seq=seq)
    return out.reshape(batch, seq, D_MODEL)
```

```python
import functools
import math

import jax
import jax.numpy as jnp
from jax import lax
from jax.experimental import pallas as pl
from jax.experimental.pallas import tpu as pltpu

F32 = jnp.float32
BF16 = jnp.bfloat16

D_MODEL = 1024
N_META = 16
META_PAD = 128
POOL_WIDTH = 512
POOL_WINDOWS = (2, 4, 8, 16)
POOL_GROUP = 128
ATTN_WIDTH = 512
HEAD_DIM = 64
N_HEADS = 4
V_DIM = 128
IN_WIDTH = POOL_WIDTH + 3 * ATTN_WIDTH
REL_BUCKETS = 32
REL_MAX_DIST = 128
D_FF = 2816
NORM_EPS = 1e-6
SUBLN_EPS = 1e-5
LAMBDA_INIT = 0.8 - 0.6 * math.exp(-0.3 * 0)
HALO = 8

VMEM_LIMIT_BYTES = 56 * 1024 * 1024

TQ = 256
TQM = 512
TK = 256
N_BIAS_TILES = 5
BIAS_ROWS = N_BIAS_TILES * TK + 2 * N_META
FF_CHUNK = 256
ONES_ROWS = 16
LOG2E = math.log2(math.e)
CHUNK_SKEW = 2
TAIL_PARTS = 1


def _inproj_kernel(x_ref, g_ref, w_ref, zp_ref, qt_ref, k_ref, vt_ref):
    x = x_ref[...]
    ms = jnp.mean(x * x, axis=-1, keepdims=True)
    u = ((x * lax.rsqrt(ms + NORM_EPS)) * g_ref[...]).astype(BF16)

    def proj(lo, hi):
        return jnp.dot(u, w_ref[:, lo:hi], preferred_element_type=F32)

    o_q = POOL_WIDTH
    o_k = o_q + ATTN_WIDTH
    o_v = o_k + ATTN_WIDTH
    zp_ref[...] = proj(0, o_q)
    qt_ref[...] = proj(o_q, o_k).T.astype(BF16)
    k_ref[...] = proj(o_k, o_v).astype(BF16)
    vt_ref[...] = proj(o_v, IN_WIDTH).T.astype(BF16)


def _inproj(x2d, g, w_bf16, tm):
    rows = x2d.shape[0]
    row_spec = lambda width: pl.BlockSpec((tm, width), lambda i: (i, 0))
    col_spec = lambda width: pl.BlockSpec((width, tm), lambda i: (0, i))
    return pl.pallas_call(
        _inproj_kernel,
        grid=(rows // tm,),
        in_specs=[
            row_spec(D_MODEL),
            pl.BlockSpec((1, D_MODEL), lambda i: (0, 0)),
            pl.BlockSpec((D_MODEL, IN_WIDTH), lambda i: (0, 0)),
        ],
        out_specs=[row_spec(POOL_WIDTH), col_spec(ATTN_WIDTH), row_spec(ATTN_WIDTH),
                   col_spec(ATTN_WIDTH)],
        out_shape=[
            jax.ShapeDtypeStruct((rows, POOL_WIDTH), F32),
            jax.ShapeDtypeStruct((ATTN_WIDTH, rows), BF16),
            jax.ShapeDtypeStruct((rows, ATTN_WIDTH), BF16),
            jax.ShapeDtypeStruct((ATTN_WIDTH, rows), BF16),
        ],
        compiler_params=pltpu.CompilerParams(
            dimension_semantics=("arbitrary",), vmem_limit_bytes=VMEM_LIMIT_BYTES),
        name="inproj",
    )(x2d, g, w_bf16)


def _t5_bucket(rel):
    nb = REL_BUCKETS // 2
    ret = jnp.where(rel > 0, nb, 0)
    n = jnp.abs(rel)
    max_exact = nb // 2
    nf = jnp.maximum(n, 1).astype(F32)
    large = max_exact + (jnp.log(nf / max_exact) / math.log(REL_MAX_DIST / max_exact)
                         * (nb - max_exact)).astype(jnp.int32)
    large = jnp.minimum(large, nb - 1)
    return ret + jnp.where(n < max_exact, n, large)


def _bias_buckets():
    r = lax.broadcasted_iota(jnp.int32, (TK, TQ), 0)
    c = lax.broadcasted_iota(jnp.int32, (TK, TQ), 1)
    tiles = [(t - 2) * TK + r - c for t in range(N_BIAS_TILES)]
    rm = lax.broadcasted_iota(jnp.int32, (N_META, TQ), 0)
    cm = lax.broadcasted_iota(jnp.int32, (N_META, TQ), 1)
    metas = [rm - N_META - m * TQ - cm for m in range(2)]
    return _t5_bucket(jnp.concatenate(tiles + metas, axis=0))


def _bias_kernel(rb_ref, bkt_ref, o_ref):
    h = pl.program_id(0)

    def lookup(bkt):
        level = [rb_ref[b, h] * LOG2E for b in range(REL_BUCKETS)]
        for bit in range(REL_BUCKETS.bit_length() - 1):
            odd = (bkt & (1 << bit)) != 0
            level = [jnp.where(odd, level[2 * j + 1], level[2 * j]) for j in range(len(level) // 2)]
        return level[0]

    def fill(r0, rows, constant):
        if constant:
            slab = lookup(bkt_ref[r0:r0 + 8, :])
            o_ref[r0:r0 + rows, :] = jnp.broadcast_to(slab[None], (rows // 8, 8, TQ)).reshape(rows, TQ)
        else:
            o_ref[r0:r0 + rows, :] = lookup(bkt_ref[r0:r0 + rows, :])

    for t in range(N_BIAS_TILES):
        fill(t * TK, TK, constant=t in (0, N_BIAS_TILES - 1))
    for m in range(2):
        fill(N_BIAS_TILES * TK + m * N_META, N_META, constant=m == 1)


def _bias_tiles(rel_bias):
    return pl.pallas_call(
        _bias_kernel,
        grid=(N_HEADS,),
        in_specs=[
            pl.BlockSpec(memory_space=pltpu.SMEM),
            pl.BlockSpec((BIAS_ROWS, TQ), lambda h: (0, 0)),
        ],
        out_specs=pl.BlockSpec((None, BIAS_ROWS, TQ), lambda h: (h, 0, 0)),
        out_shape=jax.ShapeDtypeStruct((N_HEADS, BIAS_ROWS, TQ), F32),
        compiler_params=pltpu.CompilerParams(dimension_semantics=("arbitrary",)),
        name="bias_tiles",
    )(rel_bias, _bias_buckets())


_NT = (((1,), (1,)), ((), ()))


def _attn_kernel(lam_ref, zero_ref, qt_ref, k_ref, vt_ref, km_ref, vmt_ref, bias_ref, g_ref, o_ref,
                 s0_scr, s1_scr, s2_scr, vt_scr, o0_scr, vmt_scr, *, seq):
    n_kc = seq // TK
    n_units = 2 * (seq // TQM)
    halves = TQM // TQ
    s_scr = (s0_scr, s1_scr, s2_scr)

    vt_scr[0:V_DIM, :] = vt_ref[...]
    vt_scr[V_DIM:, :] = jnp.ones((ONES_ROWS, seq), BF16)
    vmt_scr[0:V_DIM, :] = vmt_ref[:, 0:N_META]
    vmt_scr[V_DIM:, :] = jnp.ones((ONES_ROWS, N_META), BF16)
    lam = lam_ref[0, 0]
    z = zero_ref[0, 0]
    zero_half = jnp.zeros((HEAD_DIM, TQM), BF16)

    def masked_qt(u):
        qi, c = divmod(u, 2)
        qt = qt_ref[:, qi * TQM:(qi + 1) * TQM]
        if c == 0:
            return jnp.concatenate([qt[0:HEAD_DIM], zero_half], axis=0)
        return jnp.concatenate([zero_half, qt[HEAD_DIM:]], axis=0)

    def score_chunk(u, kc, qt, m_acc):
        qi = u // 2
        if kc < n_kc:
            rows = pl.ds(pl.multiple_of(z + kc * TK, TK), TK)
            s = jnp.dot(k_ref[kc * TK:(kc + 1) * TK, :], qt, preferred_element_type=F32)
        else:
            rows = pl.ds(pl.multiple_of(z + seq, N_META), N_META)
            s = jnp.dot(km_ref[0:N_META, :], qt, preferred_element_type=F32)
        out = []
        for j in range(halves):
            qh = qi * halves + j
            cols = slice(j * TQ, (j + 1) * TQ)
            if kc < n_kc:
                b0 = (min(max(kc - qh, -2), 2) + 2) * TK
                sc = s[:, cols] + bias_ref[b0:b0 + TK, :]
            else:
                b0 = N_BIAS_TILES * TK + min(qh, 1) * N_META
                sc = s[:, cols] + bias_ref[b0:b0 + N_META, :]
            s_scr[u % len(s_scr)][rows, cols] = sc
            cm = jnp.max(sc.reshape(sc.shape[0] // 8, 8, TQ), axis=0)
            out.append(cm if m_acc is None else jnp.maximum(m_acc[j], cm))
        return out

    def value_chunk(u, kc, m, oe):
        if kc < n_kc:
            rows = pl.ds(pl.multiple_of(z + kc * TK, TK), TK)
            lhs = vt_scr[:, kc * TK:(kc + 1) * TK]
        else:
            rows = pl.ds(pl.multiple_of(z + seq, N_META), N_META)
            lhs = vmt_scr[...]
        p = jnp.exp2(s_scr[u % len(s_scr)][rows, :] - m).astype(BF16)
        d = jnp.dot(lhs, p, preferred_element_type=F32)
        return d if oe is None else oe + d

    def finalize(qi, oe0, oe1):
        o = (oe0[0:V_DIM] * (1.0 / oe0[V_DIM:V_DIM + 1])
             - oe1[0:V_DIM] * (lam / oe1[V_DIM:V_DIM + 1]))
        ms = jnp.mean(o * o, axis=0, keepdims=True)
        y = ((o * lax.rsqrt(ms + SUBLN_EPS)) * g_ref[...]) * (1.0 - LAMBDA_INIT)
        o_ref[qi * TQM:(qi + 1) * TQM, :] = y.T.astype(BF16)

    def col_max(m_acc):
        return jnp.concatenate([jnp.max(a, axis=0, keepdims=True) for a in m_acc], axis=1)

    n_chunks = n_kc + 1
    total = n_units * n_chunks
    lag = n_chunks + CHUNK_SKEW
    score_order = [n_kc] + list(range(n_kc))
    meta_pos = -lag % n_chunks
    value_order = list(range(n_kc))
    value_order.insert(meta_pos, n_kc)
    col_max_of = {}
    qt = m_acc = oe = None
    for g in range(total + lag):
        if g < total:
            u, i = divmod(g, n_chunks)
            if i == 0:
                qt, m_acc = masked_qt(u), None
            m_acc = score_chunk(u, score_order[i], qt, m_acc)
            if i == n_chunks - 1:
                col_max_of[u] = col_max(m_acc)
        if g >= lag:
            u, i = divmod(g - lag, n_chunks)
            oe = value_chunk(u, value_order[i], col_max_of[u], None if i == 0 else oe)
            if i == n_chunks - 1:
                if u % 2 == 0:
                    o0_scr[...] = oe
                else:
                    finalize(u // 2, o0_scr[...], oe)


def _attention(lam, qt, k, vt, k_meta, vt_meta, bias, g2d, batch, seq):
    head_rows = lambda: pl.BlockSpec((None, seq, V_DIM), lambda b, h: (b, 0, h))
    return pl.pallas_call(
        functools.partial(_attn_kernel, seq=seq),
        grid=(batch, N_HEADS),
        in_specs=[
            pl.BlockSpec(memory_space=pltpu.SMEM),
            pl.BlockSpec(memory_space=pltpu.SMEM),
            pl.BlockSpec((V_DIM, seq), lambda b, h: (h, b)),
            head_rows(),
            pl.BlockSpec((V_DIM, seq), lambda b, h: (h, b)),
            pl.BlockSpec((META_PAD, V_DIM), lambda b, h: (0, h)),
            pl.BlockSpec((V_DIM, META_PAD), lambda b, h: (h, 0)),
            pl.BlockSpec((None, BIAS_ROWS, TQ), lambda b, h: (h, 0, 0)),
            pl.BlockSpec((V_DIM, TQM), lambda b, h: (0, 0)),
        ],
        out_specs=head_rows(),
        out_shape=jax.ShapeDtypeStruct((batch, seq, N_HEADS * V_DIM), BF16),
        scratch_shapes=[
            pltpu.VMEM((seq + N_META, TQM), F32),
            pltpu.VMEM((seq + N_META, TQM), F32),
            pltpu.VMEM((seq + N_META, TQM), F32),
            pltpu.VMEM((V_DIM + ONES_ROWS, seq), BF16),
            pltpu.VMEM((V_DIM + ONES_ROWS, TQM), F32),
            pltpu.VMEM((V_DIM + ONES_ROWS, N_META), BF16),
        ],
        compiler_params=pltpu.CompilerParams(
            dimension_semantics=("arbitrary", "arbitrary"), vmem_limit_bytes=VMEM_LIMIT_BYTES),
        name="diff_attn",
    )(lam, jnp.zeros((1, 1), jnp.int32), qt, k, vt, k_meta, vt_meta, bias, g2d)


def _rms(x, g):
    ms = jnp.mean(x * x, axis=-1, keepdims=True)
    return (x * lax.rsqrt(ms + NORM_EPS)) * g


def _tail_kernel(x_ref, zp_ref, zprev_ref, znext_ref, zmeta_ref, ya_ref, pw_ref, ps_ref, wo_ref,
                 g2_ref, wg_ref, wu_ref, wd_ref, gf_ref, o_ref, e_scr, *, tm, seq):
    tiles_per_seq = seq // tm
    t = lax.rem(pl.program_id(0), tiles_per_seq)
    p0 = t * tm
    prev = jnp.where(t == 0, zmeta_ref[...], zprev_ref[...])
    nxt = jnp.where(t == tiles_per_seq - 1, 0.0, znext_ref[...])
    zp = zp_ref[...]
    for g in range(len(POOL_WINDOWS)):
        lanes = slice(g * POOL_GROUP, (g + 1) * POOL_GROUP)
        e_scr[g, 0:HALO, :] = prev[:, lanes]
        e_scr[g, HALO:HALO + tm, :] = zp[:, lanes]
        e_scr[g, HALO + tm:2 * HALO + tm, :] = nxt[:, lanes]

    rp = tm // TAIL_PARTS
    row = lax.broadcasted_iota(jnp.int32, (rp, POOL_GROUP), 0)

    def pooled(r0):
        y_groups = []
        for g, w in enumerate(POOL_WINDOWS):
            left = w // 2
            right = w - 1 - left
            base = HALO + r0
            win = e_scr[g, base - left:base - left + rp, :]
            for k in range(-left + 1, right + 1):
                win = win + e_scr[g, base + k:base + k + rp, :]
            over = jnp.maximum(p0 + r0 + row + (right - (seq - 1)), 0)
            cnt = (w - over).astype(F32)
            d = win / cnt - zp_ref[r0:r0 + rp, g * POOL_GROUP:(g + 1) * POOL_GROUP]
            yg = jnp.dot(d.astype(BF16), pw_ref[g], preferred_element_type=F32)
            y_groups.append((yg * ps_ref[:, g * POOL_GROUP:(g + 1) * POOL_GROUP]).astype(BF16))
        return jnp.concatenate(y_groups + [ya_ref[r0:r0 + rp, :]], axis=-1)

    def out_proj(r0, ycat):
        return x_ref[r0:r0 + rp, :] + jnp.dot(ycat, wo_ref[...], preferred_element_type=F32)

    def ffn(h1, f):
        acc = h1
        for c0 in range(0, D_FF, FF_CHUNK):
            cols = slice(c0, min(c0 + FF_CHUNK, D_FF))
            gate = jnp.dot(f, wg_ref[:, cols], preferred_element_type=F32)
            up = jnp.dot(f, wu_ref[:, cols], preferred_element_type=F32)
            act = (gate * jax.nn.sigmoid(gate) * up).astype(BF16)
            acc = acc + jnp.dot(act, wd_ref[cols, :], preferred_element_type=F32)
        return acc

    parts = [i * rp for i in range(TAIL_PARTS)]
    ycat = {parts[0]: pooled(parts[0])}
    h1, f, h2 = {}, {}, {}
    for i, r0 in enumerate(parts):
        h1[r0] = out_proj(r0, ycat[r0])
        if i + 1 < len(parts):
            ycat[parts[i + 1]] = pooled(parts[i + 1])
        f[r0] = _rms(h1[r0], g2_ref[...]).astype(BF16)
    for i, r0 in enumerate(parts):
        h2[r0] = ffn(h1[r0], f[r0])
        if i > 0:
            o_ref[parts[i - 1]:parts[i - 1] + rp, :] = _rms(h2[parts[i - 1]], gf_ref[...])
    o_ref[parts[-1]:parts[-1] + rp, :] = _rms(h2[parts[-1]], gf_ref[...])


def _tail(x2d, zp, zp_meta, y_attn, pool_w, pool_scale, w_o, g2, w_gate, w_up, w_down, g_final,
          tm, seq):
    rows = x2d.shape[0]
    halo_blocks = tm // HALO
    last_halo_block = rows // HALO - 1
    const = lambda shape: pl.BlockSpec(shape, lambda i: (0,) * len(shape),
                                       pipeline_mode=pl.Buffered(1))
    return pl.pallas_call(
        functools.partial(_tail_kernel, tm=tm, seq=seq),
        grid=(rows // tm,),
        in_specs=[
            pl.BlockSpec((tm, D_MODEL), lambda i: (i, 0)),
            pl.BlockSpec((tm, POOL_WIDTH), lambda i: (i, 0)),
            pl.BlockSpec((HALO, POOL_WIDTH), lambda i: (jnp.maximum(i * halo_blocks - 1, 0), 0)),
            pl.BlockSpec((HALO, POOL_WIDTH),
                         lambda i: (jnp.minimum((i + 1) * halo_blocks, last_halo_block), 0)),
            pl.BlockSpec((HALO, POOL_WIDTH), lambda i: (N_META // HALO - 1, 0)),
            pl.BlockSpec((tm, N_HEADS * V_DIM), lambda i: (i, 0)),
            const((len(POOL_WINDOWS), POOL_GROUP, POOL_GROUP)),
            const((1, POOL_WIDTH)),
            const((D_MODEL, D_MODEL)),
            const((1, D_MODEL)),
            const((D_MODEL, D_FF)),
            const((D_MODEL, D_FF)),
            const((D_FF, D_MODEL)),
            const((1, D_MODEL)),
        ],
        out_specs=pl.BlockSpec((tm, D_MODEL), lambda i: (i, 0)),
        out_shape=jax.ShapeDtypeStruct((rows, D_MODEL), F32),
        scratch_shapes=[pltpu.VMEM((len(POOL_WINDOWS), tm + 2 * HALO, POOL_GROUP), F32)],
        compiler_params=pltpu.CompilerParams(
            dimension_semantics=("arbitrary",), vmem_limit_bytes=VMEM_LIMIT_BYTES),
        name="pool_oproj_ffn",
    )(x2d, zp, zp, zp, zp_meta, y_attn, pool_w, pool_scale, w_o, g2, w_gate, w_up, w_down, g_final)


def kernel(x, meta_tokens, rel_bias, norm1_g, w_in, pool_w, pool_scale, lambda_q1, lambda_k1,
           lambda_q2, lambda_k2, subln_g, w_o, norm2_g, w_gate, w_up, w_down, final_g):
    batch, seq, _ = x.shape
    layer = 0
    lam = (jnp.exp(jnp.sum(lambda_q1[layer].astype(F32) * lambda_k1[layer].astype(F32)))
           - jnp.exp(jnp.sum(lambda_q2[layer].astype(F32) * lambda_k2[layer].astype(F32)))
           + LAMBDA_INIT).reshape(1, 1)

    col_scale = jnp.concatenate([
        jnp.ones((POOL_WIDTH,), F32), jnp.full((ATTN_WIDTH,), HEAD_DIM ** -0.5 * LOG2E, F32),
        jnp.ones((2 * ATTN_WIDTH,), F32)])
    w_in_b = (w_in[layer] * col_scale).astype(BF16)
    g1 = norm1_g[layer].reshape(1, D_MODEL)

    x2d = x.reshape(batch * seq, D_MODEL)
    zp, qt, k, vt = _inproj(x2d, g1, w_in_b, tm=1024)
    meta_pad = jnp.pad(meta_tokens, ((0, META_PAD - N_META), (0, 0)))
    zp_meta, _, k_meta, vt_meta = _inproj(meta_pad, g1, w_in_b, tm=META_PAD)

    bias = _bias_tiles(rel_bias)
    g2d = jnp.broadcast_to(subln_g[layer].astype(F32)[:, None], (V_DIM, TQM))
    y_attn = _attention(
        lam, qt, k.reshape(batch, seq, ATTN_WIDTH),
        vt, k_meta, vt_meta, bias, g2d, batch, seq)

    out = _tail(
        x2d, zp, zp_meta, y_attn.reshape(batch * seq, N_HEADS * V_DIM),
        pool_w[layer].astype(BF16), pool_scale[layer].reshape(1, POOL_WIDTH),
        w_o[layer].astype(BF16), norm2_g[layer].reshape(1, D_MODEL),
        w_gate[layer].astype(BF16), w_up[layer].astype(BF16), w_down[layer].astype(BF16),
        final_g.reshape(1, D_MODEL), tm=1024, seq=seq)
    return out.reshape(batch, seq, D_MODEL)
```

```python
import functools
import math

import jax
import jax.numpy as jnp
from jax import lax
from jax.experimental import pallas as pl
from jax.experimental.pallas import tpu as pltpu

F32 = jnp.float32
BF16 = jnp.bfloat16

D_MODEL = 1024
N_META = 16
META_PAD = 128
POOL_WIDTH = 512
POOL_WINDOWS = (2, 4, 8, 16)
POOL_GROUP = 128
ATTN_WIDTH = 512
HEAD_DIM = 64
N_HEADS = 4
V_DIM = 128
IN_WIDTH = POOL_WIDTH + 3 * ATTN_WIDTH
REL_BUCKETS = 32
REL_MAX_DIST = 128
D_FF = 2816
NORM_EPS = 1e-6
SUBLN_EPS = 1e-5
LAMBDA_INIT = 0.8 - 0.6 * math.exp(-0.3 * 0)
HALO = 8

VMEM_LIMIT_BYTES = 56 * 1024 * 1024

TQ = 256
TQM = 512
TK = 256
N_BIAS_TILES = 5
BIAS_ROWS = N_BIAS_TILES * TK + 2 * N_META
FF_CHUNK = 256
ONES_ROWS = 16
LOG2E = math.log2(math.e)
CHUNK_SKEW = 2
TAIL_PARTS = 1


def _inproj_kernel(x_ref, g_ref, w_ref, zp_ref, qt_ref, k_ref, vt_ref):
    x = x_ref[...]
    ms = jnp.mean(x * x, axis=-1, keepdims=True)
    u = ((x * lax.rsqrt(ms + NORM_EPS)) * g_ref[...]).astype(BF16)

    def proj(lo, hi):
        return jnp.dot(u, w_ref[:, lo:hi], preferred_element_type=F32)

    o_q = POOL_WIDTH
    o_k = o_q + ATTN_WIDTH
    o_v = o_k + ATTN_WIDTH
    zp_ref[...] = proj(0, o_q)
    qt_ref[...] = proj(o_q, o_k).T.astype(BF16)
    k_ref[...] = proj(o_k, o_v).astype(BF16)
    vt_ref[...] = proj(o_v, IN_WIDTH).T.astype(BF16)


def _inproj(x2d, g, w_bf16, tm):
    rows = x2d.shape[0]
    row_spec = lambda width: pl.BlockSpec((tm, width), lambda i: (i, 0))
    col_spec = lambda width: pl.BlockSpec((width, tm), lambda i: (0, i))
    return pl.pallas_call(
        _inproj_kernel,
        grid=(rows // tm,),
        in_specs=[
            row_spec(D_MODEL),
            pl.BlockSpec((1, D_MODEL), lambda i: (0, 0)),
            pl.BlockSpec((D_MODEL, IN_WIDTH), lambda i: (0, 0)),
        ],
        out_specs=[row_spec(POOL_WIDTH), col_spec(ATTN_WIDTH), row_spec(ATTN_WIDTH),
                   col_spec(ATTN_WIDTH)],
        out_shape=[
            jax.ShapeDtypeStruct((rows, POOL_WIDTH), F32),
            jax.ShapeDtypeStruct((ATTN_WIDTH, rows), BF16),
            jax.ShapeDtypeStruct((rows, ATTN_WIDTH), BF16),
            jax.ShapeDtypeStruct((ATTN_WIDTH, rows), BF16),
        ],
        compiler_params=pltpu.CompilerParams(
            dimension_semantics=("arbitrary",), vmem_limit_bytes=VMEM_LIMIT_BYTES),
        name="inproj",
    )(x2d, g, w_bf16)


def _t5_bucket(rel):
    nb = REL_BUCKETS // 2
    ret = jnp.where(rel > 0, nb, 0)
    n = jnp.abs(rel)
    max_exact = nb // 2
    nf = jnp.maximum(n, 1).astype(F32)
    large = max_exact + (jnp.log(nf / max_exact) / math.log(REL_MAX_DIST / max_exact)
                         * (nb - max_exact)).astype(jnp.int32)
    large = jnp.minimum(large, nb - 1)
    return ret + jnp.where(n < max_exact, n, large)


def _bias_buckets():
    r = lax.broadcasted_iota(jnp.int32, (TK, TQ), 0)
    c = lax.broadcasted_iota(jnp.int32, (TK, TQ), 1)
    tiles = [(t - 2) * TK + r - c for t in range(N_BIAS_TILES)]
    rm = lax.broadcasted_iota(jnp.int32, (N_META, TQ), 0)
    cm = lax.broadcasted_iota(jnp.int32, (N_META, TQ), 1)
    metas = [rm - N_META - m * TQ - cm for m in range(2)]
    return _t5_bucket(jnp.concatenate(tiles + metas, axis=0))


def _bias_kernel(rb_ref, bkt_ref, o_ref):
    h = pl.program_id(0)

    def lookup(bkt):
        level = [rb_ref[b, h] * LOG2E for b in range(REL_BUCKETS)]
        for bit in range(REL_BUCKETS.bit_length() - 1):
            odd = (bkt & (1 << bit)) != 0
            level = [jnp.where(odd, level[2 * j + 1], level[2 * j]) for j in range(len(level) // 2)]
        return level[0]

    def fill(r0, rows, constant):
        if constant:
            slab = lookup(bkt_ref[r0:r0 + 8, :])
            o_ref[r0:r0 + rows, :] = jnp.broadcast_to(slab[None], (rows // 8, 8, TQ)).reshape(rows, TQ)
        else:
            o_ref[r0:r0 + rows, :] = lookup(bkt_ref[r0:r0 + rows, :])

    for t in range(N_BIAS_TILES):
        fill(t * TK, TK, constant=t in (0, N_BIAS_TILES - 1))
    for m in range(2):
        fill(N_BIAS_TILES * TK + m * N_META, N_META, constant=m == 1)


def _bias_tiles(rel_bias):
    return pl.pallas_call(
        _bias_kernel,
        grid=(N_HEADS,),
        in_specs=[
            pl.BlockSpec(memory_space=pltpu.SMEM),
            pl.BlockSpec((BIAS_ROWS, TQ), lambda h: (0, 0)),
        ],
        out_specs=pl.BlockSpec((None, BIAS_ROWS, TQ), lambda h: (h, 0, 0)),
        out_shape=jax.ShapeDtypeStruct((N_HEADS, BIAS_ROWS, TQ), F32),
        compiler_params=pltpu.CompilerParams(dimension_semantics=("arbitrary",)),
        name="bias_tiles",
    )(rel_bias, _bias_buckets())


_NT = (((1,), (1,)), ((), ()))


def _attn_kernel(lam_ref, zero_ref, qt_ref, k_ref, vt_ref, km_ref, vmt_ref, bias_ref, g_ref, o_ref,
                 s0_scr, s1_scr, s2_scr, vt_scr, o0_scr, vmt_scr, *, seq):
    n_kc = seq // TK
    n_units = 2 * (seq // TQM)
    halves = TQM // TQ
    s_scr = (s0_scr, s1_scr, s2_scr)

    vt_scr[0:V_DIM, :] = vt_ref[...]
    vt_scr[V_DIM:, :] = jnp.ones((ONES_ROWS, seq), BF16)
    vmt_scr[0:V_DIM, :] = vmt_ref[:, 0:N_META]
    vmt_scr[V_DIM:, :] = jnp.ones((ONES_ROWS, N_META), BF16)
    lam = lam_ref[0, 0]
    z = zero_ref[0, 0]
    zero_half = jnp.zeros((HEAD_DIM, TQM), BF16)

    def masked_qt(u):
        qi, c = divmod(u, 2)
        qt = qt_ref[:, qi * TQM:(qi + 1) * TQM]
        if c == 0:
            return jnp.concatenate([qt[0:HEAD_DIM], zero_half], axis=0)
        return jnp.concatenate([zero_half, qt[HEAD_DIM:]], axis=0)

    def score_chunk(u, kc, qt, m_acc):
        qi = u // 2
        if kc < n_kc:
            rows = pl.ds(pl.multiple_of(z + kc * TK, TK), TK)
            s = jnp.dot(k_ref[kc * TK:(kc + 1) * TK, :], qt, preferred_element_type=F32)
        else:
            rows = pl.ds(pl.multiple_of(z + seq, N_META), N_META)
            s = jnp.dot(km_ref[0:N_META, :], qt, preferred_element_type=F32)
        out = []
        for j in range(halves):
            qh = qi * halves + j
            cols = slice(j * TQ, (j + 1) * TQ)
            if kc < n_kc:
                b0 = (min(max(kc - qh, -2), 2) + 2) * TK
                sc = s[:, cols] + bias_ref[b0:b0 + TK, :]
            else:
                b0 = N_BIAS_TILES * TK + min(qh, 1) * N_META
                sc = s[:, cols] + bias_ref[b0:b0 + N_META, :]
            s_scr[u % len(s_scr)][rows, cols] = sc
            cm = jnp.max(sc.reshape(sc.shape[0] // 8, 8, TQ), axis=0)
            out.append(cm if m_acc is None else jnp.maximum(m_acc[j], cm))
        return out

    def value_chunk(u, kc, m, oe):
        if kc < n_kc:
            rows = pl.ds(pl.multiple_of(z + kc * TK, TK), TK)
            lhs = vt_scr[:, kc * TK:(kc + 1) * TK]
        else:
            rows = pl.ds(pl.multiple_of(z + seq, N_META), N_META)
            lhs = vmt_scr[...]
        p = jnp.exp2(s_scr[u % len(s_scr)][rows, :] - m).astype(BF16)
        d = jnp.dot(lhs, p, preferred_element_type=F32)
        return d if oe is None else oe + d

    def finalize(qi, oe0, oe1):
        o = (oe0[0:V_DIM] * (1.0 / oe0[V_DIM:V_DIM + 1])
             - oe1[0:V_DIM] * (lam / oe1[V_DIM:V_DIM + 1]))
        ms = jnp.mean(o * o, axis=0, keepdims=True)
        y = ((o * lax.rsqrt(ms + SUBLN_EPS)) * g_ref[...]) * (1.0 - LAMBDA_INIT)
        o_ref[qi * TQM:(qi + 1) * TQM, :] = y.T.astype(BF16)

    def col_max(m_acc):
        return jnp.concatenate([jnp.max(a, axis=0, keepdims=True) for a in m_acc], axis=1)

    n_chunks = n_kc + 1
    total = n_units * n_chunks
    lag = n_chunks + CHUNK_SKEW
    score_order = [n_kc] + list(range(n_kc))
    meta_pos = -lag % n_chunks
    value_order = list(range(n_kc))
    value_order.insert(meta_pos, n_kc)
    col_max_of = {}
    qt = m_acc = oe = None
    for g in range(total + lag):
        if g < total:
            u, i = divmod(g, n_chunks)
            if i == 0:
                qt, m_acc = masked_qt(u), None
            m_acc = score_chunk(u, score_order[i], qt, m_acc)
            if i == n_chunks - 1:
                col_max_of[u] = col_max(m_acc)
        if g >= lag:
            u, i = divmod(g - lag, n_chunks)
            oe = value_chunk(u, value_order[i], col_max_of[u], None if i == 0 else oe)
            if i == n_chunks - 1:
                if u % 2 == 0:
                    o0_scr[...] = oe
                else:
                    finalize(u // 2, o0_scr[...], oe)


def _attention(lam, qt, k, vt, k_meta, vt_meta, bias, g2d, batch, seq):
    head_rows = lambda: pl.BlockSpec((None, seq, V_DIM), lambda b, h: (b, 0, h))
    return pl.pallas_call(
        functools.partial(_attn_kernel, seq=seq),
        grid=(batch, N_HEADS),
        in_specs=[
            pl.BlockSpec(memory_space=pltpu.SMEM),
            pl.BlockSpec(memory_space=pltpu.SMEM),
            pl.BlockSpec((V_DIM, seq), lambda b, h: (h, b)),
            head_rows(),
            pl.BlockSpec((V_DIM, seq), lambda b, h: (h, b)),
            pl.BlockSpec((META_PAD, V_DIM), lambda b, h: (0, h)),
            pl.BlockSpec((V_DIM, META_PAD), lambda b, h: (h, 0)),
            pl.BlockSpec((None, BIAS_ROWS, TQ), lambda b, h: (h, 0, 0)),
            pl.BlockSpec((V_DIM, TQM), lambda b, h: (0, 0)),
        ],
        out_specs=head_rows(),
        out_shape=jax.ShapeDtypeStruct((batch, seq, N_HEADS * V_DIM), BF16),
        scratch_shapes=[
            pltpu.VMEM((seq + N_META, TQM), F32),
            pltpu.VMEM((seq + N_META, TQM), F32),
            pltpu.VMEM((seq + N_META, TQM), F32),
            pltpu.VMEM((V_DIM + ONES_ROWS, seq), BF16),
            pltpu.VMEM((V_DIM + ONES_ROWS, TQM), F32),
            pltpu.VMEM((V_DIM + ONES_ROWS, N_META), BF16),
        ],
        compiler_params=pltpu.CompilerParams(
            dimension_semantics=("arbitrary", "arbitrary"), vmem_limit_bytes=VMEM_LIMIT_BYTES),
        name="diff_attn",
    )(lam, jnp.zeros((1, 1), jnp.int32), qt, k, vt, k_meta, vt_meta, bias, g2d)


def _rms(x, g):
    ms = jnp.mean(x * x, axis=-1, keepdims=True)
    return (x * lax.rsqrt(ms + NORM_EPS)) * g


def _tail_kernel(x_ref, zp_ref, zprev_ref, znext_ref, zmeta_ref, ya_ref, pw_ref, ps_ref, wo_ref,
                 g2_ref, wg_ref, wu_ref, wd_ref, gf_ref, o_ref, e_scr, c_scr, *, tm, seq):
    tiles_per_seq = seq // tm
    t = lax.rem(pl.program_id(0), tiles_per_seq)
    p0 = t * tm
    prev = jnp.where(t == 0, zmeta_ref[...], zprev_ref[...])
    nxt = jnp.where(t == tiles_per_seq - 1, 0.0, znext_ref[...])
    zp = zp_ref[...]
    for g in range(len(POOL_WINDOWS)):
        lanes = slice(g * POOL_GROUP, (g + 1) * POOL_GROUP)
        e_scr[g, 0:HALO, :] = prev[:, lanes]
        e_scr[g, HALO:HALO + tm, :] = zp[:, lanes]
        e_scr[g, HALO + tm:2 * HALO + tm, :] = nxt[:, lanes]
        e_scr[g, 2 * HALO + tm:, :] = jnp.zeros((2 * HALO, POOL_GROUP), F32)

    rp = tm // TAIL_PARTS
    tail_row = lax.broadcasted_iota(jnp.int32, (HALO, POOL_GROUP), 0)

    def window_sum(g, w, r0):
        left = w // 2
        base = HALO + r0
        if w <= 4:
            win = e_scr[g, base - left:base - left + rp, :]
            for k in range(-left + 1, w - left):
                win = win + e_scr[g, base + k:base + k + rp, :]
            return win
        j0 = base - HALO
        src, width, n = e_scr.at[g], 1, rp + 3 * HALO
        level = 0
        while 2 * width < w:
            c_scr[level, j0:j0 + n, :] = src[j0:j0 + n, :] + src[j0 + width:j0 + width + n, :]
            src, width, n, level = c_scr.at[level], 2 * width, n - HALO, level + 1
        lo = base - left
        return src[lo:lo + rp, :] + src[lo + width:lo + width + rp, :]

    def pooled(r0):
        y_groups = []
        for g, w in enumerate(POOL_WINDOWS):
            right = w - 1 - w // 2
            win = window_sum(g, w, r0)
            center = zp_ref[r0:r0 + rp, g * POOL_GROUP:(g + 1) * POOL_GROUP]
            if r0 + rp < tm:
                d = win * (1.0 / w) - center
            else:
                body = win[0:rp - HALO] * (1.0 / w) - center[0:rp - HALO]
                pos = p0 + (tm - HALO) + tail_row
                over = jnp.maximum(pos + (right - (seq - 1)), 0)
                cnt = (w - over).astype(F32)
                d = jnp.concatenate([body, win[rp - HALO:] / cnt - center[rp - HALO:]], axis=0)
            yg = jnp.dot(d.astype(BF16), pw_ref[g], preferred_element_type=F32)
            y_groups.append((yg * ps_ref[:, g * POOL_GROUP:(g + 1) * POOL_GROUP]).astype(BF16))
        return jnp.concatenate(y_groups + [ya_ref[r0:r0 + rp, :]], axis=-1)

    def out_proj(r0, ycat):
        return x_ref[r0:r0 + rp, :] + jnp.dot(ycat, wo_ref[...], preferred_element_type=F32)

    def ffn(h1, f):
        acc = h1
        for c0 in range(0, D_FF, FF_CHUNK):
            cols = slice(c0, min(c0 + FF_CHUNK, D_FF))
            gate = jnp.dot(f, wg_ref[:, cols], preferred_element_type=F32)
            up = jnp.dot(f, wu_ref[:, cols], preferred_element_type=F32)
            act = (gate * jax.nn.sigmoid(gate) * up).astype(BF16)
            acc = acc + jnp.dot(act, wd_ref[cols, :], preferred_element_type=F32)
        return acc

    parts = [i * rp for i in range(TAIL_PARTS)]
    ycat = {parts[0]: pooled(parts[0])}
    h1, f, h2 = {}, {}, {}
    for i, r0 in enumerate(parts):
        h1[r0] = out_proj(r0, ycat[r0])
        if i + 1 < len(parts):
            ycat[parts[i + 1]] = pooled(parts[i + 1])
        f[r0] = _rms(h1[r0], g2_ref[...]).astype(BF16)
    for i, r0 in enumerate(parts):
        h2[r0] = ffn(h1[r0], f[r0])
        if i > 0:
            o_ref[parts[i - 1]:parts[i - 1] + rp, :] = _rms(h2[parts[i - 1]], gf_ref[...])
    o_ref[parts[-1]:parts[-1] + rp, :] = _rms(h2[parts[-1]], gf_ref[...])


def _tail(x2d, zp, zp_meta, y_attn, pool_w, pool_scale, w_o, g2, w_gate, w_up, w_down, g_final,
          tm, seq):
    rows = x2d.shape[0]
    halo_blocks = tm // HALO
    last_halo_block = rows // HALO - 1
    const = lambda shape: pl.BlockSpec(shape, lambda i: (0,) * len(shape),
                                       pipeline_mode=pl.Buffered(1))
    return pl.pallas_call(
        functools.partial(_tail_kernel, tm=tm, seq=seq),
        grid=(rows // tm,),
        in_specs=[
            pl.BlockSpec((tm, D_MODEL), lambda i: (i, 0)),
            pl.BlockSpec((tm, POOL_WIDTH), lambda i: (i, 0)),
            pl.BlockSpec((HALO, POOL_WIDTH), lambda i: (jnp.maximum(i * halo_blocks - 1, 0), 0)),
            pl.BlockSpec((HALO, POOL_WIDTH),
                         lambda i: (jnp.minimum((i + 1) * halo_blocks, last_halo_block), 0)),
            pl.BlockSpec((HALO, POOL_WIDTH), lambda i: (N_META // HALO - 1, 0)),
            pl.BlockSpec((tm, N_HEADS * V_DIM), lambda i: (i, 0)),
            const((len(POOL_WINDOWS), POOL_GROUP, POOL_GROUP)),
            const((1, POOL_WIDTH)),
            const((D_MODEL, D_MODEL)),
            const((1, D_MODEL)),
            const((D_MODEL, D_FF)),
            const((D_MODEL, D_FF)),
            const((D_FF, D_MODEL)),
            const((1, D_MODEL)),
        ],
        out_specs=pl.BlockSpec((tm, D_MODEL), lambda i: (i, 0)),
        out_shape=jax.ShapeDtypeStruct((rows, D_MODEL), F32),
        scratch_shapes=[pltpu.VMEM((len(POOL_WINDOWS), tm + 4 * HALO, POOL_GROUP), F32),
                        pltpu.VMEM((3, tm + 4 * HALO, POOL_GROUP), F32)],
        compiler_params=pltpu.CompilerParams(
            dimension_semantics=("arbitrary",), vmem_limit_bytes=VMEM_LIMIT_BYTES),
        name="pool_oproj_ffn",
    )(x2d, zp, zp, zp, zp_meta, y_attn, pool_w, pool_scale, w_o, g2, w_gate, w_up, w_down, g_final)


def kernel(x, meta_tokens, rel_bias, norm1_g, w_in, pool_w, pool_scale, lambda_q1, lambda_k1,
           lambda_q2, lambda_k2, subln_g, w_o, norm2_g, w_gate, w_up, w_down, final_g):
    batch, seq, _ = x.shape
    layer = 0
    lam = (jnp.exp(jnp.sum(lambda_q1[layer].astype(F32) * lambda_k1[layer].astype(F32)))
           - jnp.exp(jnp.sum(lambda_q2[layer].astype(F32) * lambda_k2[layer].astype(F32)))
           + LAMBDA_INIT).reshape(1, 1)

    col_scale = jnp.concatenate([
        jnp.ones((POOL_WIDTH,), F32), jnp.full((ATTN_WIDTH,), HEAD_DIM ** -0.5 * LOG2E, F32),
        jnp.ones((2 * ATTN_WIDTH,), F32)])
    w_in_b = (w_in[layer] * col_scale).astype(BF16)
    g1 = norm1_g[layer].reshape(1, D_MODEL)

    x2d = x.reshape(batch * seq, D_MODEL)
    zp, qt, k, vt = _inproj(x2d, g1, w_in_b, tm=1024)
    meta_pad = jnp.pad(meta_tokens, ((0, META_PAD - N_META), (0, 0)))
    zp_meta, _, k_meta, vt_meta = _inproj(meta_pad, g1, w_in_b, tm=META_PAD)

    bias = _bias_tiles(rel_bias)
    g2d = jnp.broadcast_to(subln_g[layer].astype(F32)[:, None], (V_DIM, TQM))
    y_attn = _attention(
        lam, qt, k.reshape(batch, seq, ATTN_WIDTH),
        vt, k_meta, vt_meta, bias, g2d, batch, seq)

    out = _tail(
        x2d, zp, zp_meta, y_attn.reshape(batch * seq, N_HEADS * V_DIM),
        pool_w[layer].astype(BF16), pool_scale[layer].reshape(1, POOL_WIDTH),
        w_o[layer].astype(BF16), norm2_g[layer].reshape(1, D_MODEL),
        w_gate[layer].astype(BF16), w_up[layer].astype(BF16), w_down[layer].astype(BF16),
        final_g.reshape(1, D_MODEL), tm=1024, seq=seq)
    return out.reshape(batch, seq, D_MODEL)
```

```python
import functools
import math

import jax
import jax.numpy as jnp
from jax import lax
from jax.experimental import pallas as pl
from jax.experimental.pallas import tpu as pltpu

F32 = jnp.float32
BF16 = jnp.bfloat16

D_MODEL = 1024
N_META = 16
META_PAD = 128
POOL_WIDTH = 512
POOL_WINDOWS = (2, 4, 8, 16)
POOL_GROUP = 128
ATTN_WIDTH = 512
HEAD_DIM = 64
N_HEADS = 4
V_DIM = 128
IN_WIDTH = POOL_WIDTH + 3 * ATTN_WIDTH
REL_BUCKETS = 32
REL_MAX_DIST = 128
D_FF = 2816
NORM_EPS = 1e-6
SUBLN_EPS = 1e-5
LAMBDA_INIT = 0.8 - 0.6 * math.exp(-0.3 * 0)
HALO = 8

VMEM_LIMIT_BYTES = 56 * 1024 * 1024

TQ = 256
TQM = 512
TK = 256
N_BIAS_TILES = 5
BIAS_ROWS = N_BIAS_TILES * TK + 2 * N_META
FF_CHUNK = 256
ONES_ROWS = 16
LOG2E = math.log2(math.e)
CHUNK_SKEW = 2
TAIL_PARTS = 1
ATTN_HEADS_PER_STEP = 2


def _inproj_kernel(x_ref, g_ref, w_ref, zp_ref, qt_ref, k_ref, vt_ref):
    x = x_ref[...]
    ms = jnp.mean(x * x, axis=-1, keepdims=True)
    u = ((x * lax.rsqrt(ms + NORM_EPS)) * g_ref[...]).astype(BF16)

    def proj(lo, hi):
        return jnp.dot(u, w_ref[:, lo:hi], preferred_element_type=F32)

    o_q = POOL_WIDTH
    o_k = o_q + ATTN_WIDTH
    o_v = o_k + ATTN_WIDTH
    zp_ref[...] = proj(0, o_q)
    qt_ref[...] = proj(o_q, o_k).T.astype(BF16)
    k_ref[...] = proj(o_k, o_v).astype(BF16)
    vt_ref[...] = proj(o_v, IN_WIDTH).T.astype(BF16)


def _inproj(x2d, g, w_bf16, tm):
    rows = x2d.shape[0]
    row_spec = lambda width: pl.BlockSpec((tm, width), lambda i: (i, 0))
    col_spec = lambda width: pl.BlockSpec((width, tm), lambda i: (0, i))
    return pl.pallas_call(
        _inproj_kernel,
        grid=(rows // tm,),
        in_specs=[
            row_spec(D_MODEL),
            pl.BlockSpec((1, D_MODEL), lambda i: (0, 0)),
            pl.BlockSpec((D_MODEL, IN_WIDTH), lambda i: (0, 0)),
        ],
        out_specs=[row_spec(POOL_WIDTH), col_spec(ATTN_WIDTH), row_spec(ATTN_WIDTH),
                   col_spec(ATTN_WIDTH)],
        out_shape=[
            jax.ShapeDtypeStruct((rows, POOL_WIDTH), F32),
            jax.ShapeDtypeStruct((ATTN_WIDTH, rows), BF16),
            jax.ShapeDtypeStruct((rows, ATTN_WIDTH), BF16),
            jax.ShapeDtypeStruct((ATTN_WIDTH, rows), BF16),
        ],
        compiler_params=pltpu.CompilerParams(
            dimension_semantics=("arbitrary",), vmem_limit_bytes=VMEM_LIMIT_BYTES),
        name="inproj",
    )(x2d, g, w_bf16)


def _t5_bucket(rel):
    nb = REL_BUCKETS // 2
    ret = jnp.where(rel > 0, nb, 0)
    n = jnp.abs(rel)
    max_exact = nb // 2
    nf = jnp.maximum(n, 1).astype(F32)
    large = max_exact + (jnp.log(nf / max_exact) / math.log(REL_MAX_DIST / max_exact)
                         * (nb - max_exact)).astype(jnp.int32)
    large = jnp.minimum(large, nb - 1)
    return ret + jnp.where(n < max_exact, n, large)


def _bias_buckets():
    r = lax.broadcasted_iota(jnp.int32, (TK, TQ), 0)
    c = lax.broadcasted_iota(jnp.int32, (TK, TQ), 1)
    tiles = [(t - 2) * TK + r - c for t in range(N_BIAS_TILES)]
    rm = lax.broadcasted_iota(jnp.int32, (N_META, TQ), 0)
    cm = lax.broadcasted_iota(jnp.int32, (N_META, TQ), 1)
    metas = [rm - N_META - m * TQ - cm for m in range(2)]
    return _t5_bucket(jnp.concatenate(tiles + metas, axis=0))


def _bias_kernel(rb_ref, bkt_ref, o_ref):
    h = pl.program_id(0)

    def lookup(bkt):
        level = [rb_ref[b, h] * LOG2E for b in range(REL_BUCKETS)]
        for bit in range(REL_BUCKETS.bit_length() - 1):
            odd = (bkt & (1 << bit)) != 0
            level = [jnp.where(odd, level[2 * j + 1], level[2 * j]) for j in range(len(level) // 2)]
        return level[0]

    def fill(r0, rows, constant):
        if constant:
            slab = lookup(bkt_ref[r0:r0 + 8, :])
            o_ref[r0:r0 + rows, :] = jnp.broadcast_to(slab[None], (rows // 8, 8, TQ)).reshape(rows, TQ)
        else:
            o_ref[r0:r0 + rows, :] = lookup(bkt_ref[r0:r0 + rows, :])

    for t in range(N_BIAS_TILES):
        fill(t * TK, TK, constant=t in (0, N_BIAS_TILES - 1))
    for m in range(2):
        fill(N_BIAS_TILES * TK + m * N_META, N_META, constant=m == 1)


def _bias_tiles(rel_bias):
    return pl.pallas_call(
        _bias_kernel,
        grid=(N_HEADS,),
        in_specs=[
            pl.BlockSpec(memory_space=pltpu.SMEM),
            pl.BlockSpec((BIAS_ROWS, TQ), lambda h: (0, 0)),
        ],
        out_specs=pl.BlockSpec((None, BIAS_ROWS, TQ), lambda h: (h, 0, 0)),
        out_shape=jax.ShapeDtypeStruct((N_HEADS, BIAS_ROWS, TQ), F32),
        compiler_params=pltpu.CompilerParams(dimension_semantics=("arbitrary",)),
        name="bias_tiles",
    )(rel_bias, _bias_buckets())


_NT = (((1,), (1,)), ((), ()))


def _attn_kernel(lam_ref, zero_ref, qt_ref, k_ref, vt_ref, km_ref, vmt_ref, bias_ref, g_ref, o_ref,
                 s0_scr, s1_scr, s2_scr, vt_scr, o0_scr, vmt_scr, *, seq, hp):
    n_kc = seq // TK
    units_per_head = 2 * (seq // TQM)
    n_units = hp * units_per_head
    halves = TQM // TQ
    s_scr = (s0_scr, s1_scr, s2_scr)

    for hd in range(hp):
        vt_scr[hd, 0:V_DIM, :] = vt_ref[hd * V_DIM:(hd + 1) * V_DIM, :]
        vt_scr[hd, V_DIM:, :] = jnp.ones((ONES_ROWS, seq), BF16)
        vmt_scr[hd, 0:V_DIM, :] = vmt_ref[hd * V_DIM:(hd + 1) * V_DIM, 0:N_META]
        vmt_scr[hd, V_DIM:, :] = jnp.ones((ONES_ROWS, N_META), BF16)
    lam = lam_ref[0, 0]
    z = zero_ref[0, 0]
    zero_half = jnp.zeros((HEAD_DIM, TQM), BF16)

    def masked_qt(u):
        hd, lu = divmod(u, units_per_head)
        qi, c = divmod(lu, 2)
        qt = qt_ref[hd * V_DIM:(hd + 1) * V_DIM, qi * TQM:(qi + 1) * TQM]
        if c == 0:
            return jnp.concatenate([qt[0:HEAD_DIM], zero_half], axis=0)
        return jnp.concatenate([zero_half, qt[HEAD_DIM:]], axis=0)

    def score_chunk(u, kc, qt, m_acc):
        hd, lu = divmod(u, units_per_head)
        qi = lu // 2
        head_lanes = slice(hd * V_DIM, (hd + 1) * V_DIM)
        if kc < n_kc:
            rows = pl.ds(pl.multiple_of(z + kc * TK, TK), TK)
            s = jnp.dot(k_ref[kc * TK:(kc + 1) * TK, head_lanes], qt,
                        preferred_element_type=F32)
        else:
            rows = pl.ds(pl.multiple_of(z + seq, N_META), N_META)
            s = jnp.dot(km_ref[0:N_META, head_lanes], qt, preferred_element_type=F32)
        out = []
        for j in range(halves):
            qh = qi * halves + j
            cols = slice(j * TQ, (j + 1) * TQ)
            if kc < n_kc:
                b0 = (min(max(kc - qh, -2), 2) + 2) * TK
                sc = s[:, cols] + bias_ref[hd, b0:b0 + TK, :]
            else:
                b0 = N_BIAS_TILES * TK + min(qh, 1) * N_META
                sc = s[:, cols] + bias_ref[hd, b0:b0 + N_META, :]
            s_scr[u % len(s_scr)][rows, cols] = sc
            cm = jnp.max(sc.reshape(sc.shape[0] // 8, 8, TQ), axis=0)
            out.append(cm if m_acc is None else jnp.maximum(m_acc[j], cm))
        return out

    def value_chunk(u, kc, m, oe):
        hd = u // units_per_head
        if kc < n_kc:
            rows = pl.ds(pl.multiple_of(z + kc * TK, TK), TK)
            lhs = vt_scr[hd, :, kc * TK:(kc + 1) * TK]
        else:
            rows = pl.ds(pl.multiple_of(z + seq, N_META), N_META)
            lhs = vmt_scr[hd]
        p = jnp.exp2(s_scr[u % len(s_scr)][rows, :] - m).astype(BF16)
        d = jnp.dot(lhs, p, preferred_element_type=F32)
        return d if oe is None else oe + d

    def finalize(u, oe0, oe1):
        hd, lu = divmod(u, units_per_head)
        qi = lu // 2
        o = (oe0[0:V_DIM] * (1.0 / oe0[V_DIM:V_DIM + 1])
             - oe1[0:V_DIM] * (lam / oe1[V_DIM:V_DIM + 1]))
        ms = jnp.mean(o * o, axis=0, keepdims=True)
        y = ((o * lax.rsqrt(ms + SUBLN_EPS)) * g_ref[...]) * (1.0 - LAMBDA_INIT)
        o_ref[qi * TQM:(qi + 1) * TQM, hd * V_DIM:(hd + 1) * V_DIM] = y.T.astype(BF16)

    def col_max(m_acc):
        return jnp.concatenate([jnp.max(a, axis=0, keepdims=True) for a in m_acc], axis=1)

    n_chunks = n_kc + 1
    total = n_units * n_chunks
    lag = n_chunks + CHUNK_SKEW
    score_order = [n_kc] + list(range(n_kc))
    meta_pos = -lag % n_chunks
    value_order = list(range(n_kc))
    value_order.insert(meta_pos, n_kc)
    col_max_of = {}
    qt = m_acc = oe = None
    for g in range(total + lag):
        if g < total:
            u, i = divmod(g, n_chunks)
            if i == 0:
                qt, m_acc = masked_qt(u), None
            m_acc = score_chunk(u, score_order[i], qt, m_acc)
            if i == n_chunks - 1:
                col_max_of[u] = col_max(m_acc)
        if g >= lag:
            u, i = divmod(g - lag, n_chunks)
            oe = value_chunk(u, value_order[i], col_max_of[u], None if i == 0 else oe)
            if i == n_chunks - 1:
                if u % 2 == 0:
                    o0_scr[...] = oe
                else:
                    finalize(u, o0_scr[...], oe)


def _attention(lam, qt, k, vt, k_meta, vt_meta, bias, g2d, batch, seq):
    hp = ATTN_HEADS_PER_STEP
    width = hp * V_DIM
    head_rows = lambda: pl.BlockSpec((None, seq, width), lambda b, h: (b, 0, h))
    return pl.pallas_call(
        functools.partial(_attn_kernel, seq=seq, hp=hp),
        grid=(batch, N_HEADS // hp),
        in_specs=[
            pl.BlockSpec(memory_space=pltpu.SMEM),
            pl.BlockSpec(memory_space=pltpu.SMEM),
            pl.BlockSpec((width, seq), lambda b, h: (h, b)),
            head_rows(),
            pl.BlockSpec((width, seq), lambda b, h: (h, b)),
            pl.BlockSpec((META_PAD, width), lambda b, h: (0, h)),
            pl.BlockSpec((width, META_PAD), lambda b, h: (h, 0)),
            pl.BlockSpec((hp, BIAS_ROWS, TQ), lambda b, h: (h, 0, 0)),
            pl.BlockSpec((V_DIM, TQM), lambda b, h: (0, 0)),
        ],
        out_specs=head_rows(),
        out_shape=jax.ShapeDtypeStruct((batch, seq, N_HEADS * V_DIM), BF16),
        scratch_shapes=[
            pltpu.VMEM((seq + N_META, TQM), F32),
            pltpu.VMEM((seq + N_META, TQM), F32),
            pltpu.VMEM((seq + N_META, TQM), F32),
            pltpu.VMEM((hp, V_DIM + ONES_ROWS, seq), BF16),
            pltpu.VMEM((V_DIM + ONES_ROWS, TQM), F32),
            pltpu.VMEM((hp, V_DIM + ONES_ROWS, N_META), BF16),
        ],
        compiler_params=pltpu.CompilerParams(
            dimension_semantics=("arbitrary", "arbitrary"), vmem_limit_bytes=VMEM_LIMIT_BYTES),
        name="diff_attn",
    )(lam, jnp.zeros((1, 1), jnp.int32), qt, k, vt, k_meta, vt_meta, bias, g2d)


def _rms(x, g):
    ms = jnp.mean(x * x, axis=-1, keepdims=True)
    return (x * lax.rsqrt(ms + NORM_EPS)) * g


def _tail_kernel(x_ref, zp_ref, zprev_ref, znext_ref, zmeta_ref, ya_ref, pw_ref, ps_ref, wo_ref,
                 g2_ref, wg_ref, wu_ref, wd_ref, gf_ref, o_ref, e_scr, c_scr, *, tm, seq):
    tiles_per_seq = seq // tm
    t = lax.rem(pl.program_id(0), tiles_per_seq)
    p0 = t * tm
    prev = jnp.where(t == 0, zmeta_ref[...], zprev_ref[...])
    nxt = jnp.where(t == tiles_per_seq - 1, 0.0, znext_ref[...])
    zp = zp_ref[...]
    for g in range(len(POOL_WINDOWS)):
        lanes = slice(g * POOL_GROUP, (g + 1) * POOL_GROUP)
        e_scr[g, 0:HALO, :] = prev[:, lanes]
        e_scr[g, HALO:HALO + tm, :] = zp[:, lanes]
        e_scr[g, HALO + tm:2 * HALO + tm, :] = nxt[:, lanes]
        e_scr[g, 2 * HALO + tm:, :] = jnp.zeros((2 * HALO, POOL_GROUP), F32)

    rp = tm // TAIL_PARTS
    tail_row = lax.broadcasted_iota(jnp.int32, (HALO, POOL_GROUP), 0)

    def window_sum(g, w, r0):
        left = w // 2
        base = HALO + r0
        if w <= 4:
            win = e_scr[g, base - left:base - left + rp, :]
            for k in range(-left + 1, w - left):
                win = win + e_scr[g, base + k:base + k + rp, :]
            return win
        j0 = base - HALO
        src, width, n = e_scr.at[g], 1, rp + 3 * HALO
        level = 0
        while 2 * width < w:
            c_scr[level, j0:j0 + n, :] = src[j0:j0 + n, :] + src[j0 + width:j0 + width + n, :]
            src, width, n, level = c_scr.at[level], 2 * width, n - HALO, level + 1
        lo = base - left
        return src[lo:lo + rp, :] + src[lo + width:lo + width + rp, :]

    def pooled(r0):
        y_groups = []
        for g, w in enumerate(POOL_WINDOWS):
            right = w - 1 - w // 2
            win = window_sum(g, w, r0)
            center = zp_ref[r0:r0 + rp, g * POOL_GROUP:(g + 1) * POOL_GROUP]
            if r0 + rp < tm:
                d = win * (1.0 / w) - center
            else:
                body = win[0:rp - HALO] * (1.0 / w) - center[0:rp - HALO]
                pos = p0 + (tm - HALO) + tail_row
                over = jnp.maximum(pos + (right - (seq - 1)), 0)
                cnt = (w - over).astype(F32)
                d = jnp.concatenate([body, win[rp - HALO:] / cnt - center[rp - HALO:]], axis=0)
            yg = jnp.dot(d.astype(BF16), pw_ref[g], preferred_element_type=F32)
            y_groups.append((yg * ps_ref[:, g * POOL_GROUP:(g + 1) * POOL_GROUP]).astype(BF16))
        return jnp.concatenate(y_groups + [ya_ref[r0:r0 + rp, :]], axis=-1)

    def out_proj(r0, ycat):
        return x_ref[r0:r0 + rp, :] + jnp.dot(ycat, wo_ref[...], preferred_element_type=F32)

    def ffn(h1, f):
        acc = h1
        for c0 in range(0, D_FF, FF_CHUNK):
            cols = slice(c0, min(c0 + FF_CHUNK, D_FF))
            gate = jnp.dot(f, wg_ref[:, cols], preferred_element_type=F32)
            up = jnp.dot(f, wu_ref[:, cols], preferred_element_type=F32)
            act = (gate * jax.nn.sigmoid(gate) * up).astype(BF16)
            acc = acc + jnp.dot(act, wd_ref[cols, :], preferred_element_type=F32)
        return acc

    parts = [i * rp for i in range(TAIL_PARTS)]
    ycat = {parts[0]: pooled(parts[0])}
    h1, f, h2 = {}, {}, {}
    for i, r0 in enumerate(parts):
        h1[r0] = out_proj(r0, ycat[r0])
        if i + 1 < len(parts):
            ycat[parts[i + 1]] = pooled(parts[i + 1])
        f[r0] = _rms(h1[r0], g2_ref[...]).astype(BF16)
    for i, r0 in enumerate(parts):
        h2[r0] = ffn(h1[r0], f[r0])
        if i > 0:
            o_ref[parts[i - 1]:parts[i - 1] + rp, :] = _rms(h2[parts[i - 1]], gf_ref[...])
    o_ref[parts[-1]:parts[-1] + rp, :] = _rms(h2[parts[-1]], gf_ref[...])


def _tail(x2d, zp, zp_meta, y_attn, pool_w, pool_scale, w_o, g2, w_gate, w_up, w_down, g_final,
          tm, seq):
    rows = x2d.shape[0]
    halo_blocks = tm // HALO
    last_halo_block = rows // HALO - 1
    const = lambda shape: pl.BlockSpec(shape, lambda i: (0,) * len(shape),
                                       pipeline_mode=pl.Buffered(1))
    return pl.pallas_call(
        functools.partial(_tail_kernel, tm=tm, seq=seq),
        grid=(rows // tm,),
        in_specs=[
            pl.BlockSpec((tm, D_MODEL), lambda i: (i, 0)),
            pl.BlockSpec((tm, POOL_WIDTH), lambda i: (i, 0)),
            pl.BlockSpec((HALO, POOL_WIDTH), lambda i: (jnp.maximum(i * halo_blocks - 1, 0), 0)),
            pl.BlockSpec((HALO, POOL_WIDTH),
                         lambda i: (jnp.minimum((i + 1) * halo_blocks, last_halo_block), 0)),
            pl.BlockSpec((HALO, POOL_WIDTH), lambda i: (N_META // HALO - 1, 0)),
            pl.BlockSpec((tm, N_HEADS * V_DIM), lambda i: (i, 0)),
            const((len(POOL_WINDOWS), POOL_GROUP, POOL_GROUP)),
            const((1, POOL_WIDTH)),
            const((D_MODEL, D_MODEL)),
            const((1, D_MODEL)),
            const((D_MODEL, D_FF)),
            const((D_MODEL, D_FF)),
            const((D_FF, D_MODEL)),
            const((1, D_MODEL)),
        ],
        out_specs=pl.BlockSpec((tm, D_MODEL), lambda i: (i, 0)),
        out_shape=jax.ShapeDtypeStruct((rows, D_MODEL), F32),
        scratch_shapes=[pltpu.VMEM((len(POOL_WINDOWS), tm + 4 * HALO, POOL_GROUP), F32),
                        pltpu.VMEM((3, tm + 4 * HALO, POOL_GROUP), F32)],
        compiler_params=pltpu.CompilerParams(
            dimension_semantics=("arbitrary",), vmem_limit_bytes=VMEM_LIMIT_BYTES),
        name="pool_oproj_ffn",
    )(x2d, zp, zp, zp, zp_meta, y_attn, pool_w, pool_scale, w_o, g2, w_gate, w_up, w_down, g_final)


def kernel(x, meta_tokens, rel_bias, norm1_g, w_in, pool_w, pool_scale, lambda_q1, lambda_k1,
           lambda_q2, lambda_k2, subln_g, w_o, norm2_g, w_gate, w_up, w_down, final_g):
    batch, seq, _ = x.shape
    layer = 0
    lam = (jnp.exp(jnp.sum(lambda_q1[layer].astype(F32) * lambda_k1[layer].astype(F32)))
           - jnp.exp(jnp.sum(lambda_q2[layer].astype(F32) * lambda_k2[layer].astype(F32)))
           + LAMBDA_INIT).reshape(1, 1)

    col_scale = jnp.concatenate([
        jnp.ones((POOL_WIDTH,), F32), jnp.full((ATTN_WIDTH,), HEAD_DIM ** -0.5 * LOG2E, F32),
        jnp.ones((2 * ATTN_WIDTH,), F32)])
    w_in_b = (w_in[layer] * col_scale).astype(BF16)
    g1 = norm1_g[layer].reshape(1, D_MODEL)

    x2d = x.reshape(batch * seq, D_MODEL)
    zp, qt, k, vt = _inproj(x2d, g1, w_in_b, tm=1024)
    meta_pad = jnp.pad(meta_tokens, ((0, META_PAD - N_META), (0, 0)))
    zp_meta, _, k_meta, vt_meta = _inproj(meta_pad, g1, w_in_b, tm=META_PAD)

    bias = _bias_tiles(rel_bias)
    g2d = jnp.broadcast_to(subln_g[layer].astype(F32)[:, None], (V_DIM, TQM))
    y_attn = _attention(
        lam, qt, k.reshape(batch, seq, ATTN_WIDTH),
        vt, k_meta, vt_meta, bias, g2d, batch, seq)

    out = _tail(
        x2d, zp, zp_meta, y_attn.reshape(batch * seq, N_HEADS * V_DIM),
        pool_w[layer].astype(BF16), pool_scale[layer].reshape(1, POOL_WIDTH),
        w_o[layer].astype(BF16), norm2_g[layer].reshape(1, D_MODEL),
        w_gate[layer].astype(BF16), w_up[layer].astype(BF16), w_down[layer].astype(BF16),
        final_g.reshape(1, D_MODEL), tm=1024, seq=seq)
    return out.reshape(batch, seq, D_MODEL)
```

```python
import functools
import math

import jax
import jax.numpy as jnp
from jax import lax
from jax.experimental import pallas as pl
from jax.experimental.pallas import tpu as pltpu

F32 = jnp.float32
BF16 = jnp.bfloat16

D_MODEL = 1024
N_META = 16
META_PAD = 128
POOL_WIDTH = 512
POOL_WINDOWS = (2, 4, 8, 16)
POOL_GROUP = 128
ATTN_WIDTH = 512
HEAD_DIM = 64
N_HEADS = 4
V_DIM = 128
IN_WIDTH = POOL_WIDTH + 3 * ATTN_WIDTH
REL_BUCKETS = 32
REL_MAX_DIST = 128
D_FF = 2816
NORM_EPS = 1e-6
SUBLN_EPS = 1e-5
LAMBDA_INIT = 0.8 - 0.6 * math.exp(-0.3 * 0)
HALO = 8

VMEM_LIMIT_BYTES = 56 * 1024 * 1024

TQ = 256
TQM = 512
TK = 256
N_BIAS_TILES = 5
BIAS_ROWS = N_BIAS_TILES * TK + 2 * N_META
FF_CHUNK = 256
ONES_ROWS = 16
LOG2E = math.log2(math.e)
CHUNK_SKEW = 2
TAIL_PARTS = 1
ATTN_HEADS_PER_STEP = 1


def _inproj_kernel(x_ref, g_ref, w_ref, zp_ref, qt_ref, k_ref, vt_ref):
    x = x_ref[...]
    ms = jnp.mean(x * x, axis=-1, keepdims=True)
    u = ((x * lax.rsqrt(ms + NORM_EPS)) * g_ref[...]).astype(BF16)

    def proj(lo, hi):
        return jnp.dot(u, w_ref[:, lo:hi], preferred_element_type=F32)

    o_q = POOL_WIDTH
    o_k = o_q + ATTN_WIDTH
    o_v = o_k + ATTN_WIDTH
    zp_ref[...] = proj(0, o_q)
    qt_ref[...] = proj(o_q, o_k).T.astype(BF16)
    k_ref[...] = proj(o_k, o_v).astype(BF16)
    vt_ref[...] = proj(o_v, IN_WIDTH).T.astype(BF16)


def _inproj(x2d, g, w_bf16, tm):
    rows = x2d.shape[0]
    row_spec = lambda width: pl.BlockSpec((tm, width), lambda i: (i, 0))
    col_spec = lambda width: pl.BlockSpec((width, tm), lambda i: (0, i))
    return pl.pallas_call(
        _inproj_kernel,
        grid=(rows // tm,),
        in_specs=[
            row_spec(D_MODEL),
            pl.BlockSpec((1, D_MODEL), lambda i: (0, 0)),
            pl.BlockSpec((D_MODEL, IN_WIDTH), lambda i: (0, 0)),
        ],
        out_specs=[row_spec(POOL_WIDTH), col_spec(ATTN_WIDTH), row_spec(ATTN_WIDTH),
                   col_spec(ATTN_WIDTH)],
        out_shape=[
            jax.ShapeDtypeStruct((rows, POOL_WIDTH), F32),
            jax.ShapeDtypeStruct((ATTN_WIDTH, rows), BF16),
            jax.ShapeDtypeStruct((rows, ATTN_WIDTH), BF16),
            jax.ShapeDtypeStruct((ATTN_WIDTH, rows), BF16),
        ],
        compiler_params=pltpu.CompilerParams(
            dimension_semantics=("arbitrary",), vmem_limit_bytes=VMEM_LIMIT_BYTES),
        name="inproj",
    )(x2d, g, w_bf16)


def _t5_bucket(rel):
    nb = REL_BUCKETS // 2
    ret = jnp.where(rel > 0, nb, 0)
    n = jnp.abs(rel)
    max_exact = nb // 2
    nf = jnp.maximum(n, 1).astype(F32)
    large = max_exact + (jnp.log(nf / max_exact) / math.log(REL_MAX_DIST / max_exact)
                         * (nb - max_exact)).astype(jnp.int32)
    large = jnp.minimum(large, nb - 1)
    return ret + jnp.where(n < max_exact, n, large)


def _bias_buckets():
    r = lax.broadcasted_iota(jnp.int32, (TK, TQ), 0)
    c = lax.broadcasted_iota(jnp.int32, (TK, TQ), 1)
    tiles = [(t - 2) * TK + r - c for t in range(N_BIAS_TILES)]
    rm = lax.broadcasted_iota(jnp.int32, (N_META, TQ), 0)
    cm = lax.broadcasted_iota(jnp.int32, (N_META, TQ), 1)
    metas = [rm - N_META - m * TQ - cm for m in range(2)]
    return _t5_bucket(jnp.concatenate(tiles + metas, axis=0))


def _bias_kernel(rb_ref, bkt_ref, o_ref):
    h = pl.program_id(0)

    def lookup(bkt):
        level = [rb_ref[b, h] * LOG2E for b in range(REL_BUCKETS)]
        for bit in range(REL_BUCKETS.bit_length() - 1):
            odd = (bkt & (1 << bit)) != 0
            level = [jnp.where(odd, level[2 * j + 1], level[2 * j]) for j in range(len(level) // 2)]
        return level[0]

    def fill(r0, rows, constant):
        if constant:
            slab = lookup(bkt_ref[r0:r0 + 8, :])
            o_ref[r0:r0 + rows, :] = jnp.broadcast_to(slab[None], (rows // 8, 8, TQ)).reshape(rows, TQ)
        else:
            o_ref[r0:r0 + rows, :] = lookup(bkt_ref[r0:r0 + rows, :])

    for t in range(N_BIAS_TILES):
        fill(t * TK, TK, constant=t in (0, N_BIAS_TILES - 1))
    for m in range(2):
        fill(N_BIAS_TILES * TK + m * N_META, N_META, constant=m == 1)


def _bias_tiles(rel_bias):
    return pl.pallas_call(
        _bias_kernel,
        grid=(N_HEADS,),
        in_specs=[
            pl.BlockSpec(memory_space=pltpu.SMEM),
            pl.BlockSpec((BIAS_ROWS, TQ), lambda h: (0, 0)),
        ],
        out_specs=pl.BlockSpec((None, BIAS_ROWS, TQ), lambda h: (h, 0, 0)),
        out_shape=jax.ShapeDtypeStruct((N_HEADS, BIAS_ROWS, TQ), F32),
        compiler_params=pltpu.CompilerParams(dimension_semantics=("arbitrary",)),
        name="bias_tiles",
    )(rel_bias, _bias_buckets())


_NT = (((1,), (1,)), ((), ()))


def _attn_kernel(lam_ref, zero_ref, qt_ref, k_ref, vt_ref, km_ref, vmt_ref, bias_ref, g_ref, o_ref,
                 s0_scr, s1_scr, s2_scr, vt_scr, o0_scr, vmt_scr, *, seq, hp):
    n_kc = seq // TK
    units_per_head = 2 * (seq // TQM)
    n_units = hp * units_per_head
    halves = TQM // TQ
    s_scr = (s0_scr, s1_scr, s2_scr)

    for hd in range(hp):
        vt_scr[hd, 0:V_DIM, :] = vt_ref[hd * V_DIM:(hd + 1) * V_DIM, :]
        vt_scr[hd, V_DIM:, :] = jnp.ones((ONES_ROWS, seq), BF16)
        vmt_scr[hd, 0:V_DIM, :] = vmt_ref[hd * V_DIM:(hd + 1) * V_DIM, 0:N_META]
        vmt_scr[hd, V_DIM:, :] = jnp.ones((ONES_ROWS, N_META), BF16)
    lam = lam_ref[0, 0]
    z = zero_ref[0, 0]
    zero_half = jnp.zeros((HEAD_DIM, TQM), BF16)

    def masked_qt(u):
        hd, lu = divmod(u, units_per_head)
        qi, c = divmod(lu, 2)
        qt = qt_ref[hd * V_DIM:(hd + 1) * V_DIM, qi * TQM:(qi + 1) * TQM]
        if c == 0:
            return jnp.concatenate([qt[0:HEAD_DIM], zero_half], axis=0)
        return jnp.concatenate([zero_half, qt[HEAD_DIM:]], axis=0)

    def score_chunk(u, kc, qt, m_acc):
        hd, lu = divmod(u, units_per_head)
        qi = lu // 2
        head_lanes = slice(hd * V_DIM, (hd + 1) * V_DIM)
        if kc < n_kc:
            rows = pl.ds(pl.multiple_of(z + kc * TK, TK), TK)
            s = jnp.dot(k_ref[kc * TK:(kc + 1) * TK, head_lanes], qt,
                        preferred_element_type=F32)
        else:
            rows = pl.ds(pl.multiple_of(z + seq, N_META), N_META)
            s = jnp.dot(km_ref[0:N_META, head_lanes], qt, preferred_element_type=F32)
        out = []
        for j in range(halves):
            qh = qi * halves + j
            cols = slice(j * TQ, (j + 1) * TQ)
            if kc < n_kc:
                b0 = (min(max(kc - qh, -2), 2) + 2) * TK
                sc = s[:, cols] + bias_ref[hd, b0:b0 + TK, :]
            else:
                b0 = N_BIAS_TILES * TK + min(qh, 1) * N_META
                sc = s[:, cols] + bias_ref[hd, b0:b0 + N_META, :]
            s_scr[u % len(s_scr)][rows, cols] = sc
            cm = jnp.max(sc.reshape(sc.shape[0] // 8, 8, TQ), axis=0)
            out.append(cm if m_acc is None else jnp.maximum(m_acc[j], cm))
        return out

    def value_chunk(u, kc, m, oe):
        hd = u // units_per_head
        if kc < n_kc:
            rows = pl.ds(pl.multiple_of(z + kc * TK, TK), TK)
            lhs = vt_scr[hd, :, kc * TK:(kc + 1) * TK]
        else:
            rows = pl.ds(pl.multiple_of(z + seq, N_META), N_META)
            lhs = vmt_scr[hd]
        p = jnp.exp2(s_scr[u % len(s_scr)][rows, :] - m).astype(BF16)
        d = jnp.dot(lhs, p, preferred_element_type=F32)
        return d if oe is None else oe + d

    def finalize(u, oe0, oe1):
        hd, lu = divmod(u, units_per_head)
        qi = lu // 2
        o = (oe0[0:V_DIM] * (1.0 / oe0[V_DIM:V_DIM + 1])
             - oe1[0:V_DIM] * (lam / oe1[V_DIM:V_DIM + 1]))
        ms = jnp.mean(o * o, axis=0, keepdims=True)
        y = ((o * lax.rsqrt(ms + SUBLN_EPS)) * g_ref[...]) * (1.0 - LAMBDA_INIT)
        o_ref[qi * TQM:(qi + 1) * TQM, hd * V_DIM:(hd + 1) * V_DIM] = y.T.astype(BF16)

    def col_max(m_acc):
        return jnp.concatenate([jnp.max(a, axis=0, keepdims=True) for a in m_acc], axis=1)

    n_chunks = n_kc + 1
    total = n_units * n_chunks
    lag = n_chunks + CHUNK_SKEW
    score_order = [n_kc] + list(range(n_kc))
    meta_pos = -lag % n_chunks
    value_order = list(range(n_kc))
    value_order.insert(meta_pos, n_kc)
    col_max_of = {}
    qt = m_acc = oe = None
    for g in range(total + lag):
        if g < total:
            u, i = divmod(g, n_chunks)
            if i == 0:
                qt, m_acc = masked_qt(u), None
            m_acc = score_chunk(u, score_order[i], qt, m_acc)
            if i == n_chunks - 1:
                col_max_of[u] = col_max(m_acc)
        if g >= lag:
            u, i = divmod(g - lag, n_chunks)
            oe = value_chunk(u, value_order[i], col_max_of[u], None if i == 0 else oe)
            if i == n_chunks - 1:
                if u % 2 == 0:
                    o0_scr[...] = oe
                else:
                    finalize(u, o0_scr[...], oe)


def _attention(lam, qt, k, vt, k_meta, vt_meta, bias, g2d, batch, seq):
    hp = ATTN_HEADS_PER_STEP
    width = hp * V_DIM
    head_rows = lambda: pl.BlockSpec((None, seq, width), lambda h, b: (b, 0, h))
    return pl.pallas_call(
        functools.partial(_attn_kernel, seq=seq, hp=hp),
        grid=(N_HEADS // hp, batch),
        in_specs=[
            pl.BlockSpec(memory_space=pltpu.SMEM),
            pl.BlockSpec(memory_space=pltpu.SMEM),
            pl.BlockSpec((width, seq), lambda h, b: (h, b)),
            head_rows(),
            pl.BlockSpec((width, seq), lambda h, b: (h, b)),
            pl.BlockSpec((META_PAD, width), lambda h, b: (0, h)),
            pl.BlockSpec((width, META_PAD), lambda h, b: (h, 0)),
            pl.BlockSpec((hp, BIAS_ROWS, TQ), lambda h, b: (h, 0, 0)),
            pl.BlockSpec((V_DIM, TQM), lambda h, b: (0, 0)),
        ],
        out_specs=head_rows(),
        out_shape=jax.ShapeDtypeStruct((batch, seq, N_HEADS * V_DIM), BF16),
        scratch_shapes=[
            pltpu.VMEM((seq + N_META, TQM), F32),
            pltpu.VMEM((seq + N_META, TQM), F32),
            pltpu.VMEM((seq + N_META, TQM), F32),
            pltpu.VMEM((hp, V_DIM + ONES_ROWS, seq), BF16),
            pltpu.VMEM((V_DIM + ONES_ROWS, TQM), F32),
            pltpu.VMEM((hp, V_DIM + ONES_ROWS, N_META), BF16),
        ],
        compiler_params=pltpu.CompilerParams(
            dimension_semantics=("arbitrary", "arbitrary"), vmem_limit_bytes=VMEM_LIMIT_BYTES),
        name="diff_attn",
    )(lam, jnp.zeros((1, 1), jnp.int32), qt, k, vt, k_meta, vt_meta, bias, g2d)


def _rms(x, g):
    ms = jnp.mean(x * x, axis=-1, keepdims=True)
    return (x * lax.rsqrt(ms + NORM_EPS)) * g


def _tail_kernel(x_ref, zp_ref, zprev_ref, znext_ref, zmeta_ref, ya_ref, pw_ref, ps_ref, wo_ref,
                 g2_ref, wg_ref, wu_ref, wd_ref, gf_ref, o_ref, e_scr, c_scr, *, tm, seq):
    tiles_per_seq = seq // tm
    t = lax.rem(pl.program_id(0), tiles_per_seq)
    p0 = t * tm
    prev = jnp.where(t == 0, zmeta_ref[...], zprev_ref[...])
    nxt = jnp.where(t == tiles_per_seq - 1, 0.0, znext_ref[...])
    zp = zp_ref[...]
    for g in range(len(POOL_WINDOWS)):
        lanes = slice(g * POOL_GROUP, (g + 1) * POOL_GROUP)
        e_scr[g, 0:HALO, :] = prev[:, lanes]
        e_scr[g, HALO:HALO + tm, :] = zp[:, lanes]
        e_scr[g, HALO + tm:2 * HALO + tm, :] = nxt[:, lanes]
        e_scr[g, 2 * HALO + tm:, :] = jnp.zeros((2 * HALO, POOL_GROUP), F32)

    rp = tm // TAIL_PARTS
    tail_row = lax.broadcasted_iota(jnp.int32, (HALO, POOL_GROUP), 0)

    def window_sum(g, w, r0):
        left = w // 2
        base = HALO + r0
        if w <= 4:
            win = e_scr[g, base - left:base - left + rp, :]
            for k in range(-left + 1, w - left):
                win = win + e_scr[g, base + k:base + k + rp, :]
            return win
        j0 = base - HALO
        src, width, n = e_scr.at[g], 1, rp + 3 * HALO
        level = 0
        while 2 * width < w:
            c_scr[level, j0:j0 + n, :] = src[j0:j0 + n, :] + src[j0 + width:j0 + width + n, :]
            src, width, n, level = c_scr.at[level], 2 * width, n - HALO, level + 1
        lo = base - left
        return src[lo:lo + rp, :] + src[lo + width:lo + width + rp, :]

    def pooled(r0):
        y_groups = []
        for g, w in enumerate(POOL_WINDOWS):
            right = w - 1 - w // 2
            win = window_sum(g, w, r0)
            center = zp_ref[r0:r0 + rp, g * POOL_GROUP:(g + 1) * POOL_GROUP]
            if r0 + rp < tm:
                d = win * (1.0 / w) - center
            else:
                body = win[0:rp - HALO] * (1.0 / w) - center[0:rp - HALO]
                pos = p0 + (tm - HALO) + tail_row
                over = jnp.maximum(pos + (right - (seq - 1)), 0)
                cnt = (w - over).astype(F32)
                d = jnp.concatenate([body, win[rp - HALO:] / cnt - center[rp - HALO:]], axis=0)
            yg = jnp.dot(d.astype(BF16), pw_ref[g], preferred_element_type=F32)
            y_groups.append((yg * ps_ref[:, g * POOL_GROUP:(g + 1) * POOL_GROUP]).astype(BF16))
        return jnp.concatenate(y_groups + [ya_ref[r0:r0 + rp, :]], axis=-1)

    def out_proj(r0, ycat):
        return x_ref[r0:r0 + rp, :] + jnp.dot(ycat, wo_ref[...], preferred_element_type=F32)

    def ffn(h1, f):
        acc = h1
        for c0 in range(0, D_FF, FF_CHUNK):
            cols = slice(c0, min(c0 + FF_CHUNK, D_FF))
            gate = jnp.dot(f, wg_ref[:, cols], preferred_element_type=F32)
            up = jnp.dot(f, wu_ref[:, cols], preferred_element_type=F32)
            act = (gate * jax.nn.sigmoid(gate) * up).astype(BF16)
            acc = acc + jnp.dot(act, wd_ref[cols, :], preferred_element_type=F32)
        return acc

    parts = [i * rp for i in range(TAIL_PARTS)]
    ycat = {parts[0]: pooled(parts[0])}
    h1, f, h2 = {}, {}, {}
    for i, r0 in enumerate(parts):
        h1[r0] = out_proj(r0, ycat[r0])
        if i + 1 < len(parts):
            ycat[parts[i + 1]] = pooled(parts[i + 1])
        f[r0] = _rms(h1[r0], g2_ref[...]).astype(BF16)
    for i, r0 in enumerate(parts):
        h2[r0] = ffn(h1[r0], f[r0])
        if i > 0:
            o_ref[parts[i - 1]:parts[i - 1] + rp, :] = _rms(h2[parts[i - 1]], gf_ref[...])
    o_ref[parts[-1]:parts[-1] + rp, :] = _rms(h2[parts[-1]], gf_ref[...])


def _tail(x2d, zp, zp_meta, y_attn, pool_w, pool_scale, w_o, g2, w_gate, w_up, w_down, g_final,
          tm, seq):
    rows = x2d.shape[0]
    halo_blocks = tm // HALO
    last_halo_block = rows // HALO - 1
    const = lambda shape: pl.BlockSpec(shape, lambda i: (0,) * len(shape),
                                       pipeline_mode=pl.Buffered(1))
    return pl.pallas_call(
        functools.partial(_tail_kernel, tm=tm, seq=seq),
        grid=(rows // tm,),
        in_specs=[
            pl.BlockSpec((tm, D_MODEL), lambda i: (i, 0)),
            pl.BlockSpec((tm, POOL_WIDTH), lambda i: (i, 0)),
            pl.BlockSpec((HALO, POOL_WIDTH), lambda i: (jnp.maximum(i * halo_blocks - 1, 0), 0)),
            pl.BlockSpec((HALO, POOL_WIDTH),
                         lambda i: (jnp.minimum((i + 1) * halo_blocks, last_halo_block), 0)),
            pl.BlockSpec((HALO, POOL_WIDTH), lambda i: (N_META // HALO - 1, 0)),
            pl.BlockSpec((tm, N_HEADS * V_DIM), lambda i: (i, 0)),
            const((len(POOL_WINDOWS), POOL_GROUP, POOL_GROUP)),
            const((1, POOL_WIDTH)),
            const((D_MODEL, D_MODEL)),
            const((1, D_MODEL)),
            const((D_MODEL, D_FF)),
            const((D_MODEL, D_FF)),
            const((D_FF, D_MODEL)),
            const((1, D_MODEL)),
        ],
        out_specs=pl.BlockSpec((tm, D_MODEL), lambda i: (i, 0)),
        out_shape=jax.ShapeDtypeStruct((rows, D_MODEL), F32),
        scratch_shapes=[pltpu.VMEM((len(POOL_WINDOWS), tm + 4 * HALO, POOL_GROUP), F32),
                        pltpu.VMEM((3, tm + 4 * HALO, POOL_GROUP), F32)],
        compiler_params=pltpu.CompilerParams(
            dimension_semantics=("arbitrary",), vmem_limit_bytes=VMEM_LIMIT_BYTES),
        name="pool_oproj_ffn",
    )(x2d, zp, zp, zp, zp_meta, y_attn, pool_w, pool_scale, w_o, g2, w_gate, w_up, w_down, g_final)


def kernel(x, meta_tokens, rel_bias, norm1_g, w_in, pool_w, pool_scale, lambda_q1, lambda_k1,
           lambda_q2, lambda_k2, subln_g, w_o, norm2_g, w_gate, w_up, w_down, final_g):
    batch, seq, _ = x.shape
    layer = 0
    lam = (jnp.exp(jnp.sum(lambda_q1[layer].astype(F32) * lambda_k1[layer].astype(F32)))
           - jnp.exp(jnp.sum(lambda_q2[layer].astype(F32) * lambda_k2[layer].astype(F32)))
           + LAMBDA_INIT).reshape(1, 1)

    col_scale = jnp.concatenate([
        jnp.ones((POOL_WIDTH,), F32), jnp.full((ATTN_WIDTH,), HEAD_DIM ** -0.5 * LOG2E, F32),
        jnp.ones((2 * ATTN_WIDTH,), F32)])
    w_in_b = (w_in[layer] * col_scale).astype(BF16)
    g1 = norm1_g[layer].reshape(1, D_MODEL)

    x2d = x.reshape(batch * seq, D_MODEL)
    zp, qt, k, vt = _inproj(x2d, g1, w_in_b, tm=1024)
    meta_pad = jnp.pad(meta_tokens, ((0, META_PAD - N_META), (0, 0)))
    zp_meta, _, k_meta, vt_meta = _inproj(meta_pad, g1, w_in_b, tm=META_PAD)

    bias = _bias_tiles(rel_bias)
    g2d = jnp.broadcast_to(subln_g[layer].astype(F32)[:, None], (V_DIM, TQM))
    y_attn = _attention(
        lam, qt, k.reshape(batch, seq, ATTN_WIDTH),
        vt, k_meta, vt_meta, bias, g2d, batch, seq)

    out = _tail(
        x2d, zp, zp_meta, y_attn.reshape(batch * seq, N_HEADS * V_DIM),
        pool_w[layer].astype(BF16), pool_scale[layer].reshape(1, POOL_WIDTH),
        w_o[layer].astype(BF16), norm2_g[layer].reshape(1, D_MODEL),
        w_gate[layer].astype(BF16), w_up[layer].astype(BF16), w_down[layer].astype(BF16),
        final_g.reshape(1, D_MODEL), tm=1024, seq=seq)
    return out.reshape(batch, seq, D_MODEL)
```

```python
import functools
import math

import jax
import jax.numpy as jnp
from jax import lax
from jax.experimental import pallas as pl
from jax.experimental.pallas import tpu as pltpu

F32 = jnp.float32
BF16 = jnp.bfloat16

D_MODEL = 1024
N_META = 16
META_PAD = 128
POOL_WIDTH = 512
POOL_WINDOWS = (2, 4, 8, 16)
POOL_GROUP = 128
ATTN_WIDTH = 512
HEAD_DIM = 64
N_HEADS = 4
V_DIM = 128
IN_WIDTH = POOL_WIDTH + 3 * ATTN_WIDTH
REL_BUCKETS = 32
REL_MAX_DIST = 128
D_FF = 2816
NORM_EPS = 1e-6
SUBLN_EPS = 1e-5
LAMBDA_INIT = 0.8 - 0.6 * math.exp(-0.3 * 0)
HALO = 8

VMEM_LIMIT_BYTES = 56 * 1024 * 1024

TQ = 256
TQM = 512
TK = 256
N_BIAS_TILES = 5
BIAS_ROWS = N_BIAS_TILES * TK + 2 * N_META
FF_CHUNK = 256
ONES_ROWS = 16
LOG2E = math.log2(math.e)
CHUNK_SKEW = 2
TAIL_PARTS = 1
ATTN_HEADS_PER_STEP = 1


def _inproj_kernel(x_ref, g_ref, w_ref, zp_ref, qt_ref, k_ref, vt_ref):
    x = x_ref[...]
    ms = jnp.mean(x * x, axis=-1, keepdims=True)
    u = ((x * lax.rsqrt(ms + NORM_EPS)) * g_ref[...]).astype(BF16)

    def proj(lo, hi):
        return jnp.dot(u, w_ref[:, lo:hi], preferred_element_type=F32)

    o_q = POOL_WIDTH
    o_k = o_q + ATTN_WIDTH
    o_v = o_k + ATTN_WIDTH
    zp_ref[...] = proj(0, o_q)
    qt_ref[...] = proj(o_q, o_k).T.astype(BF16)
    k_ref[...] = proj(o_k, o_v).astype(BF16)
    vt_ref[...] = proj(o_v, IN_WIDTH).T.astype(BF16)


def _inproj(x2d, g, w_bf16, tm):
    rows = x2d.shape[0]
    row_spec = lambda width: pl.BlockSpec((tm, width), lambda i: (i, 0))
    col_spec = lambda width: pl.BlockSpec((width, tm), lambda i: (0, i))
    return pl.pallas_call(
        _inproj_kernel,
        grid=(rows // tm,),
        in_specs=[
            row_spec(D_MODEL),
            pl.BlockSpec((1, D_MODEL), lambda i: (0, 0)),
            pl.BlockSpec((D_MODEL, IN_WIDTH), lambda i: (0, 0)),
        ],
        out_specs=[row_spec(POOL_WIDTH), col_spec(ATTN_WIDTH), row_spec(ATTN_WIDTH),
                   col_spec(ATTN_WIDTH)],
        out_shape=[
            jax.ShapeDtypeStruct((rows, POOL_WIDTH), F32),
            jax.ShapeDtypeStruct((ATTN_WIDTH, rows), BF16),
            jax.ShapeDtypeStruct((rows, ATTN_WIDTH), BF16),
            jax.ShapeDtypeStruct((ATTN_WIDTH, rows), BF16),
        ],
        compiler_params=pltpu.CompilerParams(
            dimension_semantics=("arbitrary",), vmem_limit_bytes=VMEM_LIMIT_BYTES),
        name="inproj",
    )(x2d, g, w_bf16)


def _t5_bucket(rel):
    nb = REL_BUCKETS // 2
    ret = jnp.where(rel > 0, nb, 0)
    n = jnp.abs(rel)
    max_exact = nb // 2
    nf = jnp.maximum(n, 1).astype(F32)
    large = max_exact + (jnp.log(nf / max_exact) / math.log(REL_MAX_DIST / max_exact)
                         * (nb - max_exact)).astype(jnp.int32)
    large = jnp.minimum(large, nb - 1)
    return ret + jnp.where(n < max_exact, n, large)


def _bias_buckets():
    r = lax.broadcasted_iota(jnp.int32, (TK, TQ), 0)
    c = lax.broadcasted_iota(jnp.int32, (TK, TQ), 1)
    tiles = [(t - 2) * TK + r - c for t in range(N_BIAS_TILES)]
    rm = lax.broadcasted_iota(jnp.int32, (N_META, TQ), 0)
    cm = lax.broadcasted_iota(jnp.int32, (N_META, TQ), 1)
    metas = [rm - N_META - m * TQ - cm for m in range(2)]
    return _t5_bucket(jnp.concatenate(tiles + metas, axis=0))


def _bias_kernel(rb_ref, bkt_ref, o_ref):
    h = pl.program_id(0)

    def lookup(bkt):
        level = [rb_ref[b, h] * LOG2E for b in range(REL_BUCKETS)]
        for bit in range(REL_BUCKETS.bit_length() - 1):
            odd = (bkt & (1 << bit)) != 0
            level = [jnp.where(odd, level[2 * j + 1], level[2 * j]) for j in range(len(level) // 2)]
        return level[0]

    def fill(r0, rows, constant):
        if constant:
            slab = lookup(bkt_ref[r0:r0 + 8, :])
            o_ref[r0:r0 + rows, :] = jnp.broadcast_to(slab[None], (rows // 8, 8, TQ)).reshape(rows, TQ)
        else:
            o_ref[r0:r0 + rows, :] = lookup(bkt_ref[r0:r0 + rows, :])

    for t in range(N_BIAS_TILES):
        fill(t * TK, TK, constant=t in (0, N_BIAS_TILES - 1))
    for m in range(2):
        fill(N_BIAS_TILES * TK + m * N_META, N_META, constant=m == 1)


def _bias_tiles(rel_bias):
    return pl.pallas_call(
        _bias_kernel,
        grid=(N_HEADS,),
        in_specs=[
            pl.BlockSpec(memory_space=pltpu.SMEM),
            pl.BlockSpec((BIAS_ROWS, TQ), lambda h: (0, 0)),
        ],
        out_specs=pl.BlockSpec((None, BIAS_ROWS, TQ), lambda h: (h, 0, 0)),
        out_shape=jax.ShapeDtypeStruct((N_HEADS, BIAS_ROWS, TQ), F32),
        compiler_params=pltpu.CompilerParams(dimension_semantics=("arbitrary",)),
        name="bias_tiles",
    )(rel_bias, _bias_buckets())


_NT = (((1,), (1,)), ((), ()))


def _attn_kernel(lam_ref, zero_ref, qt_ref, k_ref, vt_ref, km_ref, vmt_ref, bias_ref, g_ref, o_ref,
                 s0_scr, s1_scr, s2_scr, vt_scr, o0_scr, vmt_scr, *, seq, hp):
    n_kc = seq // TK
    units_per_head = 2 * (seq // TQM)
    n_units = hp * units_per_head
    halves = TQM // TQ
    s_scr = (s0_scr, s1_scr, s2_scr)

    for hd in range(hp):
        vt_scr[hd, 0:V_DIM, :] = vt_ref[hd * V_DIM:(hd + 1) * V_DIM, :]
        vt_scr[hd, V_DIM:, :] = jnp.ones((ONES_ROWS, seq), BF16)
        vmt_scr[hd, 0:V_DIM, :] = vmt_ref[hd * V_DIM:(hd + 1) * V_DIM, 0:N_META]
        vmt_scr[hd, V_DIM:, :] = jnp.ones((ONES_ROWS, N_META), BF16)
    lam = lam_ref[0, 0]
    z = zero_ref[0, 0]
    zero_half = jnp.zeros((HEAD_DIM, TQM), BF16)

    def masked_qt(u):
        hd, lu = divmod(u, units_per_head)
        qi, c = divmod(lu, 2)
        qt = qt_ref[hd * V_DIM:(hd + 1) * V_DIM, qi * TQM:(qi + 1) * TQM]
        if c == 0:
            return jnp.concatenate([qt[0:HEAD_DIM], zero_half], axis=0)
        return jnp.concatenate([zero_half, qt[HEAD_DIM:]], axis=0)

    def bias_tile(kc, qh):
        if kc < n_kc:
            t = min(max(kc - qh, -2), 2) + 2
            return t * TK, TK, t in (0, N_BIAS_TILES - 1)
        return N_BIAS_TILES * TK + min(qh, 1) * N_META, N_META, qh >= 1

    def score_chunk(u, kc, qt, m_acc):
        hd, lu = divmod(u, units_per_head)
        qi = lu // 2
        head_lanes = slice(hd * V_DIM, (hd + 1) * V_DIM)
        if kc < n_kc:
            rows = pl.ds(pl.multiple_of(z + kc * TK, TK), TK)
            s = jnp.dot(k_ref[kc * TK:(kc + 1) * TK, head_lanes], qt,
                        preferred_element_type=F32)
        else:
            rows = pl.ds(pl.multiple_of(z + seq, N_META), N_META)
            s = jnp.dot(km_ref[0:N_META, head_lanes], qt, preferred_element_type=F32)
        m_acc = [dict(a) for a in m_acc] if m_acc else [{} for _ in range(halves)]
        for j in range(halves):
            cols = slice(j * TQ, (j + 1) * TQ)
            b0, nrows, constant = bias_tile(kc, qi * halves + j)
            sc = s[:, cols] if constant else s[:, cols] + bias_ref[hd, b0:b0 + nrows, :]
            s_scr[u % len(s_scr)][rows, cols] = sc
            cm = jnp.max(sc.reshape(nrows // 8, 8, TQ), axis=0)
            key = b0 if constant else None
            m_acc[j][key] = jnp.maximum(m_acc[j][key], cm) if key in m_acc[j] else cm
        return m_acc

    def col_max(u, m_acc):
        hd = u // units_per_head
        out = []
        for acc in m_acc:
            m = None
            for key, a in acc.items():
                v = jnp.max(a, axis=0, keepdims=True)
                if key is not None:
                    v = v + bias_ref[hd, key:key + 1, :]
                m = v if m is None else jnp.maximum(m, v)
            out.append(m)
        return out

    def exp_offset(u, kc, m, cache):
        hd, lu = divmod(u, units_per_head)
        tiles = tuple(bias_tile(kc, (lu // 2) * halves + j) for j in range(halves))
        key = tuple(b0 if constant else None for b0, _, constant in tiles)
        if key not in cache:
            cache[key] = jnp.concatenate(
                [m[j] if b0 is None else m[j] - bias_ref[hd, b0:b0 + 1, :]
                 for j, b0 in enumerate(key)], axis=1)
        return cache[key]

    def value_chunk(u, kc, off, oe):
        hd = u // units_per_head
        if kc < n_kc:
            rows = pl.ds(pl.multiple_of(z + kc * TK, TK), TK)
            lhs = vt_scr[hd, :, kc * TK:(kc + 1) * TK]
        else:
            rows = pl.ds(pl.multiple_of(z + seq, N_META), N_META)
            lhs = vmt_scr[hd]
        p = jnp.exp2(s_scr[u % len(s_scr)][rows, :] - off).astype(BF16)
        d = jnp.dot(lhs, p, preferred_element_type=F32)
        return d if oe is None else oe + d

    def finalize(u, oe0, oe1):
        hd, lu = divmod(u, units_per_head)
        qi = lu // 2
        o = (oe0[0:V_DIM] * (1.0 / oe0[V_DIM:V_DIM + 1])
             - oe1[0:V_DIM] * (lam / oe1[V_DIM:V_DIM + 1]))
        ms = jnp.mean(o * o, axis=0, keepdims=True)
        y = ((o * lax.rsqrt(ms + SUBLN_EPS)) * g_ref[...]) * (1.0 - LAMBDA_INIT)
        o_ref[qi * TQM:(qi + 1) * TQM, hd * V_DIM:(hd + 1) * V_DIM] = y.T.astype(BF16)

    n_chunks = n_kc + 1
    total = n_units * n_chunks
    lag = n_chunks + CHUNK_SKEW
    score_order = [n_kc] + list(range(n_kc))
    meta_pos = -lag % n_chunks
    value_order = list(range(n_kc))
    value_order.insert(meta_pos, n_kc)
    col_max_of, offsets_of = {}, {}
    qt = m_acc = oe = None
    for g in range(total + lag):
        if g < total:
            u, i = divmod(g, n_chunks)
            if i == 0:
                qt, m_acc = masked_qt(u), None
            m_acc = score_chunk(u, score_order[i], qt, m_acc)
            if i == n_chunks - 1:
                col_max_of[u], offsets_of[u] = col_max(u, m_acc), {}
        if g >= lag:
            u, i = divmod(g - lag, n_chunks)
            off = exp_offset(u, value_order[i], col_max_of[u], offsets_of[u])
            oe = value_chunk(u, value_order[i], off, None if i == 0 else oe)
            if i == n_chunks - 1:
                if u % 2 == 0:
                    o0_scr[...] = oe
                else:
                    finalize(u, o0_scr[...], oe)


def _attention(lam, qt, k, vt, k_meta, vt_meta, bias, g2d, batch, seq):
    hp = ATTN_HEADS_PER_STEP
    width = hp * V_DIM
    head_rows = lambda: pl.BlockSpec((None, seq, width), lambda h, b: (b, 0, h))
    return pl.pallas_call(
        functools.partial(_attn_kernel, seq=seq, hp=hp),
        grid=(N_HEADS // hp, batch),
        in_specs=[
            pl.BlockSpec(memory_space=pltpu.SMEM),
            pl.BlockSpec(memory_space=pltpu.SMEM),
            pl.BlockSpec((width, seq), lambda h, b: (h, b)),
            head_rows(),
            pl.BlockSpec((width, seq), lambda h, b: (h, b)),
            pl.BlockSpec((META_PAD, width), lambda h, b: (0, h)),
            pl.BlockSpec((width, META_PAD), lambda h, b: (h, 0)),
            pl.BlockSpec((hp, BIAS_ROWS, TQ), lambda h, b: (h, 0, 0)),
            pl.BlockSpec((V_DIM, TQM), lambda h, b: (0, 0)),
        ],
        out_specs=head_rows(),
        out_shape=jax.ShapeDtypeStruct((batch, seq, N_HEADS * V_DIM), BF16),
        scratch_shapes=[
            pltpu.VMEM((seq + N_META, TQM), F32),
            pltpu.VMEM((seq + N_META, TQM), F32),
            pltpu.VMEM((seq + N_META, TQM), F32),
            pltpu.VMEM((hp, V_DIM + ONES_ROWS, seq), BF16),
            pltpu.VMEM((V_DIM + ONES_ROWS, TQM), F32),
            pltpu.VMEM((hp, V_DIM + ONES_ROWS, N_META), BF16),
        ],
        compiler_params=pltpu.CompilerParams(
            dimension_semantics=("arbitrary", "arbitrary"), vmem_limit_bytes=VMEM_LIMIT_BYTES),
        name="diff_attn",
    )(lam, jnp.zeros((1, 1), jnp.int32), qt, k, vt, k_meta, vt_meta, bias, g2d)


def _rms(x, g):
    ms = jnp.mean(x * x, axis=-1, keepdims=True)
    return (x * lax.rsqrt(ms + NORM_EPS)) * g


def _tail_kernel(x_ref, zp_ref, zprev_ref, znext_ref, zmeta_ref, ya_ref, pw_ref, ps_ref, wo_ref,
                 g2_ref, wg_ref, wu_ref, wd_ref, gf_ref, o_ref, e_scr, c_scr, *, tm, seq):
    tiles_per_seq = seq // tm
    t = lax.rem(pl.program_id(0), tiles_per_seq)
    p0 = t * tm
    prev = jnp.where(t == 0, zmeta_ref[...], zprev_ref[...])
    nxt = jnp.where(t == tiles_per_seq - 1, 0.0, znext_ref[...])
    zp = zp_ref[...]
    for g in range(len(POOL_WINDOWS)):
        lanes = slice(g * POOL_GROUP, (g + 1) * POOL_GROUP)
        e_scr[g, 0:HALO, :] = prev[:, lanes]
        e_scr[g, HALO:HALO + tm, :] = zp[:, lanes]
        e_scr[g, HALO + tm:2 * HALO + tm, :] = nxt[:, lanes]
        e_scr[g, 2 * HALO + tm:, :] = jnp.zeros((2 * HALO, POOL_GROUP), F32)

    rp = tm // TAIL_PARTS
    tail_row = lax.broadcasted_iota(jnp.int32, (HALO, POOL_GROUP), 0)

    def window_sum(g, w, r0):
        left = w // 2
        base = HALO + r0
        if w <= 4:
            win = e_scr[g, base - left:base - left + rp, :]
            for k in range(-left + 1, w - left):
                win = win + e_scr[g, base + k:base + k + rp, :]
            return win
        j0 = base - HALO
        src, width, n = e_scr.at[g], 1, rp + 3 * HALO
        level = 0
        while 2 * width < w:
            c_scr[level, j0:j0 + n, :] = src[j0:j0 + n, :] + src[j0 + width:j0 + width + n, :]
            src, width, n, level = c_scr.at[level], 2 * width, n - HALO, level + 1
        lo = base - left
        return src[lo:lo + rp, :] + src[lo + width:lo + width + rp, :]

    def pooled(r0):
        y_groups = []
        for g, w in enumerate(POOL_WINDOWS):
            right = w - 1 - w // 2
            win = window_sum(g, w, r0)
            center = zp_ref[r0:r0 + rp, g * POOL_GROUP:(g + 1) * POOL_GROUP]
            if r0 + rp < tm:
                d = win * (1.0 / w) - center
            else:
                body = win[0:rp - HALO] * (1.0 / w) - center[0:rp - HALO]
                pos = p0 + (tm - HALO) + tail_row
                over = jnp.maximum(pos + (right - (seq - 1)), 0)
                cnt = (w - over).astype(F32)
                d = jnp.concatenate([body, win[rp - HALO:] / cnt - center[rp - HALO:]], axis=0)
            yg = jnp.dot(d.astype(BF16), pw_ref[g], preferred_element_type=F32)
            y_groups.append((yg * ps_ref[:, g * POOL_GROUP:(g + 1) * POOL_GROUP]).astype(BF16))
        return jnp.concatenate(y_groups + [ya_ref[r0:r0 + rp, :]], axis=-1)

    def out_proj(r0, ycat):
        return x_ref[r0:r0 + rp, :] + jnp.dot(ycat, wo_ref[...], preferred_element_type=F32)

    def ffn(h1, f):
        acc = h1
        for c0 in range(0, D_FF, FF_CHUNK):
            cols = slice(c0, min(c0 + FF_CHUNK, D_FF))
            gate = jnp.dot(f, wg_ref[:, cols], preferred_element_type=F32)
            up = jnp.dot(f, wu_ref[:, cols], preferred_element_type=F32)
            act = (gate * jax.nn.sigmoid(gate) * up).astype(BF16)
            acc = acc + jnp.dot(act, wd_ref[cols, :], preferred_element_type=F32)
        return acc

    parts = [i * rp for i in range(TAIL_PARTS)]
    ycat = {parts[0]: pooled(parts[0])}
    h1, f, h2 = {}, {}, {}
    for i, r0 in enumerate(parts):
        h1[r0] = out_proj(r0, ycat[r0])
        if i + 1 < len(parts):
            ycat[parts[i + 1]] = pooled(parts[i + 1])
        f[r0] = _rms(h1[r0], g2_ref[...]).astype(BF16)
    for i, r0 in enumerate(parts):
        h2[r0] = ffn(h1[r0], f[r0])
        if i > 0:
            o_ref[parts[i - 1]:parts[i - 1] + rp, :] = _rms(h2[parts[i - 1]], gf_ref[...])
    o_ref[parts[-1]:parts[-1] + rp, :] = _rms(h2[parts[-1]], gf_ref[...])


def _tail(x2d, zp, zp_meta, y_attn, pool_w, pool_scale, w_o, g2, w_gate, w_up, w_down, g_final,
          tm, seq):
    rows = x2d.shape[0]
    halo_blocks = tm // HALO
    last_halo_block = rows // HALO - 1
    const = lambda shape: pl.BlockSpec(shape, lambda i: (0,) * len(shape),
                                       pipeline_mode=pl.Buffered(1))
    return pl.pallas_call(
        functools.partial(_tail_kernel, tm=tm, seq=seq),
        grid=(rows // tm,),
        in_specs=[
            pl.BlockSpec((tm, D_MODEL), lambda i: (i, 0)),
            pl.BlockSpec((tm, POOL_WIDTH), lambda i: (i, 0)),
            pl.BlockSpec((HALO, POOL_WIDTH), lambda i: (jnp.maximum(i * halo_blocks - 1, 0), 0)),
            pl.BlockSpec((HALO, POOL_WIDTH),
                         lambda i: (jnp.minimum((i + 1) * halo_blocks, last_halo_block), 0)),
            pl.BlockSpec((HALO, POOL_WIDTH), lambda i: (N_META // HALO - 1, 0)),
            pl.BlockSpec((tm, N_HEADS * V_DIM), lambda i: (i, 0)),
            const((len(POOL_WINDOWS), POOL_GROUP, POOL_GROUP)),
            const((1, POOL_WIDTH)),
            const((D_MODEL, D_MODEL)),
            const((1, D_MODEL)),
            const((D_MODEL, D_FF)),
            const((D_MODEL, D_FF)),
            const((D_FF, D_MODEL)),
            const((1, D_MODEL)),
        ],
        out_specs=pl.BlockSpec((tm, D_MODEL), lambda i: (i, 0)),
        out_shape=jax.ShapeDtypeStruct((rows, D_MODEL), F32),
        scratch_shapes=[pltpu.VMEM((len(POOL_WINDOWS), tm + 4 * HALO, POOL_GROUP), F32),
                        pltpu.VMEM((3, tm + 4 * HALO, POOL_GROUP), F32)],
        compiler_params=pltpu.CompilerParams(
            dimension_semantics=("arbitrary",), vmem_limit_bytes=VMEM_LIMIT_BYTES),
        name="pool_oproj_ffn",
    )(x2d, zp, zp, zp, zp_meta, y_attn, pool_w, pool_scale, w_o, g2, w_gate, w_up, w_down, g_final)


def kernel(x, meta_tokens, rel_bias, norm1_g, w_in, pool_w, pool_scale, lambda_q1, lambda_k1,
           lambda_q2, lambda_k2, subln_g, w_o, norm2_g, w_gate, w_up, w_down, final_g):
    batch, seq, _ = x.shape
    layer = 0
    lam = (jnp.exp(jnp.sum(lambda_q1[layer].astype(F32) * lambda_k1[layer].astype(F32)))
           - jnp.exp(jnp.sum(lambda_q2[layer].astype(F32) * lambda_k2[layer].astype(F32)))
           + LAMBDA_INIT).reshape(1, 1)

    col_scale = jnp.concatenate([
        jnp.ones((POOL_WIDTH,), F32), jnp.full((ATTN_WIDTH,), HEAD_DIM ** -0.5 * LOG2E, F32),
        jnp.ones((2 * ATTN_WIDTH,), F32)])
    w_in_b = (w_in[layer] * col_scale).astype(BF16)
    g1 = norm1_g[layer].reshape(1, D_MODEL)

    x2d = x.reshape(batch * seq, D_MODEL)
    zp, qt, k, vt = _inproj(x2d, g1, w_in_b, tm=1024)
    meta_pad = jnp.pad(meta_tokens, ((0, META_PAD - N_META), (0, 0)))
    zp_meta, _, k_meta, vt_meta = _inproj(meta_pad, g1, w_in_b, tm=META_PAD)

    bias = _bias_tiles(rel_bias)
    g2d = jnp.broadcast_to(subln_g[layer].astype(F32)[:, None], (V_DIM, TQM))
    y_attn = _attention(
        lam, qt, k.reshape(batch, seq, ATTN_WIDTH),
        vt, k_meta, vt_meta, bias, g2d, batch, seq)

    out = _tail(
        x2d, zp, zp_meta, y_attn.reshape(batch * seq, N_HEADS * V_DIM),
        pool_w[layer].astype(BF16), pool_scale[layer].reshape(1, POOL_WIDTH),
        w_o[layer].astype(BF16), norm2_g[layer].reshape(1, D_MODEL),
        w_gate[layer].astype(BF16), w_up[layer].astype(BF16), w_down[layer].astype(BF16),
        final_g.reshape(1, D_MODEL), tm=1024, seq=seq)
    return out.reshape(batch, seq, D_MODEL)
```

```python
import functools
import math

import jax
import jax.numpy as jnp
from jax import lax
from jax.experimental import pallas as pl
from jax.experimental.pallas import tpu as pltpu

F32 = jnp.float32
BF16 = jnp.bfloat16

D_MODEL = 1024
N_META = 16
META_PAD = 128
POOL_WIDTH = 512
POOL_WINDOWS = (2, 4, 8, 16)
POOL_GROUP = 128
ATTN_WIDTH = 512
HEAD_DIM = 64
N_HEADS = 4
V_DIM = 128
IN_WIDTH = POOL_WIDTH + 3 * ATTN_WIDTH
REL_BUCKETS = 32
REL_MAX_DIST = 128
D_FF = 2816
NORM_EPS = 1e-6
SUBLN_EPS = 1e-5
LAMBDA_INIT = 0.8 - 0.6 * math.exp(-0.3 * 0)
HALO = 8

VMEM_LIMIT_BYTES = 56 * 1024 * 1024
BF16_SUBLANES = 16

TQ = 256
TQM = 512
TK = 256
N_BIAS_TILES = 5
BIAS_ROWS = N_BIAS_TILES * TK + 2 * N_META
FF_CHUNK = 256
ONES_ROWS = 16
LOG2E = math.log2(math.e)
CHUNK_SKEW = 2
TAIL_PARTS = 1
ATTN_HEADS_PER_STEP = 1


def _inproj_kernel(x_ref, g_ref, w_ref, cs_ref, *refs, n_cast):
    cast_in, outs, cast_out = refs[:n_cast], refs[n_cast:n_cast + 4], refs[n_cast + 4:-1]
    zp_ref, qt_ref, k_ref, vt_ref = outs
    wb_scr = refs[-1]

    @pl.when(pl.program_id(0) == 0)
    def _():
        wb_scr[...] = (w_ref[...] * cs_ref[...]).astype(BF16)

    for src, dst in zip(cast_in, cast_out):
        dst[...] = src[...].astype(BF16)

    x = x_ref[...]
    ms = jnp.mean(x * x, axis=-1, keepdims=True)
    u = ((x * lax.rsqrt(ms + NORM_EPS)) * g_ref[...]).astype(BF16)

    def proj(lo, hi):
        return jnp.dot(u, wb_scr[:, lo:hi], preferred_element_type=F32)

    o_q = POOL_WIDTH
    o_k = o_q + ATTN_WIDTH
    o_v = o_k + ATTN_WIDTH
    zp_ref[...] = proj(0, o_q)
    qt_ref[...] = proj(o_q, o_k).T.astype(BF16)
    k_ref[...] = proj(o_k, o_v).astype(BF16)
    vt_ref[...] = proj(o_v, IN_WIDTH).T.astype(BF16)


def _inproj(x2d, g, w_in, col_scale, tm, cast_weights=()):
    rows = x2d.shape[0]
    steps = rows // tm
    row_spec = lambda width: pl.BlockSpec((tm, width), lambda i: (i, 0))
    col_spec = lambda width: pl.BlockSpec((width, tm), lambda i: (0, i))
    const = lambda shape: pl.BlockSpec(shape, lambda i: (0, 0), pipeline_mode=pl.Buffered(1))

    def cast_spec(w):
        n_rows, n_cols = w.shape
        block = min(d for d in range(BF16_SUBLANES, n_rows + 1, BF16_SUBLANES)
                    if n_rows % d == 0 and d * steps >= n_rows)
        last = n_rows // block - 1
        return pl.BlockSpec((block, n_cols), lambda i: (jnp.minimum(i, last), 0))

    cast_specs = [cast_spec(w) for w in cast_weights]
    return pl.pallas_call(
        functools.partial(_inproj_kernel, n_cast=len(cast_weights)),
        grid=(steps,),
        in_specs=[
            row_spec(D_MODEL),
            const((1, D_MODEL)),
            const((D_MODEL, IN_WIDTH)),
            const((1, IN_WIDTH)),
            *cast_specs,
        ],
        out_specs=[row_spec(POOL_WIDTH), col_spec(ATTN_WIDTH), row_spec(ATTN_WIDTH),
                   col_spec(ATTN_WIDTH), *cast_specs],
        out_shape=[
            jax.ShapeDtypeStruct((rows, POOL_WIDTH), F32),
            jax.ShapeDtypeStruct((ATTN_WIDTH, rows), BF16),
            jax.ShapeDtypeStruct((rows, ATTN_WIDTH), BF16),
            jax.ShapeDtypeStruct((ATTN_WIDTH, rows), BF16),
            *[jax.ShapeDtypeStruct(w.shape, BF16) for w in cast_weights],
        ],
        scratch_shapes=[pltpu.VMEM((D_MODEL, IN_WIDTH), BF16)],
        compiler_params=pltpu.CompilerParams(
            dimension_semantics=("arbitrary",), vmem_limit_bytes=VMEM_LIMIT_BYTES),
        name="inproj",
    )(x2d, g, w_in, col_scale, *cast_weights)


def _t5_bucket(rel):
    nb = REL_BUCKETS // 2
    ret = jnp.where(rel > 0, nb, 0)
    n = jnp.abs(rel)
    max_exact = nb // 2
    nf = jnp.maximum(n, 1).astype(F32)
    large = max_exact + (jnp.log(nf / max_exact) / math.log(REL_MAX_DIST / max_exact)
                         * (nb - max_exact)).astype(jnp.int32)
    large = jnp.minimum(large, nb - 1)
    return ret + jnp.where(n < max_exact, n, large)


def _bias_buckets():
    r = lax.broadcasted_iota(jnp.int32, (TK, TQ), 0)
    c = lax.broadcasted_iota(jnp.int32, (TK, TQ), 1)
    tiles = [(t - 2) * TK + r - c for t in range(N_BIAS_TILES)]
    rm = lax.broadcasted_iota(jnp.int32, (N_META, TQ), 0)
    cm = lax.broadcasted_iota(jnp.int32, (N_META, TQ), 1)
    metas = [rm - N_META - m * TQ - cm for m in range(2)]
    return _t5_bucket(jnp.concatenate(tiles + metas, axis=0))


def _bias_kernel(rb_ref, bkt_ref, o_ref):
    h = pl.program_id(0)

    def lookup(bkt):
        level = [rb_ref[b, h] * LOG2E for b in range(REL_BUCKETS)]
        for bit in range(REL_BUCKETS.bit_length() - 1):
            odd = (bkt & (1 << bit)) != 0
            level = [jnp.where(odd, level[2 * j + 1], level[2 * j]) for j in range(len(level) // 2)]
        return level[0]

    def fill(r0, rows, constant):
        if constant:
            slab = lookup(bkt_ref[r0:r0 + 8, :])
            o_ref[r0:r0 + rows, :] = jnp.broadcast_to(slab[None], (rows // 8, 8, TQ)).reshape(rows, TQ)
        else:
            o_ref[r0:r0 + rows, :] = lookup(bkt_ref[r0:r0 + rows, :])

    for t in range(N_BIAS_TILES):
        fill(t * TK, TK, constant=t in (0, N_BIAS_TILES - 1))
    for m in range(2):
        fill(N_BIAS_TILES * TK + m * N_META, N_META, constant=m == 1)


def _bias_tiles(rel_bias):
    return pl.pallas_call(
        _bias_kernel,
        grid=(N_HEADS,),
        in_specs=[
            pl.BlockSpec(memory_space=pltpu.SMEM),
            pl.BlockSpec((BIAS_ROWS, TQ), lambda h: (0, 0)),
        ],
        out_specs=pl.BlockSpec((None, BIAS_ROWS, TQ), lambda h: (h, 0, 0)),
        out_shape=jax.ShapeDtypeStruct((N_HEADS, BIAS_ROWS, TQ), F32),
        compiler_params=pltpu.CompilerParams(dimension_semantics=("arbitrary",)),
        name="bias_tiles",
    )(rel_bias, _bias_buckets())


_NT = (((1,), (1,)), ((), ()))


def _attn_kernel(lam_ref, zero_ref, qt_ref, k_ref, vt_ref, km_ref, vmt_ref, bias_ref, g_ref, o_ref,
                 s0_scr, s1_scr, s2_scr, vt_scr, o0_scr, vmt_scr, *, seq, hp):
    n_kc = seq // TK
    units_per_head = 2 * (seq // TQM)
    n_units = hp * units_per_head
    halves = TQM // TQ
    s_scr = (s0_scr, s1_scr, s2_scr)

    for hd in range(hp):
        vt_scr[hd, 0:V_DIM, :] = vt_ref[hd * V_DIM:(hd + 1) * V_DIM, :]
        vt_scr[hd, V_DIM:, :] = jnp.ones((ONES_ROWS, seq), BF16)
        vmt_scr[hd, 0:V_DIM, :] = vmt_ref[hd * V_DIM:(hd + 1) * V_DIM, 0:N_META]
        vmt_scr[hd, V_DIM:, :] = jnp.ones((ONES_ROWS, N_META), BF16)
    lam = lam_ref[0, 0]
    z = zero_ref[0, 0]
    zero_half = jnp.zeros((HEAD_DIM, TQM), BF16)

    def masked_qt(u):
        hd, lu = divmod(u, units_per_head)
        qi, c = divmod(lu, 2)
        qt = qt_ref[hd * V_DIM:(hd + 1) * V_DIM, qi * TQM:(qi + 1) * TQM]
        if c == 0:
            return jnp.concatenate([qt[0:HEAD_DIM], zero_half], axis=0)
        return jnp.concatenate([zero_half, qt[HEAD_DIM:]], axis=0)

    def bias_tile(kc, qh):
        if kc < n_kc:
            t = min(max(kc - qh, -2), 2) + 2
            return t * TK, TK, t in (0, N_BIAS_TILES - 1)
        return N_BIAS_TILES * TK + min(qh, 1) * N_META, N_META, qh >= 1

    def score_chunk(u, kc, qt, m_acc):
        hd, lu = divmod(u, units_per_head)
        qi = lu // 2
        head_lanes = slice(hd * V_DIM, (hd + 1) * V_DIM)
        if kc < n_kc:
            rows = pl.ds(pl.multiple_of(z + kc * TK, TK), TK)
            s = jnp.dot(k_ref[kc * TK:(kc + 1) * TK, head_lanes], qt,
                        preferred_element_type=F32)
        else:
            rows = pl.ds(pl.multiple_of(z + seq, N_META), N_META)
            s = jnp.dot(km_ref[0:N_META, head_lanes], qt, preferred_element_type=F32)
        m_acc = [dict(a) for a in m_acc] if m_acc else [{} for _ in range(halves)]
        for j in range(halves):
            cols = slice(j * TQ, (j + 1) * TQ)
            b0, nrows, constant = bias_tile(kc, qi * halves + j)
            sc = s[:, cols] if constant else s[:, cols] + bias_ref[hd, b0:b0 + nrows, :]
            s_scr[u % len(s_scr)][rows, cols] = sc
            cm = jnp.max(sc.reshape(nrows // 8, 8, TQ), axis=0)
            key = b0 if constant else None
            m_acc[j][key] = jnp.maximum(m_acc[j][key], cm) if key in m_acc[j] else cm
        return m_acc

    def col_max(u, m_acc):
        hd = u // units_per_head
        out = []
        for acc in m_acc:
            m = None
            for key, a in acc.items():
                v = jnp.max(a, axis=0, keepdims=True)
                if key is not None:
                    v = v + bias_ref[hd, key:key + 1, :]
                m = v if m is None else jnp.maximum(m, v)
            out.append(m)
        return out

    def exp_offset(u, kc, m, cache):
        hd, lu = divmod(u, units_per_head)
        tiles = tuple(bias_tile(kc, (lu // 2) * halves + j) for j in range(halves))
        key = tuple(b0 if constant else None for b0, _, constant in tiles)
        if key not in cache:
            cache[key] = jnp.concatenate(
                [m[j] if b0 is None else m[j] - bias_ref[hd, b0:b0 + 1, :]
                 for j, b0 in enumerate(key)], axis=1)
        return cache[key]

    def value_chunk(u, kc, off, oe):
        hd = u // units_per_head
        if kc < n_kc:
            rows = pl.ds(pl.multiple_of(z + kc * TK, TK), TK)
            lhs = vt_scr[hd, :, kc * TK:(kc + 1) * TK]
        else:
            rows = pl.ds(pl.multiple_of(z + seq, N_META), N_META)
            lhs = vmt_scr[hd]
        p = jnp.exp2(s_scr[u % len(s_scr)][rows, :] - off).astype(BF16)
        d = jnp.dot(lhs, p, preferred_element_type=F32)
        return d if oe is None else oe + d

    def finalize(u, oe0, oe1):
        hd, lu = divmod(u, units_per_head)
        qi = lu // 2
        o = (oe0[0:V_DIM] * (1.0 / oe0[V_DIM:V_DIM + 1])
             - oe1[0:V_DIM] * (lam / oe1[V_DIM:V_DIM + 1]))
        ms = jnp.mean(o * o, axis=0, keepdims=True)
        y = ((o * lax.rsqrt(ms + SUBLN_EPS)) * g_ref[...]) * (1.0 - LAMBDA_INIT)
        o_ref[qi * TQM:(qi + 1) * TQM, hd * V_DIM:(hd + 1) * V_DIM] = y.T.astype(BF16)

    n_chunks = n_kc + 1
    total = n_units * n_chunks
    lag = n_chunks + CHUNK_SKEW
    score_order = [n_kc] + list(range(n_kc))
    meta_pos = -lag % n_chunks
    value_order = list(range(n_kc))
    value_order.insert(meta_pos, n_kc)
    col_max_of, offsets_of = {}, {}
    qt = m_acc = oe = None
    for g in range(total + lag):
        if g < total:
            u, i = divmod(g, n_chunks)
            if i == 0:
                qt, m_acc = masked_qt(u), None
            m_acc = score_chunk(u, score_order[i], qt, m_acc)
            if i == n_chunks - 1:
                col_max_of[u], offsets_of[u] = col_max(u, m_acc), {}
        if g >= lag:
            u, i = divmod(g - lag, n_chunks)
            off = exp_offset(u, value_order[i], col_max_of[u], offsets_of[u])
            oe = value_chunk(u, value_order[i], off, None if i == 0 else oe)
            if i == n_chunks - 1:
                if u % 2 == 0:
                    o0_scr[...] = oe
                else:
                    finalize(u, o0_scr[...], oe)


def _attention(lam, qt, k, vt, k_meta, vt_meta, bias, g2d, batch, seq):
    hp = ATTN_HEADS_PER_STEP
    width = hp * V_DIM
    head_rows = lambda: pl.BlockSpec((None, seq, width), lambda h, b: (b, 0, h))
    return pl.pallas_call(
        functools.partial(_attn_kernel, seq=seq, hp=hp),
        grid=(N_HEADS // hp, batch),
        in_specs=[
            pl.BlockSpec(memory_space=pltpu.SMEM),
            pl.BlockSpec(memory_space=pltpu.SMEM),
            pl.BlockSpec((width, seq), lambda h, b: (h, b)),
            head_rows(),
            pl.BlockSpec((width, seq), lambda h, b: (h, b)),
            pl.BlockSpec((META_PAD, width), lambda h, b: (0, h)),
            pl.BlockSpec((width, META_PAD), lambda h, b: (h, 0)),
            pl.BlockSpec((hp, BIAS_ROWS, TQ), lambda h, b: (h, 0, 0)),
            pl.BlockSpec((V_DIM, TQM), lambda h, b: (0, 0)),
        ],
        out_specs=head_rows(),
        out_shape=jax.ShapeDtypeStruct((batch, seq, N_HEADS * V_DIM), BF16),
        scratch_shapes=[
            pltpu.VMEM((seq + N_META, TQM), F32),
            pltpu.VMEM((seq + N_META, TQM), F32),
            pltpu.VMEM((seq + N_META, TQM), F32),
            pltpu.VMEM((hp, V_DIM + ONES_ROWS, seq), BF16),
            pltpu.VMEM((V_DIM + ONES_ROWS, TQM), F32),
            pltpu.VMEM((hp, V_DIM + ONES_ROWS, N_META), BF16),
        ],
        compiler_params=pltpu.CompilerParams(
            dimension_semantics=("arbitrary", "arbitrary"), vmem_limit_bytes=VMEM_LIMIT_BYTES),
        name="diff_attn",
    )(lam, jnp.zeros((1, 1), jnp.int32), qt, k, vt, k_meta, vt_meta, bias, g2d)


def _rms(x, g):
    ms = jnp.mean(x * x, axis=-1, keepdims=True)
    return (x * lax.rsqrt(ms + NORM_EPS)) * g


def _tail_kernel(x_ref, zp_ref, zprev_ref, znext_ref, zmeta_ref, ya_ref, pw_ref, ps_ref, wo_ref,
                 g2_ref, wg_ref, wu_ref, wd_ref, gf_ref, o_ref, e_scr, c_scr, *, tm, seq):
    tiles_per_seq = seq // tm
    t = lax.rem(pl.program_id(0), tiles_per_seq)
    p0 = t * tm
    prev = jnp.where(t == 0, zmeta_ref[...], zprev_ref[...])
    nxt = jnp.where(t == tiles_per_seq - 1, 0.0, znext_ref[...])
    zp = zp_ref[...]
    for g in range(len(POOL_WINDOWS)):
        lanes = slice(g * POOL_GROUP, (g + 1) * POOL_GROUP)
        e_scr[g, 0:HALO, :] = prev[:, lanes]
        e_scr[g, HALO:HALO + tm, :] = zp[:, lanes]
        e_scr[g, HALO + tm:2 * HALO + tm, :] = nxt[:, lanes]
        e_scr[g, 2 * HALO + tm:, :] = jnp.zeros((2 * HALO, POOL_GROUP), F32)

    rp = tm // TAIL_PARTS
    tail_row = lax.broadcasted_iota(jnp.int32, (HALO, POOL_GROUP), 0)

    def window_sum(g, w, r0):
        left = w // 2
        base = HALO + r0
        if w <= 4:
            win = e_scr[g, base - left:base - left + rp, :]
            for k in range(-left + 1, w - left):
                win = win + e_scr[g, base + k:base + k + rp, :]
            return win
        j0 = base - HALO
        src, width, n = e_scr.at[g], 1, rp + 3 * HALO
        level = 0
        while 2 * width < w:
            c_scr[level, j0:j0 + n, :] = src[j0:j0 + n, :] + src[j0 + width:j0 + width + n, :]
            src, width, n, level = c_scr.at[level], 2 * width, n - HALO, level + 1
        lo = base - left
        return src[lo:lo + rp, :] + src[lo + width:lo + width + rp, :]

    def pooled(r0):
        y_groups = []
        for g, w in enumerate(POOL_WINDOWS):
            right = w - 1 - w // 2
            win = window_sum(g, w, r0)
            center = zp_ref[r0:r0 + rp, g * POOL_GROUP:(g + 1) * POOL_GROUP]
            if r0 + rp < tm:
                d = win * (1.0 / w) - center
            else:
                body = win[0:rp - HALO] * (1.0 / w) - center[0:rp - HALO]
                pos = p0 + (tm - HALO) + tail_row
                over = jnp.maximum(pos + (right - (seq - 1)), 0)
                cnt = (w - over).astype(F32)
                d = jnp.concatenate([body, win[rp - HALO:] / cnt - center[rp - HALO:]], axis=0)
            yg = jnp.dot(d.astype(BF16), pw_ref[g], preferred_element_type=F32)
            y_groups.append((yg * ps_ref[:, g * POOL_GROUP:(g + 1) * POOL_GROUP]).astype(BF16))
        return jnp.concatenate(y_groups + [ya_ref[r0:r0 + rp, :]], axis=-1)

    def out_proj(r0, ycat):
        return x_ref[r0:r0 + rp, :] + jnp.dot(ycat, wo_ref[...], preferred_element_type=F32)

    def ffn(h1, f):
        acc = h1
        for c0 in range(0, D_FF, FF_CHUNK):
            cols = slice(c0, min(c0 + FF_CHUNK, D_FF))
            gate = jnp.dot(f, wg_ref[:, cols], preferred_element_type=F32)
            up = jnp.dot(f, wu_ref[:, cols], preferred_element_type=F32)
            act = (gate * jax.nn.sigmoid(gate) * up).astype(BF16)
            acc = acc + jnp.dot(act, wd_ref[cols, :], preferred_element_type=F32)
        return acc

    parts = [i * rp for i in range(TAIL_PARTS)]
    ycat = {parts[0]: pooled(parts[0])}
    h1, f, h2 = {}, {}, {}
    for i, r0 in enumerate(parts):
        h1[r0] = out_proj(r0, ycat[r0])
        if i + 1 < len(parts):
            ycat[parts[i + 1]] = pooled(parts[i + 1])
        f[r0] = _rms(h1[r0], g2_ref[...]).astype(BF16)
    for i, r0 in enumerate(parts):
        h2[r0] = ffn(h1[r0], f[r0])
        if i > 0:
            o_ref[parts[i - 1]:parts[i - 1] + rp, :] = _rms(h2[parts[i - 1]], gf_ref[...])
    o_ref[parts[-1]:parts[-1] + rp, :] = _rms(h2[parts[-1]], gf_ref[...])


def _tail(x2d, zp, zp_meta, y_attn, pool_w, pool_scale, w_o, g2, w_gate, w_up, w_down, g_final,
          tm, seq):
    rows = x2d.shape[0]
    halo_blocks = tm // HALO
    last_halo_block = rows // HALO - 1
    const = lambda shape: pl.BlockSpec(shape, lambda i: (0,) * len(shape),
                                       pipeline_mode=pl.Buffered(1))
    return pl.pallas_call(
        functools.partial(_tail_kernel, tm=tm, seq=seq),
        grid=(rows // tm,),
        in_specs=[
            pl.BlockSpec((tm, D_MODEL), lambda i: (i, 0)),
            pl.BlockSpec((tm, POOL_WIDTH), lambda i: (i, 0)),
            pl.BlockSpec((HALO, POOL_WIDTH), lambda i: (jnp.maximum(i * halo_blocks - 1, 0), 0)),
            pl.BlockSpec((HALO, POOL_WIDTH),
                         lambda i: (jnp.minimum((i + 1) * halo_blocks, last_halo_block), 0)),
            pl.BlockSpec((HALO, POOL_WIDTH), lambda i: (N_META // HALO - 1, 0)),
            pl.BlockSpec((tm, N_HEADS * V_DIM), lambda i: (i, 0)),
            const((len(POOL_WINDOWS), POOL_GROUP, POOL_GROUP)),
            const((1, POOL_WIDTH)),
            const((D_MODEL, D_MODEL)),
            const((1, D_MODEL)),
            const((D_MODEL, D_FF)),
            const((D_MODEL, D_FF)),
            const((D_FF, D_MODEL)),
            const((1, D_MODEL)),
        ],
        out_specs=pl.BlockSpec((tm, D_MODEL), lambda i: (i, 0)),
        out_shape=jax.ShapeDtypeStruct((rows, D_MODEL), F32),
        scratch_shapes=[pltpu.VMEM((len(POOL_WINDOWS), tm + 4 * HALO, POOL_GROUP), F32),
                        pltpu.VMEM((3, tm + 4 * HALO, POOL_GROUP), F32)],
        compiler_params=pltpu.CompilerParams(
            dimension_semantics=("arbitrary",), vmem_limit_bytes=VMEM_LIMIT_BYTES),
        name="pool_oproj_ffn",
    )(x2d, zp, zp, zp, zp_meta, y_attn, pool_w, pool_scale, w_o, g2, w_gate, w_up, w_down, g_final)


def kernel(x, meta_tokens, rel_bias, norm1_g, w_in, pool_w, pool_scale, lambda_q1, lambda_k1,
           lambda_q2, lambda_k2, subln_g, w_o, norm2_g, w_gate, w_up, w_down, final_g):
    batch, seq, _ = x.shape
    layer = 0
    lam = (jnp.exp(jnp.sum(lambda_q1[layer].astype(F32) * lambda_k1[layer].astype(F32)))
           - jnp.exp(jnp.sum(lambda_q2[layer].astype(F32) * lambda_k2[layer].astype(F32)))
           + LAMBDA_INIT).reshape(1, 1)

    col_scale = jnp.concatenate([
        jnp.ones((POOL_WIDTH,), F32), jnp.full((ATTN_WIDTH,), HEAD_DIM ** -0.5 * LOG2E, F32),
        jnp.ones((2 * ATTN_WIDTH,), F32)]).reshape(1, IN_WIDTH)
    g1 = norm1_g[layer].reshape(1, D_MODEL)

    x2d = x.reshape(batch * seq, D_MODEL)
    zp, qt, k, vt, w_o_b, w_gate_b, w_up_b, w_down_b = _inproj(
        x2d, g1, w_in[layer], col_scale, tm=1024,
        cast_weights=(w_o[layer], w_gate[layer], w_up[layer], w_down[layer]))
    meta_pad = jnp.pad(meta_tokens, ((0, META_PAD - N_META), (0, 0)))
    zp_meta, _, k_meta, vt_meta = _inproj(meta_pad, g1, w_in[layer], col_scale, tm=META_PAD)

    bias = _bias_tiles(rel_bias)
    g2d = jnp.broadcast_to(subln_g[layer].astype(F32)[:, None], (V_DIM, TQM))
    y_attn = _attention(
        lam, qt, k.reshape(batch, seq, ATTN_WIDTH),
        vt, k_meta, vt_meta, bias, g2d, batch, seq)

    out = _tail(
        x2d, zp, zp_meta, y_attn.reshape(batch * seq, N_HEADS * V_DIM),
        pool_w[layer].astype(BF16), pool_scale[layer].reshape(1, POOL_WIDTH),
        w_o_b, norm2_g[layer].reshape(1, D_MODEL), w_gate_b, w_up_b, w_down_b,
        final_g.reshape(1, D_MODEL), tm=1024, seq=seq)
    return out.reshape(batch, seq, D_MODEL)
```

```python
import functools
import math

import jax
import jax.numpy as jnp
from jax import lax
from jax.experimental import pallas as pl
from jax.experimental.pallas import tpu as pltpu

F32 = jnp.float32
BF16 = jnp.bfloat16

D_MODEL = 1024
N_META = 16
META_PAD = 128
POOL_WIDTH = 512
POOL_WINDOWS = (2, 4, 8, 16)
POOL_GROUP = 128
ATTN_WIDTH = 512
HEAD_DIM = 64
N_HEADS = 4
V_DIM = 128
IN_WIDTH = POOL_WIDTH + 3 * ATTN_WIDTH
REL_BUCKETS = 32
REL_MAX_DIST = 128
D_FF = 2816
NORM_EPS = 1e-6
SUBLN_EPS = 1e-5
LAMBDA_INIT = 0.8 - 0.6 * math.exp(-0.3 * 0)
HALO = 8

VMEM_LIMIT_BYTES = 56 * 1024 * 1024
BF16_SUBLANES = 16

TQ = 256
TQM = 512
TK = 256
N_BIAS_TILES = 5
BIAS_ROWS = N_BIAS_TILES * TK + 2 * N_META
FF_CHUNK = 256
ONES_ROWS = 16
LOG2E = math.log2(math.e)
CHUNK_SKEW = 2
TAIL_PARTS = 1
ATTN_HEADS_PER_STEP = 1


def _inproj_kernel(x_ref, meta_ref, g_ref, w_ref, cs_ref, *refs, n_cast):
    cast_in = refs[:n_cast]
    zp_ref, qt_ref, k_ref, vt_ref, zpm_ref, km_ref, vmt_ref = refs[n_cast:n_cast + 7]
    cast_out = refs[n_cast + 7:-1]
    wb_scr = refs[-1]
    o_q = POOL_WIDTH
    o_k = o_q + ATTN_WIDTH
    o_v = o_k + ATTN_WIDTH

    def project(x, zp_out, qt_out, k_out, vt_out):
        ms = jnp.mean(x * x, axis=-1, keepdims=True)
        u = ((x * lax.rsqrt(ms + NORM_EPS)) * g_ref[...]).astype(BF16)
        proj = lambda lo, hi: jnp.dot(u, wb_scr[:, lo:hi], preferred_element_type=F32)
        zp_out[...] = proj(0, o_q)
        if qt_out is not None:
            qt_out[...] = proj(o_q, o_k).T.astype(BF16)
        k_out[...] = proj(o_k, o_v).astype(BF16)
        vt_out[...] = proj(o_v, IN_WIDTH).T.astype(BF16)

    @pl.when(pl.program_id(0) == 0)
    def _():
        wb_scr[...] = (w_ref[...] * cs_ref[...]).astype(BF16)
        meta = jnp.concatenate(
            [meta_ref[...], jnp.zeros((META_PAD - N_META, D_MODEL), F32)], axis=0)
        project(meta, zpm_ref, None, km_ref, vmt_ref)

    for src, dst in zip(cast_in, cast_out):
        dst[...] = src[...].astype(BF16)
    project(x_ref[...], zp_ref, qt_ref, k_ref, vt_ref)


def _inproj(x2d, meta_tokens, g, w_in, col_scale, tm, cast_weights=()):
    rows = x2d.shape[0]
    steps = rows // tm
    row_spec = lambda width: pl.BlockSpec((tm, width), lambda i: (i, 0))
    col_spec = lambda width: pl.BlockSpec((width, tm), lambda i: (0, i))
    const = lambda shape: pl.BlockSpec(shape, lambda i: (0, 0), pipeline_mode=pl.Buffered(1))
    fixed = lambda shape: pl.BlockSpec(shape, lambda i: (0, 0))

    def cast_spec(w):
        n_rows, n_cols = w.shape
        block = min(d for d in range(BF16_SUBLANES, n_rows + 1, BF16_SUBLANES)
                    if n_rows % d == 0 and d * steps >= n_rows)
        last = n_rows // block - 1
        return pl.BlockSpec((block, n_cols), lambda i: (jnp.minimum(i, last), 0))

    cast_specs = [cast_spec(w) for w in cast_weights]
    return pl.pallas_call(
        functools.partial(_inproj_kernel, n_cast=len(cast_weights)),
        grid=(steps,),
        in_specs=[
            row_spec(D_MODEL),
            const((N_META, D_MODEL)),
            const((1, D_MODEL)),
            const((D_MODEL, IN_WIDTH)),
            const((1, IN_WIDTH)),
            *cast_specs,
        ],
        out_specs=[row_spec(POOL_WIDTH), col_spec(ATTN_WIDTH), row_spec(ATTN_WIDTH),
                   col_spec(ATTN_WIDTH),
                   fixed((META_PAD, POOL_WIDTH)), fixed((META_PAD, ATTN_WIDTH)),
                   fixed((ATTN_WIDTH, META_PAD)), *cast_specs],
        out_shape=[
            jax.ShapeDtypeStruct((rows, POOL_WIDTH), F32),
            jax.ShapeDtypeStruct((ATTN_WIDTH, rows), BF16),
            jax.ShapeDtypeStruct((rows, ATTN_WIDTH), BF16),
            jax.ShapeDtypeStruct((ATTN_WIDTH, rows), BF16),
            jax.ShapeDtypeStruct((META_PAD, POOL_WIDTH), F32),
            jax.ShapeDtypeStruct((META_PAD, ATTN_WIDTH), BF16),
            jax.ShapeDtypeStruct((ATTN_WIDTH, META_PAD), BF16),
            *[jax.ShapeDtypeStruct(w.shape, BF16) for w in cast_weights],
        ],
        scratch_shapes=[pltpu.VMEM((D_MODEL, IN_WIDTH), BF16)],
        compiler_params=pltpu.CompilerParams(
            dimension_semantics=("arbitrary",), vmem_limit_bytes=VMEM_LIMIT_BYTES),
        name="inproj",
    )(x2d, meta_tokens, g, w_in, col_scale, *cast_weights)


def _t5_bucket(rel):
    nb = REL_BUCKETS // 2
    ret = jnp.where(rel > 0, nb, 0)
    n = jnp.abs(rel)
    max_exact = nb // 2
    nf = jnp.maximum(n, 1).astype(F32)
    large = max_exact + (jnp.log(nf / max_exact) / math.log(REL_MAX_DIST / max_exact)
                         * (nb - max_exact)).astype(jnp.int32)
    large = jnp.minimum(large, nb - 1)
    return ret + jnp.where(n < max_exact, n, large)


def _bias_buckets():
    r = lax.broadcasted_iota(jnp.int32, (TK, TQ), 0)
    c = lax.broadcasted_iota(jnp.int32, (TK, TQ), 1)
    tiles = [(t - 2) * TK + r - c for t in range(N_BIAS_TILES)]
    rm = lax.broadcasted_iota(jnp.int32, (N_META, TQ), 0)
    cm = lax.broadcasted_iota(jnp.int32, (N_META, TQ), 1)
    metas = [rm - N_META - m * TQ - cm for m in range(2)]
    return _t5_bucket(jnp.concatenate(tiles + metas, axis=0))


def _bias_kernel(rb_ref, bkt_ref, o_ref):
    h = pl.program_id(0)

    def lookup(bkt):
        level = [rb_ref[b, h] * LOG2E for b in range(REL_BUCKETS)]
        for bit in range(REL_BUCKETS.bit_length() - 1):
            odd = (bkt & (1 << bit)) != 0
            level = [jnp.where(odd, level[2 * j + 1], level[2 * j]) for j in range(len(level) // 2)]
        return level[0]

    def fill(r0, rows, constant):
        if constant:
            slab = lookup(bkt_ref[r0:r0 + 8, :])
            o_ref[r0:r0 + rows, :] = jnp.broadcast_to(slab[None], (rows // 8, 8, TQ)).reshape(rows, TQ)
        else:
            o_ref[r0:r0 + rows, :] = lookup(bkt_ref[r0:r0 + rows, :])

    for t in range(N_BIAS_TILES):
        fill(t * TK, TK, constant=t in (0, N_BIAS_TILES - 1))
    for m in range(2):
        fill(N_BIAS_TILES * TK + m * N_META, N_META, constant=m == 1)


def _bias_tiles(rel_bias):
    return pl.pallas_call(
        _bias_kernel,
        grid=(N_HEADS,),
        in_specs=[
            pl.BlockSpec(memory_space=pltpu.SMEM),
            pl.BlockSpec((BIAS_ROWS, TQ), lambda h: (0, 0)),
        ],
        out_specs=pl.BlockSpec((None, BIAS_ROWS, TQ), lambda h: (h, 0, 0)),
        out_shape=jax.ShapeDtypeStruct((N_HEADS, BIAS_ROWS, TQ), F32),
        compiler_params=pltpu.CompilerParams(dimension_semantics=("arbitrary",)),
        name="bias_tiles",
    )(rel_bias, _bias_buckets())


_NT = (((1,), (1,)), ((), ()))


def _attn_kernel(lq1_ref, lk1_ref, lq2_ref, lk2_ref, zero_ref, qt_ref, k_ref, vt_ref, km_ref, vmt_ref, bias_ref, g_ref, o_ref,
                 s0_scr, s1_scr, s2_scr, vt_scr, o0_scr, vmt_scr, *, seq, hp):
    n_kc = seq // TK
    units_per_head = 2 * (seq // TQM)
    n_units = hp * units_per_head
    halves = TQM // TQ
    s_scr = (s0_scr, s1_scr, s2_scr)

    for hd in range(hp):
        vt_scr[hd, 0:V_DIM, :] = vt_ref[hd * V_DIM:(hd + 1) * V_DIM, :]
        vt_scr[hd, V_DIM:, :] = jnp.ones((ONES_ROWS, seq), BF16)
        vmt_scr[hd, 0:V_DIM, :] = vmt_ref[hd * V_DIM:(hd + 1) * V_DIM, 0:N_META]
        vmt_scr[hd, V_DIM:, :] = jnp.ones((ONES_ROWS, N_META), BF16)
    lam = (jnp.exp(jnp.sum(lq1_ref[...] * lk1_ref[...], axis=1, keepdims=True))
           - jnp.exp(jnp.sum(lq2_ref[...] * lk2_ref[...], axis=1, keepdims=True))
           + LAMBDA_INIT)
    z = zero_ref[0, 0]
    zero_half = jnp.zeros((HEAD_DIM, TQM), BF16)

    def masked_qt(u):
        hd, lu = divmod(u, units_per_head)
        qi, c = divmod(lu, 2)
        qt = qt_ref[hd * V_DIM:(hd + 1) * V_DIM, qi * TQM:(qi + 1) * TQM]
        if c == 0:
            return jnp.concatenate([qt[0:HEAD_DIM], zero_half], axis=0)
        return jnp.concatenate([zero_half, qt[HEAD_DIM:]], axis=0)

    def bias_tile(kc, qh):
        if kc < n_kc:
            t = min(max(kc - qh, -2), 2) + 2
            return t * TK, TK, t in (0, N_BIAS_TILES - 1)
        return N_BIAS_TILES * TK + min(qh, 1) * N_META, N_META, qh >= 1

    def score_chunk(u, kc, qt, m_acc):
        hd, lu = divmod(u, units_per_head)
        qi = lu // 2
        head_lanes = slice(hd * V_DIM, (hd + 1) * V_DIM)
        if kc < n_kc:
            rows = pl.ds(pl.multiple_of(z + kc * TK, TK), TK)
            s = jnp.dot(k_ref[kc * TK:(kc + 1) * TK, head_lanes], qt,
                        preferred_element_type=F32)
        else:
            rows = pl.ds(pl.multiple_of(z + seq, N_META), N_META)
            s = jnp.dot(km_ref[0:N_META, head_lanes], qt, preferred_element_type=F32)
        m_acc = [dict(a) for a in m_acc] if m_acc else [{} for _ in range(halves)]
        for j in range(halves):
            cols = slice(j * TQ, (j + 1) * TQ)
            b0, nrows, constant = bias_tile(kc, qi * halves + j)
            sc = s[:, cols] if constant else s[:, cols] + bias_ref[hd, b0:b0 + nrows, :]
            s_scr[u % len(s_scr)][rows, cols] = sc
            cm = jnp.max(sc.reshape(nrows // 8, 8, TQ), axis=0)
            key = b0 if constant else None
            m_acc[j][key] = jnp.maximum(m_acc[j][key], cm) if key in m_acc[j] else cm
        return m_acc

    def col_max(u, m_acc):
        hd = u // units_per_head
        out = []
        for acc in m_acc:
            m = None
            for key, a in acc.items():
                v = jnp.max(a, axis=0, keepdims=True)
                if key is not None:
                    v = v + bias_ref[hd, key:key + 1, :]
                m = v if m is None else jnp.maximum(m, v)
            out.append(m)
        return out

    def exp_offset(u, kc, m, cache):
        hd, lu = divmod(u, units_per_head)
        tiles = tuple(bias_tile(kc, (lu // 2) * halves + j) for j in range(halves))
        key = tuple(b0 if constant else None for b0, _, constant in tiles)
        if key not in cache:
            cache[key] = jnp.concatenate(
                [m[j] if b0 is None else m[j] - bias_ref[hd, b0:b0 + 1, :]
                 for j, b0 in enumerate(key)], axis=1)
        return cache[key]

    def value_chunk(u, kc, off, oe):
        hd = u // units_per_head
        if kc < n_kc:
            rows = pl.ds(pl.multiple_of(z + kc * TK, TK), TK)
            lhs = vt_scr[hd, :, kc * TK:(kc + 1) * TK]
        else:
            rows = pl.ds(pl.multiple_of(z + seq, N_META), N_META)
            lhs = vmt_scr[hd]
        p = jnp.exp2(s_scr[u % len(s_scr)][rows, :] - off).astype(BF16)
        d = jnp.dot(lhs, p, preferred_element_type=F32)
        return d if oe is None else oe + d

    def finalize(u, oe0, oe1):
        hd, lu = divmod(u, units_per_head)
        qi = lu // 2
        o = (oe0[0:V_DIM] * (1.0 / oe0[V_DIM:V_DIM + 1])
             - oe1[0:V_DIM] * (lam / oe1[V_DIM:V_DIM + 1]))
        ms = jnp.mean(o * o, axis=0, keepdims=True)
        y = ((o * lax.rsqrt(ms + SUBLN_EPS)) * g_ref[...]) * (1.0 - LAMBDA_INIT)
        o_ref[qi * TQM:(qi + 1) * TQM, hd * V_DIM:(hd + 1) * V_DIM] = y.T.astype(BF16)

    n_chunks = n_kc + 1
    total = n_units * n_chunks
    lag = n_chunks + CHUNK_SKEW
    score_order = [n_kc] + list(range(n_kc))
    meta_pos = -lag % n_chunks
    value_order = list(range(n_kc))
    value_order.insert(meta_pos, n_kc)
    col_max_of, offsets_of = {}, {}
    qt = m_acc = oe = None
    for g in range(total + lag):
        if g < total:
            u, i = divmod(g, n_chunks)
            if i == 0:
                qt, m_acc = masked_qt(u), None
            m_acc = score_chunk(u, score_order[i], qt, m_acc)
            if i == n_chunks - 1:
                col_max_of[u], offsets_of[u] = col_max(u, m_acc), {}
        if g >= lag:
            u, i = divmod(g - lag, n_chunks)
            off = exp_offset(u, value_order[i], col_max_of[u], offsets_of[u])
            oe = value_chunk(u, value_order[i], off, None if i == 0 else oe)
            if i == n_chunks - 1:
                if u % 2 == 0:
                    o0_scr[...] = oe
                else:
                    finalize(u, o0_scr[...], oe)


def _attention(lambdas, qt, k, vt, k_meta, vt_meta, bias, g2d, batch, seq):
    hp = ATTN_HEADS_PER_STEP
    width = hp * V_DIM
    head_rows = lambda: pl.BlockSpec((None, seq, width), lambda h, b: (b, 0, h))
    return pl.pallas_call(
        functools.partial(_attn_kernel, seq=seq, hp=hp),
        grid=(N_HEADS // hp, batch),
        in_specs=[
            *[pl.BlockSpec((1, HEAD_DIM), lambda h, b: (0, 0)) for _ in lambdas],
            pl.BlockSpec(memory_space=pltpu.SMEM),
            pl.BlockSpec((width, seq), lambda h, b: (h, b)),
            head_rows(),
            pl.BlockSpec((width, seq), lambda h, b: (h, b)),
            pl.BlockSpec((META_PAD, width), lambda h, b: (0, h)),
            pl.BlockSpec((width, META_PAD), lambda h, b: (h, 0)),
            pl.BlockSpec((hp, BIAS_ROWS, TQ), lambda h, b: (h, 0, 0)),
            pl.BlockSpec((V_DIM, TQM), lambda h, b: (0, 0)),
        ],
        out_specs=head_rows(),
        out_shape=jax.ShapeDtypeStruct((batch, seq, N_HEADS * V_DIM), BF16),
        scratch_shapes=[
            pltpu.VMEM((seq + N_META, TQM), F32),
            pltpu.VMEM((seq + N_META, TQM), F32),
            pltpu.VMEM((seq + N_META, TQM), F32),
            pltpu.VMEM((hp, V_DIM + ONES_ROWS, seq), BF16),
            pltpu.VMEM((V_DIM + ONES_ROWS, TQM), F32),
            pltpu.VMEM((hp, V_DIM + ONES_ROWS, N_META), BF16),
        ],
        compiler_params=pltpu.CompilerParams(
            dimension_semantics=("arbitrary", "arbitrary"), vmem_limit_bytes=VMEM_LIMIT_BYTES),
        name="diff_attn",
    )(*lambdas, jnp.zeros((1, 1), jnp.int32), qt, k, vt, k_meta, vt_meta, bias, g2d)


def _rms(x, g):
    ms = jnp.mean(x * x, axis=-1, keepdims=True)
    return (x * lax.rsqrt(ms + NORM_EPS)) * g


def _tail_kernel(x_ref, zp_ref, zprev_ref, znext_ref, zmeta_ref, ya_ref, pw_ref, ps_ref, wo_ref,
                 g2_ref, wg_ref, wu_ref, wd_ref, gf_ref, o_ref, e_scr, c_scr, *, tm, seq):
    tiles_per_seq = seq // tm
    t = lax.rem(pl.program_id(0), tiles_per_seq)
    p0 = t * tm
    prev = jnp.where(t == 0, zmeta_ref[...], zprev_ref[...])
    nxt = jnp.where(t == tiles_per_seq - 1, 0.0, znext_ref[...])
    zp = zp_ref[...]
    for g in range(len(POOL_WINDOWS)):
        lanes = slice(g * POOL_GROUP, (g + 1) * POOL_GROUP)
        e_scr[g, 0:HALO, :] = prev[:, lanes]
        e_scr[g, HALO:HALO + tm, :] = zp[:, lanes]
        e_scr[g, HALO + tm:2 * HALO + tm, :] = nxt[:, lanes]
        e_scr[g, 2 * HALO + tm:, :] = jnp.zeros((2 * HALO, POOL_GROUP), F32)

    rp = tm // TAIL_PARTS
    tail_row = lax.broadcasted_iota(jnp.int32, (HALO, POOL_GROUP), 0)

    def window_sum(g, w, r0):
        left = w // 2
        base = HALO + r0
        if w <= 4:
            win = e_scr[g, base - left:base - left + rp, :]
            for k in range(-left + 1, w - left):
                win = win + e_scr[g, base + k:base + k + rp, :]
            return win
        j0 = base - HALO
        src, width, n = e_scr.at[g], 1, rp + 3 * HALO
        level = 0
        while 2 * width < w:
            c_scr[level, j0:j0 + n, :] = src[j0:j0 + n, :] + src[j0 + width:j0 + width + n, :]
            src, width, n, level = c_scr.at[level], 2 * width, n - HALO, level + 1
        lo = base - left
        return src[lo:lo + rp, :] + src[lo + width:lo + width + rp, :]

    def pooled(r0):
        y_groups = []
        for g, w in enumerate(POOL_WINDOWS):
            right = w - 1 - w // 2
            win = window_sum(g, w, r0)
            center = zp_ref[r0:r0 + rp, g * POOL_GROUP:(g + 1) * POOL_GROUP]
            if r0 + rp < tm:
                d = win * (1.0 / w) - center
            else:
                body = win[0:rp - HALO] * (1.0 / w) - center[0:rp - HALO]
                pos = p0 + (tm - HALO) + tail_row
                over = jnp.maximum(pos + (right - (seq - 1)), 0)
                cnt = (w - over).astype(F32)
                d = jnp.concatenate([body, win[rp - HALO:] / cnt - center[rp - HALO:]], axis=0)
            yg = jnp.dot(d.astype(BF16), pw_ref[g], preferred_element_type=F32)
            y_groups.append((yg * ps_ref[:, g * POOL_GROUP:(g + 1) * POOL_GROUP]).astype(BF16))
        return jnp.concatenate(y_groups + [ya_ref[r0:r0 + rp, :]], axis=-1)

    def out_proj(r0, ycat):
        return x_ref[r0:r0 + rp, :] + jnp.dot(ycat, wo_ref[...], preferred_element_type=F32)

    def ffn(h1, f):
        acc = h1
        for c0 in range(0, D_FF, FF_CHUNK):
            cols = slice(c0, min(c0 + FF_CHUNK, D_FF))
            gate = jnp.dot(f, wg_ref[:, cols], preferred_element_type=F32)
            up = jnp.dot(f, wu_ref[:, cols], preferred_element_type=F32)
            act = (gate * jax.nn.sigmoid(gate) * up).astype(BF16)
            acc = acc + jnp.dot(act, wd_ref[cols, :], preferred_element_type=F32)
        return acc

    parts = [i * rp for i in range(TAIL_PARTS)]
    ycat = {parts[0]: pooled(parts[0])}
    h1, f, h2 = {}, {}, {}
    for i, r0 in enumerate(parts):
        h1[r0] = out_proj(r0, ycat[r0])
        if i + 1 < len(parts):
            ycat[parts[i + 1]] = pooled(parts[i + 1])
        f[r0] = _rms(h1[r0], g2_ref[...]).astype(BF16)
    for i, r0 in enumerate(parts):
        h2[r0] = ffn(h1[r0], f[r0])
        if i > 0:
            o_ref[parts[i - 1]:parts[i - 1] + rp, :] = _rms(h2[parts[i - 1]], gf_ref[...])
    o_ref[parts[-1]:parts[-1] + rp, :] = _rms(h2[parts[-1]], gf_ref[...])


def _tail(x2d, zp, zp_meta, y_attn, pool_w, pool_scale, w_o, g2, w_gate, w_up, w_down, g_final,
          tm, seq):
    rows = x2d.shape[0]
    halo_blocks = tm // HALO
    last_halo_block = rows // HALO - 1
    const = lambda shape: pl.BlockSpec(shape, lambda i: (0,) * len(shape),
                                       pipeline_mode=pl.Buffered(1))
    return pl.pallas_call(
        functools.partial(_tail_kernel, tm=tm, seq=seq),
        grid=(rows // tm,),
        in_specs=[
            pl.BlockSpec((tm, D_MODEL), lambda i: (i, 0)),
            pl.BlockSpec((tm, POOL_WIDTH), lambda i: (i, 0)),
            pl.BlockSpec((HALO, POOL_WIDTH), lambda i: (jnp.maximum(i * halo_blocks - 1, 0), 0)),
            pl.BlockSpec((HALO, POOL_WIDTH),
                         lambda i: (jnp.minimum((i + 1) * halo_blocks, last_halo_block), 0)),
            pl.BlockSpec((HALO, POOL_WIDTH), lambda i: (N_META // HALO - 1, 0)),
            pl.BlockSpec((tm, N_HEADS * V_DIM), lambda i: (i, 0)),
            const((len(POOL_WINDOWS), POOL_GROUP, POOL_GROUP)),
            const((1, POOL_WIDTH)),
            const((D_MODEL, D_MODEL)),
            const((1, D_MODEL)),
            const((D_MODEL, D_FF)),
            const((D_MODEL, D_FF)),
            const((D_FF, D_MODEL)),
            const((1, D_MODEL)),
        ],
        out_specs=pl.BlockSpec((tm, D_MODEL), lambda i: (i, 0)),
        out_shape=jax.ShapeDtypeStruct((rows, D_MODEL), F32),
        scratch_shapes=[pltpu.VMEM((len(POOL_WINDOWS), tm + 4 * HALO, POOL_GROUP), F32),
                        pltpu.VMEM((3, tm + 4 * HALO, POOL_GROUP), F32)],
        compiler_params=pltpu.CompilerParams(
            dimension_semantics=("arbitrary",), vmem_limit_bytes=VMEM_LIMIT_BYTES),
        name="pool_oproj_ffn",
    )(x2d, zp, zp, zp, zp_meta, y_attn, pool_w, pool_scale, w_o, g2, w_gate, w_up, w_down, g_final)


def kernel(x, meta_tokens, rel_bias, norm1_g, w_in, pool_w, pool_scale, lambda_q1, lambda_k1,
           lambda_q2, lambda_k2, subln_g, w_o, norm2_g, w_gate, w_up, w_down, final_g):
    batch, seq, _ = x.shape
    layer = 0
    lambdas = [v[layer].astype(F32).reshape(1, HEAD_DIM)
               for v in (lambda_q1, lambda_k1, lambda_q2, lambda_k2)]

    col_scale = jnp.concatenate([
        jnp.ones((POOL_WIDTH,), F32), jnp.full((ATTN_WIDTH,), HEAD_DIM ** -0.5 * LOG2E, F32),
        jnp.ones((2 * ATTN_WIDTH,), F32)]).reshape(1, IN_WIDTH)
    g1 = norm1_g[layer].reshape(1, D_MODEL)

    x2d = x.reshape(batch * seq, D_MODEL)
    n_groups = len(POOL_WINDOWS)
    (zp, qt, k, vt, zp_meta, k_meta, vt_meta,
     w_o_b, w_gate_b, w_up_b, w_down_b, pool_w_b) = _inproj(
        x2d, meta_tokens, g1, w_in[layer], col_scale, tm=1024,
        cast_weights=(w_o[layer], w_gate[layer], w_up[layer], w_down[layer],
                      pool_w[layer].reshape(n_groups * POOL_GROUP, POOL_GROUP)))

    bias = _bias_tiles(rel_bias)
    g2d = jnp.broadcast_to(subln_g[layer].astype(F32)[:, None], (V_DIM, TQM))
    y_attn = _attention(
        lambdas, qt, k.reshape(batch, seq, ATTN_WIDTH),
        vt, k_meta, vt_meta, bias, g2d, batch, seq)

    out = _tail(
        x2d, zp, zp_meta, y_attn.reshape(batch * seq, N_HEADS * V_DIM),
        pool_w_b.reshape(n_groups, POOL_GROUP, POOL_GROUP), pool_scale[layer].reshape(1, POOL_WIDTH),
        w_o_b, norm2_g[layer].reshape(1, D_MODEL), w_gate_b, w_up_b, w_down_b,
        final_g.reshape(1, D_MODEL), tm=1024, seq=seq)
    return out.reshape(batch, seq, D_MODEL)
```

```python
import functools
import math

import jax
import jax.numpy as jnp
from jax import lax
from jax.experimental import pallas as pl
from jax.experimental.pallas import tpu as pltpu

F32 = jnp.float32
BF16 = jnp.bfloat16

D_MODEL = 1024
N_META = 16
META_PAD = 128
POOL_WIDTH = 512
POOL_WINDOWS = (2, 4, 8, 16)
POOL_GROUP = 128
ATTN_WIDTH = 512
HEAD_DIM = 64
N_HEADS = 4
V_DIM = 128
IN_WIDTH = POOL_WIDTH + 3 * ATTN_WIDTH
REL_BUCKETS = 32
REL_MAX_DIST = 128
D_FF = 2816
NORM_EPS = 1e-6
SUBLN_EPS = 1e-5
LAMBDA_INIT = 0.8 - 0.6 * math.exp(-0.3 * 0)
HALO = 8

VMEM_LIMIT_BYTES = 56 * 1024 * 1024
BF16_SUBLANES = 16

TQ = 256
TQM = 512
TK = 256
N_BIAS_TILES = 5
BIAS_ROWS = N_BIAS_TILES * TK + 2 * N_META
FF_CHUNK = 256
ONES_ROWS = 16
LOG2E = math.log2(math.e)
CHUNK_SKEW = 2
TAIL_PARTS = 2
INPROJ_PARTS = 2
ATTN_HEADS_PER_STEP = 1


def _inproj_kernel(x_ref, meta_ref, g_ref, w_ref, cs_ref, *refs, n_cast):
    cast_in = refs[:n_cast]
    zp_ref, qt_ref, k_ref, vt_ref, zpm_ref, km_ref, vmt_ref = refs[n_cast:n_cast + 7]
    cast_out = refs[n_cast + 7:-1]
    wb_scr = refs[-1]
    o_q = POOL_WIDTH
    o_k = o_q + ATTN_WIDTH
    o_v = o_k + ATTN_WIDTH

    def project(x, zp_out, qt_out, k_out, vt_out):
        ms = jnp.mean(x * x, axis=-1, keepdims=True)
        u = ((x * lax.rsqrt(ms + NORM_EPS)) * g_ref[...]).astype(BF16)
        proj = lambda lo, hi: jnp.dot(u, wb_scr[:, lo:hi], preferred_element_type=F32)
        zp_out[...] = proj(0, o_q)
        if qt_out is not None:
            qt_out[...] = proj(o_q, o_k).T.astype(BF16)
        k_out[...] = proj(o_k, o_v).astype(BF16)
        vt_out[...] = proj(o_v, IN_WIDTH).T.astype(BF16)

    @pl.when(pl.program_id(0) == 0)
    def _():
        wb_scr[...] = (w_ref[...] * cs_ref[...]).astype(BF16)
        meta = jnp.concatenate(
            [meta_ref[...], jnp.zeros((META_PAD - N_META, D_MODEL), F32)], axis=0)
        project(meta, zpm_ref, None, km_ref, vmt_ref)

    for src, dst in zip(cast_in, cast_out):
        dst[...] = src[...].astype(BF16)
    rp = x_ref.shape[0] // INPROJ_PARTS
    for i in range(INPROJ_PARTS):
        rows = slice(i * rp, (i + 1) * rp)
        project(x_ref[rows, :], zp_ref.at[rows, :], qt_ref.at[:, rows], k_ref.at[rows, :],
                vt_ref.at[:, rows])


def _inproj(x2d, meta_tokens, g, w_in, col_scale, tm, cast_weights=()):
    rows = x2d.shape[0]
    steps = rows // tm
    row_spec = lambda width: pl.BlockSpec((tm, width), lambda i: (i, 0))
    col_spec = lambda width: pl.BlockSpec((width, tm), lambda i: (0, i))
    const = lambda shape: pl.BlockSpec(shape, lambda i: (0, 0), pipeline_mode=pl.Buffered(1))
    fixed = lambda shape: pl.BlockSpec(shape, lambda i: (0, 0))

    def cast_spec(w):
        n_rows, n_cols = w.shape
        block = min(d for d in range(BF16_SUBLANES, n_rows + 1, BF16_SUBLANES)
                    if n_rows % d == 0 and d * steps >= n_rows)
        last = n_rows // block - 1
        return pl.BlockSpec((block, n_cols), lambda i: (jnp.minimum(i, last), 0))

    cast_specs = [cast_spec(w) for w in cast_weights]
    return pl.pallas_call(
        functools.partial(_inproj_kernel, n_cast=len(cast_weights)),
        grid=(steps,),
        in_specs=[
            row_spec(D_MODEL),
            const((N_META, D_MODEL)),
            const((1, D_MODEL)),
            const((D_MODEL, IN_WIDTH)),
            const((1, IN_WIDTH)),
            *cast_specs,
        ],
        out_specs=[row_spec(POOL_WIDTH), col_spec(ATTN_WIDTH), row_spec(ATTN_WIDTH),
                   col_spec(ATTN_WIDTH),
                   fixed((META_PAD, POOL_WIDTH)), fixed((META_PAD, ATTN_WIDTH)),
                   fixed((ATTN_WIDTH, META_PAD)), *cast_specs],
        out_shape=[
            jax.ShapeDtypeStruct((rows, POOL_WIDTH), F32),
            jax.ShapeDtypeStruct((ATTN_WIDTH, rows), BF16),
            jax.ShapeDtypeStruct((rows, ATTN_WIDTH), BF16),
            jax.ShapeDtypeStruct((ATTN_WIDTH, rows), BF16),
            jax.ShapeDtypeStruct((META_PAD, POOL_WIDTH), F32),
            jax.ShapeDtypeStruct((META_PAD, ATTN_WIDTH), BF16),
            jax.ShapeDtypeStruct((ATTN_WIDTH, META_PAD), BF16),
            *[jax.ShapeDtypeStruct(w.shape, BF16) for w in cast_weights],
        ],
        scratch_shapes=[pltpu.VMEM((D_MODEL, IN_WIDTH), BF16)],
        compiler_params=pltpu.CompilerParams(
            dimension_semantics=("arbitrary",), vmem_limit_bytes=VMEM_LIMIT_BYTES),
        name="inproj",
    )(x2d, meta_tokens, g, w_in, col_scale, *cast_weights)


def _t5_bucket(rel):
    nb = REL_BUCKETS // 2
    ret = jnp.where(rel > 0, nb, 0)
    n = jnp.abs(rel)
    max_exact = nb // 2
    nf = jnp.maximum(n, 1).astype(F32)
    large = max_exact + (jnp.log(nf / max_exact) / math.log(REL_MAX_DIST / max_exact)
                         * (nb - max_exact)).astype(jnp.int32)
    large = jnp.minimum(large, nb - 1)
    return ret + jnp.where(n < max_exact, n, large)


def _bias_buckets():
    r = lax.broadcasted_iota(jnp.int32, (TK, TQ), 0)
    c = lax.broadcasted_iota(jnp.int32, (TK, TQ), 1)
    tiles = [(t - 2) * TK + r - c for t in range(N_BIAS_TILES)]
    rm = lax.broadcasted_iota(jnp.int32, (N_META, TQ), 0)
    cm = lax.broadcasted_iota(jnp.int32, (N_META, TQ), 1)
    metas = [rm - N_META - m * TQ - cm for m in range(2)]
    return _t5_bucket(jnp.concatenate(tiles + metas, axis=0))


def _bias_kernel(rb_ref, bkt_ref, o_ref):
    h = pl.program_id(0)

    def lookup(bkt):
        level = [rb_ref[b, h] * LOG2E for b in range(REL_BUCKETS)]
        for bit in range(REL_BUCKETS.bit_length() - 1):
            odd = (bkt & (1 << bit)) != 0
            level = [jnp.where(odd, level[2 * j + 1], level[2 * j]) for j in range(len(level) // 2)]
        return level[0]

    def fill(r0, rows, constant):
        if constant:
            slab = lookup(bkt_ref[r0:r0 + 8, :])
            o_ref[r0:r0 + rows, :] = jnp.broadcast_to(slab[None], (rows // 8, 8, TQ)).reshape(rows, TQ)
        else:
            o_ref[r0:r0 + rows, :] = lookup(bkt_ref[r0:r0 + rows, :])

    for t in range(N_BIAS_TILES):
        fill(t * TK, TK, constant=t in (0, N_BIAS_TILES - 1))
    for m in range(2):
        fill(N_BIAS_TILES * TK + m * N_META, N_META, constant=m == 1)


def _bias_tiles(rel_bias):
    return pl.pallas_call(
        _bias_kernel,
        grid=(N_HEADS,),
        in_specs=[
            pl.BlockSpec(memory_space=pltpu.SMEM),
            pl.BlockSpec((BIAS_ROWS, TQ), lambda h: (0, 0)),
        ],
        out_specs=pl.BlockSpec((None, BIAS_ROWS, TQ), lambda h: (h, 0, 0)),
        out_shape=jax.ShapeDtypeStruct((N_HEADS, BIAS_ROWS, TQ), F32),
        compiler_params=pltpu.CompilerParams(dimension_semantics=("arbitrary",)),
        name="bias_tiles",
    )(rel_bias, _bias_buckets())


_NT = (((1,), (1,)), ((), ()))


def _attn_kernel(lq1_ref, lk1_ref, lq2_ref, lk2_ref, zero_ref, qt_ref, k_ref, vt_ref, km_ref, vmt_ref, bias_ref, g_ref, o_ref,
                 s0_scr, s1_scr, s2_scr, vt_scr, o0_scr, vmt_scr, *, seq, hp):
    n_kc = seq // TK
    units_per_head = 2 * (seq // TQM)
    n_units = hp * units_per_head
    halves = TQM // TQ
    s_scr = (s0_scr, s1_scr, s2_scr)

    for hd in range(hp):
        vt_scr[hd, 0:V_DIM, :] = vt_ref[hd * V_DIM:(hd + 1) * V_DIM, :]
        vt_scr[hd, V_DIM:, :] = jnp.ones((ONES_ROWS, seq), BF16)
        vmt_scr[hd, 0:V_DIM, :] = vmt_ref[hd * V_DIM:(hd + 1) * V_DIM, 0:N_META]
        vmt_scr[hd, V_DIM:, :] = jnp.ones((ONES_ROWS, N_META), BF16)
    lam = (jnp.exp(jnp.sum(lq1_ref[...] * lk1_ref[...], axis=1, keepdims=True))
           - jnp.exp(jnp.sum(lq2_ref[...] * lk2_ref[...], axis=1, keepdims=True))
           + LAMBDA_INIT)
    z = zero_ref[0, 0]
    zero_half = jnp.zeros((HEAD_DIM, TQM), BF16)

    def masked_qt(u):
        hd, lu = divmod(u, units_per_head)
        qi, c = divmod(lu, 2)
        qt = qt_ref[hd * V_DIM:(hd + 1) * V_DIM, qi * TQM:(qi + 1) * TQM]
        if c == 0:
            return jnp.concatenate([qt[0:HEAD_DIM], zero_half], axis=0)
        return jnp.concatenate([zero_half, qt[HEAD_DIM:]], axis=0)

    def bias_tile(kc, qh):
        if kc < n_kc:
            t = min(max(kc - qh, -2), 2) + 2
            return t * TK, TK, t in (0, N_BIAS_TILES - 1)
        return N_BIAS_TILES * TK + min(qh, 1) * N_META, N_META, qh >= 1

    def score_chunk(u, kc, qt, m_acc):
        hd, lu = divmod(u, units_per_head)
        qi = lu // 2
        head_lanes = slice(hd * V_DIM, (hd + 1) * V_DIM)
        if kc < n_kc:
            rows = pl.ds(pl.multiple_of(z + kc * TK, TK), TK)
            s = jnp.dot(k_ref[kc * TK:(kc + 1) * TK, head_lanes], qt,
                        preferred_element_type=F32)
        else:
            rows = pl.ds(pl.multiple_of(z + seq, N_META), N_META)
            s = jnp.dot(km_ref[0:N_META, head_lanes], qt, preferred_element_type=F32)
        m_acc = [dict(a) for a in m_acc] if m_acc else [{} for _ in range(halves)]
        for j in range(halves):
            cols = slice(j * TQ, (j + 1) * TQ)
            b0, nrows, constant = bias_tile(kc, qi * halves + j)
            sc = s[:, cols] if constant else s[:, cols] + bias_ref[hd, b0:b0 + nrows, :]
            s_scr[u % len(s_scr)][rows, cols] = sc
            cm = jnp.max(sc.reshape(nrows // 8, 8, TQ), axis=0)
            key = b0 if constant else None
            m_acc[j][key] = jnp.maximum(m_acc[j][key], cm) if key in m_acc[j] else cm
        return m_acc

    def col_max(u, m_acc):
        hd = u // units_per_head
        out = []
        for acc in m_acc:
            m = None
            for key, a in acc.items():
                v = jnp.max(a, axis=0, keepdims=True)
                if key is not None:
                    v = v + bias_ref[hd, key:key + 1, :]
                m = v if m is None else jnp.maximum(m, v)
            out.append(m)
        return out

    def exp_offset(u, kc, m, cache):
        hd, lu = divmod(u, units_per_head)
        tiles = tuple(bias_tile(kc, (lu // 2) * halves + j) for j in range(halves))
        key = tuple(b0 if constant else None for b0, _, constant in tiles)
        if key not in cache:
            cache[key] = jnp.concatenate(
                [m[j] if b0 is None else m[j] - bias_ref[hd, b0:b0 + 1, :]
                 for j, b0 in enumerate(key)], axis=1)
        return cache[key]

    def value_chunk(u, kc, off, oe):
        hd = u // units_per_head
        if kc < n_kc:
            rows = pl.ds(pl.multiple_of(z + kc * TK, TK), TK)
            lhs = vt_scr[hd, :, kc * TK:(kc + 1) * TK]
        else:
            rows = pl.ds(pl.multiple_of(z + seq, N_META), N_META)
            lhs = vmt_scr[hd]
        p = jnp.exp2(s_scr[u % len(s_scr)][rows, :] - off).astype(BF16)
        d = jnp.dot(lhs, p, preferred_element_type=F32)
        return d if oe is None else oe + d

    def finalize(u, oe0, oe1):
        hd, lu = divmod(u, units_per_head)
        qi = lu // 2
        o = (oe0[0:V_DIM] * (1.0 / oe0[V_DIM:V_DIM + 1])
             - oe1[0:V_DIM] * (lam / oe1[V_DIM:V_DIM + 1]))
        ms = jnp.mean(o * o, axis=0, keepdims=True)
        y = ((o * lax.rsqrt(ms + SUBLN_EPS)) * g_ref[...]) * (1.0 - LAMBDA_INIT)
        o_ref[qi * TQM:(qi + 1) * TQM, hd * V_DIM:(hd + 1) * V_DIM] = y.T.astype(BF16)

    n_chunks = n_kc + 1
    total = n_units * n_chunks
    lag = n_chunks + CHUNK_SKEW
    score_order = [n_kc] + list(range(n_kc))
    meta_pos = -lag % n_chunks
    value_order = list(range(n_kc))
    value_order.insert(meta_pos, n_kc)
    col_max_of, offsets_of = {}, {}
    qt = m_acc = oe = None
    for g in range(total + lag):
        if g < total:
            u, i = divmod(g, n_chunks)
            if i == 0:
                qt, m_acc = masked_qt(u), None
            m_acc = score_chunk(u, score_order[i], qt, m_acc)
            if i == n_chunks - 1:
                col_max_of[u], offsets_of[u] = col_max(u, m_acc), {}
        if g >= lag:
            u, i = divmod(g - lag, n_chunks)
            off = exp_offset(u, value_order[i], col_max_of[u], offsets_of[u])
            oe = value_chunk(u, value_order[i], off, None if i == 0 else oe)
            if i == n_chunks - 1:
                if u % 2 == 0:
                    o0_scr[...] = oe
                else:
                    finalize(u, o0_scr[...], oe)


def _attention(lambdas, qt, k, vt, k_meta, vt_meta, bias, g2d, batch, seq):
    hp = ATTN_HEADS_PER_STEP
    width = hp * V_DIM
    head_rows = lambda: pl.BlockSpec((None, seq, width), lambda h, b: (b, 0, h))
    return pl.pallas_call(
        functools.partial(_attn_kernel, seq=seq, hp=hp),
        grid=(N_HEADS // hp, batch),
        in_specs=[
            *[pl.BlockSpec((1, HEAD_DIM), lambda h, b: (0, 0)) for _ in lambdas],
            pl.BlockSpec(memory_space=pltpu.SMEM),
            pl.BlockSpec((width, seq), lambda h, b: (h, b)),
            head_rows(),
            pl.BlockSpec((width, seq), lambda h, b: (h, b)),
            pl.BlockSpec((META_PAD, width), lambda h, b: (0, h)),
            pl.BlockSpec((width, META_PAD), lambda h, b: (h, 0)),
            pl.BlockSpec((hp, BIAS_ROWS, TQ), lambda h, b: (h, 0, 0)),
            pl.BlockSpec((V_DIM, TQM), lambda h, b: (0, 0)),
        ],
        out_specs=head_rows(),
        out_shape=jax.ShapeDtypeStruct((batch, seq, N_HEADS * V_DIM), BF16),
        scratch_shapes=[
            pltpu.VMEM((seq + N_META, TQM), F32),
            pltpu.VMEM((seq + N_META, TQM), F32),
            pltpu.VMEM((seq + N_META, TQM), F32),
            pltpu.VMEM((hp, V_DIM + ONES_ROWS, seq), BF16),
            pltpu.VMEM((V_DIM + ONES_ROWS, TQM), F32),
            pltpu.VMEM((hp, V_DIM + ONES_ROWS, N_META), BF16),
        ],
        compiler_params=pltpu.CompilerParams(
            dimension_semantics=("arbitrary", "arbitrary"), vmem_limit_bytes=VMEM_LIMIT_BYTES),
        name="diff_attn",
    )(*lambdas, jnp.zeros((1, 1), jnp.int32), qt, k, vt, k_meta, vt_meta, bias, g2d)


def _rms(x, g):
    ms = jnp.mean(x * x, axis=-1, keepdims=True)
    return (x * lax.rsqrt(ms + NORM_EPS)) * g


def _tail_kernel(x_ref, zp_ref, zprev_ref, znext_ref, zmeta_ref, ya_ref, pw_ref, ps_ref, wo_ref,
                 g2_ref, wg_ref, wu_ref, wd_ref, gf_ref, o_ref, e_scr, c_scr, *, tm, seq):
    tiles_per_seq = seq // tm
    t = lax.rem(pl.program_id(0), tiles_per_seq)
    p0 = t * tm
    prev = jnp.where(t == 0, zmeta_ref[...], zprev_ref[...])
    nxt = jnp.where(t == tiles_per_seq - 1, 0.0, znext_ref[...])
    zp = zp_ref[...]
    for g in range(len(POOL_WINDOWS)):
        lanes = slice(g * POOL_GROUP, (g + 1) * POOL_GROUP)
        e_scr[g, 0:HALO, :] = prev[:, lanes]
        e_scr[g, HALO:HALO + tm, :] = zp[:, lanes]
        e_scr[g, HALO + tm:2 * HALO + tm, :] = nxt[:, lanes]
        e_scr[g, 2 * HALO + tm:, :] = jnp.zeros((2 * HALO, POOL_GROUP), F32)

    rp = tm // TAIL_PARTS
    tail_row = lax.broadcasted_iota(jnp.int32, (HALO, POOL_GROUP), 0)

    def window_sum(g, w, r0):
        left = w // 2
        base = HALO + r0
        if w <= 4:
            win = e_scr[g, base - left:base - left + rp, :]
            for k in range(-left + 1, w - left):
                win = win + e_scr[g, base + k:base + k + rp, :]
            return win
        j0 = base - HALO
        src, width, n = e_scr.at[g], 1, rp + 3 * HALO
        level = 0
        while 2 * width < w:
            c_scr[level, j0:j0 + n, :] = src[j0:j0 + n, :] + src[j0 + width:j0 + width + n, :]
            src, width, n, level = c_scr.at[level], 2 * width, n - HALO, level + 1
        lo = base - left
        return src[lo:lo + rp, :] + src[lo + width:lo + width + rp, :]

    def pooled(r0):
        y_groups = []
        for g, w in enumerate(POOL_WINDOWS):
            right = w - 1 - w // 2
            win = window_sum(g, w, r0)
            center = zp_ref[r0:r0 + rp, g * POOL_GROUP:(g + 1) * POOL_GROUP]
            if r0 + rp < tm:
                d = win * (1.0 / w) - center
            else:
                body = win[0:rp - HALO] * (1.0 / w) - center[0:rp - HALO]
                pos = p0 + (tm - HALO) + tail_row
                over = jnp.maximum(pos + (right - (seq - 1)), 0)
                cnt = (w - over).astype(F32)
                d = jnp.concatenate([body, win[rp - HALO:] / cnt - center[rp - HALO:]], axis=0)
            yg = jnp.dot(d.astype(BF16), pw_ref[g], preferred_element_type=F32)
            y_groups.append((yg * ps_ref[:, g * POOL_GROUP:(g + 1) * POOL_GROUP]).astype(BF16))
        return jnp.concatenate(y_groups + [ya_ref[r0:r0 + rp, :]], axis=-1)

    def out_proj(r0, ycat):
        return x_ref[r0:r0 + rp, :] + jnp.dot(ycat, wo_ref[...], preferred_element_type=F32)

    def activation(f, cols):
        gate = jnp.dot(f, wg_ref[:, cols], preferred_element_type=F32)
        up = jnp.dot(f, wu_ref[:, cols], preferred_element_type=F32)
        return (gate * jax.nn.sigmoid(gate) * up).astype(BF16)

    def down(act, cols):
        return jnp.dot(act, wd_ref[cols, :], preferred_element_type=F32)

    parts = [slice(i * rp, (i + 1) * rp) for i in range(TAIL_PARTS)]
    chunks = [slice(c0, min(c0 + FF_CHUNK, D_FF)) for c0 in range(0, D_FF, FF_CHUNK)]
    f_parts, acc_parts = [], []
    for rows in parts:
        h1 = out_proj(rows.start, pooled(rows.start))
        f_parts.append(_rms(h1, g2_ref[...]).astype(BF16))
        acc_parts.append(h1)
    for i in range(TAIL_PARTS):
        acc_parts[i] = acc_parts[i] + down(activation(f_parts[i], chunks[0]), chunks[0])
    f = jnp.concatenate(f_parts, axis=0)
    acc = jnp.concatenate(acc_parts, axis=0)
    for cols in chunks[1:-1]:
        acc = acc + down(activation(f, cols), cols)
    act = activation(f, chunks[-1])
    for rows in parts:
        h2 = acc[rows] + down(act[rows], chunks[-1])
        o_ref[rows, :] = _rms(h2, gf_ref[...])


def _tail(x2d, zp, zp_meta, y_attn, pool_w, pool_scale, w_o, g2, w_gate, w_up, w_down, g_final,
          tm, seq):
    rows = x2d.shape[0]
    halo_blocks = tm // HALO
    last_halo_block = rows // HALO - 1
    const = lambda shape: pl.BlockSpec(shape, lambda i: (0,) * len(shape),
                                       pipeline_mode=pl.Buffered(1))
    return pl.pallas_call(
        functools.partial(_tail_kernel, tm=tm, seq=seq),
        grid=(rows // tm,),
        in_specs=[
            pl.BlockSpec((tm, D_MODEL), lambda i: (i, 0)),
            pl.BlockSpec((tm, POOL_WIDTH), lambda i: (i, 0)),
            pl.BlockSpec((HALO, POOL_WIDTH), lambda i: (jnp.maximum(i * halo_blocks - 1, 0), 0)),
            pl.BlockSpec((HALO, POOL_WIDTH),
                         lambda i: (jnp.minimum((i + 1) * halo_blocks, last_halo_block), 0)),
            pl.BlockSpec((HALO, POOL_WIDTH), lambda i: (N_META // HALO - 1, 0)),
            pl.BlockSpec((tm, N_HEADS * V_DIM), lambda i: (i, 0)),
            const((len(POOL_WINDOWS), POOL_GROUP, POOL_GROUP)),
            const((1, POOL_WIDTH)),
            const((D_MODEL, D_MODEL)),
            const((1, D_MODEL)),
            const((D_MODEL, D_FF)),
            const((D_MODEL, D_FF)),
            const((D_FF, D_MODEL)),
            const((1, D_MODEL)),
        ],
        out_specs=pl.BlockSpec((tm, D_MODEL), lambda i: (i, 0)),
        out_shape=jax.ShapeDtypeStruct((rows, D_MODEL), F32),
        scratch_shapes=[pltpu.VMEM((len(POOL_WINDOWS), tm + 4 * HALO, POOL_GROUP), F32),
                        pltpu.VMEM((3, tm + 4 * HALO, POOL_GROUP), F32)],
        compiler_params=pltpu.CompilerParams(
            dimension_semantics=("arbitrary",), vmem_limit_bytes=VMEM_LIMIT_BYTES),
        name="pool_oproj_ffn",
    )(x2d, zp, zp, zp, zp_meta, y_attn, pool_w, pool_scale, w_o, g2, w_gate, w_up, w_down, g_final)


def kernel(x, meta_tokens, rel_bias, norm1_g, w_in, pool_w, pool_scale, lambda_q1, lambda_k1,
           lambda_q2, lambda_k2, subln_g, w_o, norm2_g, w_gate, w_up, w_down, final_g):
    batch, seq, _ = x.shape
    layer = 0
    lambdas = [v[layer].astype(F32).reshape(1, HEAD_DIM)
               for v in (lambda_q1, lambda_k1, lambda_q2, lambda_k2)]

    col_scale = jnp.concatenate([
        jnp.ones((POOL_WIDTH,), F32), jnp.full((ATTN_WIDTH,), HEAD_DIM ** -0.5 * LOG2E, F32),
        jnp.ones((2 * ATTN_WIDTH,), F32)]).reshape(1, IN_WIDTH)
    g1 = norm1_g[layer].reshape(1, D_MODEL)

    x2d = x.reshape(batch * seq, D_MODEL)
    n_groups = len(POOL_WINDOWS)
    (zp, qt, k, vt, zp_meta, k_meta, vt_meta,
     w_o_b, w_gate_b, w_up_b, w_down_b, pool_w_b) = _inproj(
        x2d, meta_tokens, g1, w_in[layer], col_scale, tm=1024,
        cast_weights=(w_o[layer], w_gate[layer], w_up[layer], w_down[layer],
                      pool_w[layer].reshape(n_groups * POOL_GROUP, POOL_GROUP)))

    bias = _bias_tiles(rel_bias)
    g2d = jnp.broadcast_to(subln_g[layer].astype(F32)[:, None], (V_DIM, TQM))
    y_attn = _attention(
        lambdas, qt, k.reshape(batch, seq, ATTN_WIDTH),
        vt, k_meta, vt_meta, bias, g2d, batch, seq)

    out = _tail(
        x2d, zp, zp_meta, y_attn.reshape(batch * seq, N_HEADS * V_DIM),
        pool_w_b.reshape(n_groups, POOL_GROUP, POOL_GROUP), pool_scale[layer].reshape(1, POOL_WIDTH),
        w_o_b, norm2_g[layer].reshape(1, D_MODEL), w_gate_b, w_up_b, w_down_b,
        final_g.reshape(1, D_MODEL), tm=1024, seq=seq)
    return out.reshape(batch, seq, D_MODEL)
```

```python
import functools
import math

import jax
import jax.numpy as jnp
from jax import lax
from jax.experimental import pallas as pl
from jax.experimental.pallas import tpu as pltpu

F32 = jnp.float32
BF16 = jnp.bfloat16

D_MODEL = 1024
N_META = 16
META_PAD = 128
POOL_WIDTH = 512
POOL_WINDOWS = (2, 4, 8, 16)
POOL_GROUP = 128
ATTN_WIDTH = 512
HEAD_DIM = 64
N_HEADS = 4
V_DIM = 128
IN_WIDTH = POOL_WIDTH + 3 * ATTN_WIDTH
REL_BUCKETS = 32
REL_MAX_DIST = 128
D_FF = 2816
NORM_EPS = 1e-6
SUBLN_EPS = 1e-5
LAMBDA_INIT = 0.8 - 0.6 * math.exp(-0.3 * 0)
HALO = 8

VMEM_LIMIT_BYTES = 56 * 1024 * 1024
BF16_SUBLANES = 16

TQ = 256
TQM = 512
TK = 256
N_BIAS_TILES = 5
BIAS_ROWS = N_BIAS_TILES * TK + 2 * N_META
FF_CHUNK = 256
ONES_ROWS = 16
LOG2E = math.log2(math.e)
CHUNK_SKEW = 2
SCORE_LANE_PAD = 128
TAIL_PARTS = 2
INPROJ_PARTS = 2
ATTN_HEADS_PER_STEP = 1


def _inproj_kernel(x_ref, meta_ref, g_ref, w_ref, cs_ref, *refs, n_cast):
    cast_in = refs[:n_cast]
    zp_ref, qt_ref, k_ref, vt_ref, zpm_ref, km_ref, vmt_ref = refs[n_cast:n_cast + 7]
    cast_out = refs[n_cast + 7:-1]
    wb_scr = refs[-1]
    o_q = POOL_WIDTH
    o_k = o_q + ATTN_WIDTH
    o_v = o_k + ATTN_WIDTH

    def project(x, zp_out, qt_out, k_out, vt_out):
        ms = jnp.mean(x * x, axis=-1, keepdims=True)
        u = ((x * lax.rsqrt(ms + NORM_EPS)) * g_ref[...]).astype(BF16)
        proj = lambda lo, hi: jnp.dot(u, wb_scr[:, lo:hi], preferred_element_type=F32)
        zp_out[...] = proj(0, o_q)
        if qt_out is not None:
            qt_out[...] = proj(o_q, o_k).T.astype(BF16)
        k_out[...] = proj(o_k, o_v).astype(BF16)
        vt_out[...] = proj(o_v, IN_WIDTH).T.astype(BF16)

    @pl.when(pl.program_id(0) == 0)
    def _():
        wb_scr[...] = (w_ref[...] * cs_ref[...]).astype(BF16)
        meta = jnp.concatenate(
            [meta_ref[...], jnp.zeros((META_PAD - N_META, D_MODEL), F32)], axis=0)
        project(meta, zpm_ref, None, km_ref, vmt_ref)

    for src, dst in zip(cast_in, cast_out):
        dst[...] = src[...].astype(BF16)
    rp = x_ref.shape[0] // INPROJ_PARTS
    for i in range(INPROJ_PARTS):
        rows = slice(i * rp, (i + 1) * rp)
        project(x_ref[rows, :], zp_ref.at[rows, :], qt_ref.at[:, rows], k_ref.at[rows, :],
                vt_ref.at[:, rows])


def _inproj(x2d, meta_tokens, g, w_in, col_scale, tm, cast_weights=()):
    rows = x2d.shape[0]
    steps = rows // tm
    row_spec = lambda width: pl.BlockSpec((tm, width), lambda i: (i, 0))
    col_spec = lambda width: pl.BlockSpec((width, tm), lambda i: (0, i))
    const = lambda shape: pl.BlockSpec(shape, lambda i: (0, 0), pipeline_mode=pl.Buffered(1))
    fixed = lambda shape: pl.BlockSpec(shape, lambda i: (0, 0))

    def cast_spec(w):
        n_rows, n_cols = w.shape
        block = min(d for d in range(BF16_SUBLANES, n_rows + 1, BF16_SUBLANES)
                    if n_rows % d == 0 and d * steps >= n_rows)
        last = n_rows // block - 1
        return pl.BlockSpec((block, n_cols), lambda i: (jnp.minimum(i, last), 0))

    cast_specs = [cast_spec(w) for w in cast_weights]
    return pl.pallas_call(
        functools.partial(_inproj_kernel, n_cast=len(cast_weights)),
        grid=(steps,),
        in_specs=[
            row_spec(D_MODEL),
            const((N_META, D_MODEL)),
            const((1, D_MODEL)),
            const((D_MODEL, IN_WIDTH)),
            const((1, IN_WIDTH)),
            *cast_specs,
        ],
        out_specs=[row_spec(POOL_WIDTH), col_spec(ATTN_WIDTH), row_spec(ATTN_WIDTH),
                   col_spec(ATTN_WIDTH),
                   fixed((META_PAD, POOL_WIDTH)), fixed((META_PAD, ATTN_WIDTH)),
                   fixed((ATTN_WIDTH, META_PAD)), *cast_specs],
        out_shape=[
            jax.ShapeDtypeStruct((rows, POOL_WIDTH), F32),
            jax.ShapeDtypeStruct((ATTN_WIDTH, rows), BF16),
            jax.ShapeDtypeStruct((rows, ATTN_WIDTH), BF16),
            jax.ShapeDtypeStruct((ATTN_WIDTH, rows), BF16),
            jax.ShapeDtypeStruct((META_PAD, POOL_WIDTH), F32),
            jax.ShapeDtypeStruct((META_PAD, ATTN_WIDTH), BF16),
            jax.ShapeDtypeStruct((ATTN_WIDTH, META_PAD), BF16),
            *[jax.ShapeDtypeStruct(w.shape, BF16) for w in cast_weights],
        ],
        scratch_shapes=[pltpu.VMEM((D_MODEL, IN_WIDTH), BF16)],
        compiler_params=pltpu.CompilerParams(
            dimension_semantics=("arbitrary",), vmem_limit_bytes=VMEM_LIMIT_BYTES),
        name="inproj",
    )(x2d, meta_tokens, g, w_in, col_scale, *cast_weights)


def _t5_bucket(rel):
    nb = REL_BUCKETS // 2
    ret = jnp.where(rel > 0, nb, 0)
    n = jnp.abs(rel)
    max_exact = nb // 2
    nf = jnp.maximum(n, 1).astype(F32)
    large = max_exact + (jnp.log(nf / max_exact) / math.log(REL_MAX_DIST / max_exact)
                         * (nb - max_exact)).astype(jnp.int32)
    large = jnp.minimum(large, nb - 1)
    return ret + jnp.where(n < max_exact, n, large)


def _bias_buckets():
    r = lax.broadcasted_iota(jnp.int32, (TK, TQ), 0)
    c = lax.broadcasted_iota(jnp.int32, (TK, TQ), 1)
    tiles = [(t - 2) * TK + r - c for t in range(N_BIAS_TILES)]
    rm = lax.broadcasted_iota(jnp.int32, (N_META, TQ), 0)
    cm = lax.broadcasted_iota(jnp.int32, (N_META, TQ), 1)
    metas = [rm - N_META - m * TQ - cm for m in range(2)]
    return _t5_bucket(jnp.concatenate(tiles + metas, axis=0))


def _bias_kernel(rb_ref, bkt_ref, o_ref):
    h = pl.program_id(0)

    def lookup(bkt):
        level = [rb_ref[b, h] * LOG2E for b in range(REL_BUCKETS)]
        for bit in range(REL_BUCKETS.bit_length() - 1):
            odd = (bkt & (1 << bit)) != 0
            level = [jnp.where(odd, level[2 * j + 1], level[2 * j]) for j in range(len(level) // 2)]
        return level[0]

    def fill(r0, rows, constant):
        if constant:
            slab = lookup(bkt_ref[r0:r0 + 8, :])
            o_ref[r0:r0 + rows, :] = jnp.broadcast_to(slab[None], (rows // 8, 8, TQ)).reshape(rows, TQ)
        else:
            o_ref[r0:r0 + rows, :] = lookup(bkt_ref[r0:r0 + rows, :])

    for t in range(N_BIAS_TILES):
        fill(t * TK, TK, constant=t in (0, N_BIAS_TILES - 1))
    for m in range(2):
        fill(N_BIAS_TILES * TK + m * N_META, N_META, constant=m == 1)


def _bias_tiles(rel_bias):
    return pl.pallas_call(
        _bias_kernel,
        grid=(N_HEADS,),
        in_specs=[
            pl.BlockSpec(memory_space=pltpu.SMEM),
            pl.BlockSpec((BIAS_ROWS, TQ), lambda h: (0, 0)),
        ],
        out_specs=pl.BlockSpec((None, BIAS_ROWS, TQ), lambda h: (h, 0, 0)),
        out_shape=jax.ShapeDtypeStruct((N_HEADS, BIAS_ROWS, TQ), F32),
        compiler_params=pltpu.CompilerParams(dimension_semantics=("arbitrary",)),
        name="bias_tiles",
    )(rel_bias, _bias_buckets())


_NT = (((1,), (1,)), ((), ()))


def _attn_kernel(lq1_ref, lk1_ref, lq2_ref, lk2_ref, zero_ref, qt_ref, k_ref, vt_ref, km_ref, vmt_ref, bias_ref, g_ref, o_ref,
                 s0_scr, s1_scr, s2_scr, vt_scr, o0_scr, vmt_scr, *, seq, hp):
    n_kc = seq // TK
    units_per_head = 2 * (seq // TQM)
    n_units = hp * units_per_head
    halves = TQM // TQ
    s_scr = (s0_scr, s1_scr, s2_scr)

    for hd in range(hp):
        vt_scr[hd, 0:V_DIM, :] = vt_ref[hd * V_DIM:(hd + 1) * V_DIM, :]
        vt_scr[hd, V_DIM:, :] = jnp.ones((ONES_ROWS, seq), BF16)
        vmt_scr[hd, 0:V_DIM, :] = vmt_ref[hd * V_DIM:(hd + 1) * V_DIM, 0:N_META]
        vmt_scr[hd, V_DIM:, :] = jnp.ones((ONES_ROWS, N_META), BF16)
    lam = (jnp.exp(jnp.sum(lq1_ref[...] * lk1_ref[...], axis=1, keepdims=True))
           - jnp.exp(jnp.sum(lq2_ref[...] * lk2_ref[...], axis=1, keepdims=True))
           + LAMBDA_INIT)
    z = zero_ref[0, 0]
    zero_half = jnp.zeros((HEAD_DIM, TQM), BF16)

    def masked_qt(u):
        hd, lu = divmod(u, units_per_head)
        qi, c = divmod(lu, 2)
        qt = qt_ref[hd * V_DIM:(hd + 1) * V_DIM, qi * TQM:(qi + 1) * TQM]
        if c == 0:
            return jnp.concatenate([qt[0:HEAD_DIM], zero_half], axis=0)
        return jnp.concatenate([zero_half, qt[HEAD_DIM:]], axis=0)

    def bias_tile(kc, qh):
        if kc < n_kc:
            t = min(max(kc - qh, -2), 2) + 2
            return t * TK, TK, t in (0, N_BIAS_TILES - 1)
        return N_BIAS_TILES * TK + min(qh, 1) * N_META, N_META, qh >= 1

    def score_chunk(u, kc, qt, m_acc):
        hd, lu = divmod(u, units_per_head)
        qi = lu // 2
        head_lanes = slice(hd * V_DIM, (hd + 1) * V_DIM)
        if kc < n_kc:
            rows = pl.ds(pl.multiple_of(z + kc * TK, TK), TK)
            s = jnp.dot(k_ref[kc * TK:(kc + 1) * TK, head_lanes], qt,
                        preferred_element_type=F32)
        else:
            rows = pl.ds(pl.multiple_of(z + seq, N_META), N_META)
            s = jnp.dot(km_ref[0:N_META, head_lanes], qt, preferred_element_type=F32)
        m_acc = [dict(a) for a in m_acc] if m_acc else [{} for _ in range(halves)]
        for j in range(halves):
            cols = slice(j * TQ, (j + 1) * TQ)
            b0, nrows, constant = bias_tile(kc, qi * halves + j)
            sc = s[:, cols] if constant else s[:, cols] + bias_ref[hd, b0:b0 + nrows, :]
            s_scr[u % len(s_scr)][rows, cols] = sc
            cm = jnp.max(sc.reshape(nrows // 8, 8, TQ), axis=0)
            key = b0 if constant else None
            m_acc[j][key] = jnp.maximum(m_acc[j][key], cm) if key in m_acc[j] else cm
        return m_acc

    def col_max(u, m_acc):
        hd = u // units_per_head
        out = []
        for acc in m_acc:
            m = None
            for key, a in acc.items():
                v = jnp.max(a, axis=0, keepdims=True)
                if key is not None:
                    v = v + bias_ref[hd, key:key + 1, :]
                m = v if m is None else jnp.maximum(m, v)
            out.append(m)
        return out

    def exp_offset(u, kc, m, cache):
        hd, lu = divmod(u, units_per_head)
        tiles = tuple(bias_tile(kc, (lu // 2) * halves + j) for j in range(halves))
        key = tuple(b0 if constant else None for b0, _, constant in tiles)
        if key not in cache:
            cache[key] = jnp.concatenate(
                [m[j] if b0 is None else m[j] - bias_ref[hd, b0:b0 + 1, :]
                 for j, b0 in enumerate(key)], axis=1)
        return cache[key]

    def value_chunk(u, kc, off, oe):
        hd = u // units_per_head
        if kc < n_kc:
            rows = pl.ds(pl.multiple_of(z + kc * TK, TK), TK)
            lhs = vt_scr[hd, :, kc * TK:(kc + 1) * TK]
        else:
            rows = pl.ds(pl.multiple_of(z + seq, N_META), N_META)
            lhs = vmt_scr[hd]
        p = jnp.exp2(s_scr[u % len(s_scr)][rows, 0:TQM] - off).astype(BF16)
        d = jnp.dot(lhs, p, preferred_element_type=F32)
        return d if oe is None else oe + d

    def finalize(u, oe0, oe1):
        hd, lu = divmod(u, units_per_head)
        qi = lu // 2
        o = (oe0[0:V_DIM] * (1.0 / oe0[V_DIM:V_DIM + 1])
             - oe1[0:V_DIM] * (lam / oe1[V_DIM:V_DIM + 1]))
        ms = jnp.mean(o * o, axis=0, keepdims=True)
        y = ((o * lax.rsqrt(ms + SUBLN_EPS)) * g_ref[...]) * (1.0 - LAMBDA_INIT)
        o_ref[qi * TQM:(qi + 1) * TQM, hd * V_DIM:(hd + 1) * V_DIM] = y.T.astype(BF16)

    n_chunks = n_kc + 1
    total = n_units * n_chunks
    lag = n_chunks + CHUNK_SKEW
    score_order = [n_kc] + list(range(n_kc))
    meta_pos = -lag % n_chunks
    value_order = list(range(n_kc))
    value_order.insert(meta_pos, n_kc)
    col_max_of, offsets_of = {}, {}
    qt = m_acc = oe = None
    for g in range(total + lag):
        if g < total:
            u, i = divmod(g, n_chunks)
            if i == 0:
                qt, m_acc = masked_qt(u), None
            m_acc = score_chunk(u, score_order[i], qt, m_acc)
            if i == n_chunks - 1:
                col_max_of[u], offsets_of[u] = col_max(u, m_acc), {}
        if g >= lag:
            u, i = divmod(g - lag, n_chunks)
            off = exp_offset(u, value_order[i], col_max_of[u], offsets_of[u])
            oe = value_chunk(u, value_order[i], off, None if i == 0 else oe)
            if i == n_chunks - 1:
                if u % 2 == 0:
                    o0_scr[...] = oe
                else:
                    finalize(u, o0_scr[...], oe)


def _attention(lambdas, qt, k, vt, k_meta, vt_meta, bias, g2d, batch, seq):
    hp = ATTN_HEADS_PER_STEP
    width = hp * V_DIM
    head_rows = lambda: pl.BlockSpec((None, seq, width), lambda h, b: (b, 0, h))
    return pl.pallas_call(
        functools.partial(_attn_kernel, seq=seq, hp=hp),
        grid=(N_HEADS // hp, batch),
        in_specs=[
            *[pl.BlockSpec((1, HEAD_DIM), lambda h, b: (0, 0)) for _ in lambdas],
            pl.BlockSpec(memory_space=pltpu.SMEM),
            pl.BlockSpec((width, seq), lambda h, b: (h, b)),
            head_rows(),
            pl.BlockSpec((width, seq), lambda h, b: (h, b)),
            pl.BlockSpec((META_PAD, width), lambda h, b: (0, h)),
            pl.BlockSpec((width, META_PAD), lambda h, b: (h, 0)),
            pl.BlockSpec((hp, BIAS_ROWS, TQ), lambda h, b: (h, 0, 0)),
            pl.BlockSpec((V_DIM, TQM), lambda h, b: (0, 0)),
        ],
        out_specs=head_rows(),
        out_shape=jax.ShapeDtypeStruct((batch, seq, N_HEADS * V_DIM), BF16),
        scratch_shapes=[
            pltpu.VMEM((seq + N_META, TQM + SCORE_LANE_PAD), F32),
            pltpu.VMEM((seq + N_META, TQM + SCORE_LANE_PAD), F32),
            pltpu.VMEM((seq + N_META, TQM + SCORE_LANE_PAD), F32),
            pltpu.VMEM((hp, V_DIM + ONES_ROWS, seq), BF16),
            pltpu.VMEM((V_DIM + ONES_ROWS, TQM), F32),
            pltpu.VMEM((hp, V_DIM + ONES_ROWS, N_META), BF16),
        ],
        compiler_params=pltpu.CompilerParams(
            dimension_semantics=("arbitrary", "arbitrary"), vmem_limit_bytes=VMEM_LIMIT_BYTES),
        name="diff_attn",
    )(*lambdas, jnp.zeros((1, 1), jnp.int32), qt, k, vt, k_meta, vt_meta, bias, g2d)


def _rms(x, g):
    ms = jnp.mean(x * x, axis=-1, keepdims=True)
    return (x * lax.rsqrt(ms + NORM_EPS)) * g


def _tail_kernel(x_ref, zp_ref, zprev_ref, znext_ref, zmeta_ref, ya_ref, pw_ref, ps_ref, wo_ref,
                 g2_ref, wg_ref, wu_ref, wd_ref, gf_ref, o_ref, e_scr, c_scr, *, tm, seq):
    tiles_per_seq = seq // tm
    t = lax.rem(pl.program_id(0), tiles_per_seq)
    p0 = t * tm
    prev = jnp.where(t == 0, zmeta_ref[...], zprev_ref[...])
    nxt = jnp.where(t == tiles_per_seq - 1, 0.0, znext_ref[...])
    zp = zp_ref[...]
    for g in range(len(POOL_WINDOWS)):
        lanes = slice(g * POOL_GROUP, (g + 1) * POOL_GROUP)
        e_scr[g, 0:HALO, :] = prev[:, lanes]
        e_scr[g, HALO:HALO + tm, :] = zp[:, lanes]
        e_scr[g, HALO + tm:2 * HALO + tm, :] = nxt[:, lanes]
        e_scr[g, 2 * HALO + tm:, :] = jnp.zeros((2 * HALO, POOL_GROUP), F32)

    rp = tm // TAIL_PARTS
    tail_row = lax.broadcasted_iota(jnp.int32, (HALO, POOL_GROUP), 0)

    def window_sum(g, w, r0):
        left = w // 2
        base = HALO + r0
        if w <= 4:
            win = e_scr[g, base - left:base - left + rp, :]
            for k in range(-left + 1, w - left):
                win = win + e_scr[g, base + k:base + k + rp, :]
            return win
        j0 = base - HALO
        src, width, n = e_scr.at[g], 1, rp + 3 * HALO
        level = 0
        while 2 * width < w:
            c_scr[level, j0:j0 + n, :] = src[j0:j0 + n, :] + src[j0 + width:j0 + width + n, :]
            src, width, n, level = c_scr.at[level], 2 * width, n - HALO, level + 1
        lo = base - left
        return src[lo:lo + rp, :] + src[lo + width:lo + width + rp, :]

    def pooled(r0):
        y_groups = []
        for g, w in enumerate(POOL_WINDOWS):
            right = w - 1 - w // 2
            win = window_sum(g, w, r0)
            center = zp_ref[r0:r0 + rp, g * POOL_GROUP:(g + 1) * POOL_GROUP]
            if r0 + rp < tm:
                d = win * (1.0 / w) - center
            else:
                body = win[0:rp - HALO] * (1.0 / w) - center[0:rp - HALO]
                pos = p0 + (tm - HALO) + tail_row
                over = jnp.maximum(pos + (right - (seq - 1)), 0)
                cnt = (w - over).astype(F32)
                d = jnp.concatenate([body, win[rp - HALO:] / cnt - center[rp - HALO:]], axis=0)
            yg = jnp.dot(d.astype(BF16), pw_ref[g], preferred_element_type=F32)
            y_groups.append((yg * ps_ref[:, g * POOL_GROUP:(g + 1) * POOL_GROUP]).astype(BF16))
        return jnp.concatenate(y_groups + [ya_ref[r0:r0 + rp, :]], axis=-1)

    def out_proj(r0, ycat):
        return x_ref[r0:r0 + rp, :] + jnp.dot(ycat, wo_ref[...], preferred_element_type=F32)

    def activation(f, cols):
        gate = jnp.dot(f, wg_ref[:, cols], preferred_element_type=F32)
        up = jnp.dot(f, wu_ref[:, cols], preferred_element_type=F32)
        return (gate * jax.nn.sigmoid(gate) * up).astype(BF16)

    def down(act, cols):
        return jnp.dot(act, wd_ref[cols, :], preferred_element_type=F32)

    parts = [slice(i * rp, (i + 1) * rp) for i in range(TAIL_PARTS)]
    chunks = [slice(c0, min(c0 + FF_CHUNK, D_FF)) for c0 in range(0, D_FF, FF_CHUNK)]
    f_parts, acc_parts = [], []
    for rows in parts:
        h1 = out_proj(rows.start, pooled(rows.start))
        f_parts.append(_rms(h1, g2_ref[...]).astype(BF16))
        acc_parts.append(h1)
    for i in range(TAIL_PARTS):
        acc_parts[i] = acc_parts[i] + down(activation(f_parts[i], chunks[0]), chunks[0])
    f = jnp.concatenate(f_parts, axis=0)
    acc = jnp.concatenate(acc_parts, axis=0)
    for cols in chunks[1:-1]:
        acc = acc + down(activation(f, cols), cols)
    act = activation(f, chunks[-1])
    for rows in parts:
        h2 = acc[rows] + down(act[rows], chunks[-1])
        o_ref[rows, :] = _rms(h2, gf_ref[...])


def _tail(x2d, zp, zp_meta, y_attn, pool_w, pool_scale, w_o, g2, w_gate, w_up, w_down, g_final,
          tm, seq):
    rows = x2d.shape[0]
    halo_blocks = tm // HALO
    last_halo_block = rows // HALO - 1
    const = lambda shape: pl.BlockSpec(shape, lambda i: (0,) * len(shape),
                                       pipeline_mode=pl.Buffered(1))
    return pl.pallas_call(
        functools.partial(_tail_kernel, tm=tm, seq=seq),
        grid=(rows // tm,),
        in_specs=[
            pl.BlockSpec((tm, D_MODEL), lambda i: (i, 0)),
            pl.BlockSpec((tm, POOL_WIDTH), lambda i: (i, 0)),
            pl.BlockSpec((HALO, POOL_WIDTH), lambda i: (jnp.maximum(i * halo_blocks - 1, 0), 0)),
            pl.BlockSpec((HALO, POOL_WIDTH),
                         lambda i: (jnp.minimum((i + 1) * halo_blocks, last_halo_block), 0)),
            pl.BlockSpec((HALO, POOL_WIDTH), lambda i: (N_META // HALO - 1, 0)),
            pl.BlockSpec((tm, N_HEADS * V_DIM), lambda i: (i, 0)),
            const((len(POOL_WINDOWS), POOL_GROUP, POOL_GROUP)),
            const((1, POOL_WIDTH)),
            const((D_MODEL, D_MODEL)),
            const((1, D_MODEL)),
            const((D_MODEL, D_FF)),
            const((D_MODEL, D_FF)),
            const((D_FF, D_MODEL)),
            const((1, D_MODEL)),
        ],
        out_specs=pl.BlockSpec((tm, D_MODEL), lambda i: (i, 0)),
        out_shape=jax.ShapeDtypeStruct((rows, D_MODEL), F32),
        scratch_shapes=[pltpu.VMEM((len(POOL_WINDOWS), tm + 4 * HALO, POOL_GROUP), F32),
                        pltpu.VMEM((3, tm + 4 * HALO, POOL_GROUP), F32)],
        compiler_params=pltpu.CompilerParams(
            dimension_semantics=("arbitrary",), vmem_limit_bytes=VMEM_LIMIT_BYTES),
        name="pool_oproj_ffn",
    )(x2d, zp, zp, zp, zp_meta, y_attn, pool_w, pool_scale, w_o, g2, w_gate, w_up, w_down, g_final)


def kernel(x, meta_tokens, rel_bias, norm1_g, w_in, pool_w, pool_scale, lambda_q1, lambda_k1,
           lambda_q2, lambda_k2, subln_g, w_o, norm2_g, w_gate, w_up, w_down, final_g):
    batch, seq, _ = x.shape
    layer = 0
    lambdas = [v[layer].astype(F32).reshape(1, HEAD_DIM)
               for v in (lambda_q1, lambda_k1, lambda_q2, lambda_k2)]

    col_scale = jnp.concatenate([
        jnp.ones((POOL_WIDTH,), F32), jnp.full((ATTN_WIDTH,), HEAD_DIM ** -0.5 * LOG2E, F32),
        jnp.ones((2 * ATTN_WIDTH,), F32)]).reshape(1, IN_WIDTH)
    g1 = norm1_g[layer].reshape(1, D_MODEL)

    x2d = x.reshape(batch * seq, D_MODEL)
    n_groups = len(POOL_WINDOWS)
    (zp, qt, k, vt, zp_meta, k_meta, vt_meta,
     w_o_b, w_gate_b, w_up_b, w_down_b, pool_w_b) = _inproj(
        x2d, meta_tokens, g1, w_in[layer], col_scale, tm=1024,
        cast_weights=(w_o[layer], w_gate[layer], w_up[layer], w_down[layer],
                      pool_w[layer].reshape(n_groups * POOL_GROUP, POOL_GROUP)))

    bias = _bias_tiles(rel_bias)
    g2d = jnp.broadcast_to(subln_g[layer].astype(F32)[:, None], (V_DIM, TQM))
    y_attn = _attention(
        lambdas, qt, k.reshape(batch, seq, ATTN_WIDTH),
        vt, k_meta, vt_meta, bias, g2d, batch, seq)

    out = _tail(
        x2d, zp, zp_meta, y_attn.reshape(batch * seq, N_HEADS * V_DIM),
        pool_w_b.reshape(n_groups, POOL_GROUP, POOL_GROUP), pool_scale[layer].reshape(1, POOL_WIDTH),
        w_o_b, norm2_g[layer].reshape(1, D_MODEL), w_gate_b, w_up_b, w_down_b,
        final_g.reshape(1, D_MODEL), tm=1024, seq=seq)
    return out.reshape(batch, seq, D_MODEL)
```

```python
import functools
import math

import jax
import jax.numpy as jnp
import numpy as np
from jax import lax
from jax.experimental import pallas as pl
from jax.experimental.pallas import tpu as pltpu

F32 = jnp.float32
BF16 = jnp.bfloat16

D_MODEL = 1024
N_META = 16
META_PAD = 128
POOL_WIDTH = 512
POOL_WINDOWS = (2, 4, 8, 16)
POOL_GROUP = 128
ATTN_WIDTH = 512
HEAD_DIM = 64
N_HEADS = 4
V_DIM = 128
IN_WIDTH = POOL_WIDTH + 3 * ATTN_WIDTH
REL_BUCKETS = 32
REL_MAX_DIST = 128
D_FF = 2816
NORM_EPS = 1e-6
SUBLN_EPS = 1e-5
LAMBDA_INIT = 0.8 - 0.6 * math.exp(-0.3 * 0)
HALO = 8

VMEM_LIMIT_BYTES = 56 * 1024 * 1024
BF16_SUBLANES = 16

TQ = 256
TQM = 512
TK = 256
N_BIAS_TILES = 5
BIAS_ROWS = N_BIAS_TILES * TK + 2 * N_META
FF_CHUNK = 256
ONES_ROWS = 16
LOG2E = math.log2(math.e)
CHUNK_SKEW = 2
TAIL_PARTS = 2
INPROJ_PARTS = 2
ATTN_HEADS_PER_STEP = 1


def _inproj_kernel(x_ref, meta_ref, g_ref, w_ref, cs_ref, *refs, n_cast):
    cast_in = refs[:n_cast]
    zp_ref, qt_ref, k_ref, vt_ref, zpm_ref, km_ref, vmt_ref = refs[n_cast:n_cast + 7]
    cast_out = refs[n_cast + 7:-1]
    wb_scr = refs[-1]
    o_q = POOL_WIDTH
    o_k = o_q + ATTN_WIDTH
    o_v = o_k + ATTN_WIDTH

    def project(x, zp_out, qt_out, k_out, vt_out):
        ms = jnp.mean(x * x, axis=-1, keepdims=True)
        u = ((x * lax.rsqrt(ms + NORM_EPS)) * g_ref[...]).astype(BF16)
        proj = lambda lo, hi: jnp.dot(u, wb_scr[:, lo:hi], preferred_element_type=F32)
        zp_out[...] = proj(0, o_q)
        if qt_out is not None:
            qt_out[...] = proj(o_q, o_k).T.astype(BF16)
        k_out[...] = proj(o_k, o_v).astype(BF16)
        vt_out[...] = proj(o_v, IN_WIDTH).T.astype(BF16)

    @pl.when(pl.program_id(0) == 0)
    def _():
        wb_scr[...] = (w_ref[...] * cs_ref[...]).astype(BF16)
        meta = jnp.concatenate(
            [meta_ref[...], jnp.zeros((META_PAD - N_META, D_MODEL), F32)], axis=0)
        project(meta, zpm_ref, None, km_ref, vmt_ref)

    for src, dst in zip(cast_in, cast_out):
        dst[...] = src[...].astype(BF16)
    rp = x_ref.shape[0] // INPROJ_PARTS
    for i in range(INPROJ_PARTS):
        rows = slice(i * rp, (i + 1) * rp)
        project(x_ref[rows, :], zp_ref.at[rows, :], qt_ref.at[:, rows], k_ref.at[rows, :],
                vt_ref.at[:, rows])


def _inproj(x2d, meta_tokens, g, w_in, col_scale, tm, cast_weights=()):
    rows = x2d.shape[0]
    steps = rows // tm
    row_spec = lambda width: pl.BlockSpec((tm, width), lambda i: (i, 0))
    col_spec = lambda width: pl.BlockSpec((width, tm), lambda i: (0, i))
    const = lambda shape: pl.BlockSpec(shape, lambda i: (0, 0), pipeline_mode=pl.Buffered(1))
    fixed = lambda shape: pl.BlockSpec(shape, lambda i: (0, 0))

    def cast_spec(w):
        n_rows, n_cols = w.shape
        block = min(d for d in range(BF16_SUBLANES, n_rows + 1, BF16_SUBLANES)
                    if n_rows % d == 0 and d * steps >= n_rows)
        last = n_rows // block - 1
        return pl.BlockSpec((block, n_cols), lambda i: (jnp.minimum(i, last), 0))

    cast_specs = [cast_spec(w) for w in cast_weights]
    return pl.pallas_call(
        functools.partial(_inproj_kernel, n_cast=len(cast_weights)),
        grid=(steps,),
        in_specs=[
            row_spec(D_MODEL),
            const((N_META, D_MODEL)),
            const((1, D_MODEL)),
            const((D_MODEL, IN_WIDTH)),
            const((1, IN_WIDTH)),
            *cast_specs,
        ],
        out_specs=[row_spec(POOL_WIDTH), col_spec(ATTN_WIDTH), row_spec(ATTN_WIDTH),
                   col_spec(ATTN_WIDTH),
                   fixed((META_PAD, POOL_WIDTH)), fixed((META_PAD, ATTN_WIDTH)),
                   fixed((ATTN_WIDTH, META_PAD)), *cast_specs],
        out_shape=[
            jax.ShapeDtypeStruct((rows, POOL_WIDTH), F32),
            jax.ShapeDtypeStruct((ATTN_WIDTH, rows), BF16),
            jax.ShapeDtypeStruct((rows, ATTN_WIDTH), BF16),
            jax.ShapeDtypeStruct((ATTN_WIDTH, rows), BF16),
            jax.ShapeDtypeStruct((META_PAD, POOL_WIDTH), F32),
            jax.ShapeDtypeStruct((META_PAD, ATTN_WIDTH), BF16),
            jax.ShapeDtypeStruct((ATTN_WIDTH, META_PAD), BF16),
            *[jax.ShapeDtypeStruct(w.shape, BF16) for w in cast_weights],
        ],
        scratch_shapes=[pltpu.VMEM((D_MODEL, IN_WIDTH), BF16)],
        compiler_params=pltpu.CompilerParams(
            dimension_semantics=("arbitrary",), vmem_limit_bytes=VMEM_LIMIT_BYTES),
        name="inproj",
    )(x2d, meta_tokens, g, w_in, col_scale, *cast_weights)


def _t5_bucket(rel):
    nb = REL_BUCKETS // 2
    ret = jnp.where(rel > 0, nb, 0)
    n = jnp.abs(rel)
    max_exact = nb // 2
    nf = jnp.maximum(n, 1).astype(F32)
    large = max_exact + (jnp.log(nf / max_exact) / math.log(REL_MAX_DIST / max_exact)
                         * (nb - max_exact)).astype(jnp.int32)
    large = jnp.minimum(large, nb - 1)
    return ret + jnp.where(n < max_exact, n, large)


def _bias_buckets():
    r = lax.broadcasted_iota(jnp.int32, (TK, TQ), 0)
    c = lax.broadcasted_iota(jnp.int32, (TK, TQ), 1)
    tiles = [(t - 2) * TK + r - c for t in range(N_BIAS_TILES)]
    rm = lax.broadcasted_iota(jnp.int32, (N_META, TQ), 0)
    cm = lax.broadcasted_iota(jnp.int32, (N_META, TQ), 1)
    metas = [rm - N_META - m * TQ - cm for m in range(2)]
    return _t5_bucket(jnp.concatenate(tiles + metas, axis=0))


def _bias_kernel(rb_ref, bkt_ref, o_ref):
    h = pl.program_id(0)

    def lookup(bkt):
        level = [rb_ref[b, h] * LOG2E for b in range(REL_BUCKETS)]
        for bit in range(REL_BUCKETS.bit_length() - 1):
            odd = (bkt & (1 << bit)) != 0
            level = [jnp.where(odd, level[2 * j + 1], level[2 * j]) for j in range(len(level) // 2)]
        return level[0]

    def fill(r0, rows, constant):
        if constant:
            slab = lookup(bkt_ref[r0:r0 + 8, :])
            o_ref[r0:r0 + rows, :] = jnp.broadcast_to(slab[None], (rows // 8, 8, TQ)).reshape(rows, TQ)
        else:
            o_ref[r0:r0 + rows, :] = lookup(bkt_ref[r0:r0 + rows, :])

    for t in range(N_BIAS_TILES):
        fill(t * TK, TK, constant=t in (0, N_BIAS_TILES - 1))
    for m in range(2):
        fill(N_BIAS_TILES * TK + m * N_META, N_META, constant=m == 1)


def _bias_tiles(rel_bias):
    return pl.pallas_call(
        _bias_kernel,
        grid=(N_HEADS,),
        in_specs=[
            pl.BlockSpec(memory_space=pltpu.SMEM),
            pl.BlockSpec((BIAS_ROWS, TQ), lambda h: (0, 0)),
        ],
        out_specs=pl.BlockSpec((None, BIAS_ROWS, TQ), lambda h: (h, 0, 0)),
        out_shape=jax.ShapeDtypeStruct((N_HEADS, BIAS_ROWS, TQ), F32),
        compiler_params=pltpu.CompilerParams(dimension_semantics=("arbitrary",)),
        name="bias_tiles",
    )(rel_bias, _bias_buckets())


_NT = (((1,), (1,)), ((), ()))


def _attn_kernel(lq1_ref, lk1_ref, lq2_ref, lk2_ref, zero_ref, qt_ref, k_ref, vt_ref, km_ref, vmt_ref, bias_ref, g_ref, o_ref,
                 s0_scr, s1_scr, s2_scr, vt_scr, o0_scr, vmt_scr, *, seq, hp):
    n_kc = seq // TK
    units_per_head = 2 * (seq // TQM)
    n_units = hp * units_per_head
    halves = TQM // TQ
    s_scr = (s0_scr, s1_scr, s2_scr)

    for hd in range(hp):
        vt_scr[hd, 0:V_DIM, :] = vt_ref[hd * V_DIM:(hd + 1) * V_DIM, :]
        vt_scr[hd, V_DIM:, :] = jnp.ones((ONES_ROWS, seq), BF16)
        vmt_scr[hd, 0:V_DIM, :] = vmt_ref[hd * V_DIM:(hd + 1) * V_DIM, 0:N_META]
        vmt_scr[hd, V_DIM:, :] = jnp.ones((ONES_ROWS, N_META), BF16)
    lam = (jnp.exp(jnp.sum(lq1_ref[...] * lk1_ref[...], axis=1, keepdims=True))
           - jnp.exp(jnp.sum(lq2_ref[...] * lk2_ref[...], axis=1, keepdims=True))
           + LAMBDA_INIT)
    z = zero_ref[0, 0]
    zero_half = jnp.zeros((HEAD_DIM, TQM), BF16)

    def masked_qt(u):
        hd, lu = divmod(u, units_per_head)
        qi, c = divmod(lu, 2)
        qt = qt_ref[hd * V_DIM:(hd + 1) * V_DIM, qi * TQM:(qi + 1) * TQM]
        if c == 0:
            return jnp.concatenate([qt[0:HEAD_DIM], zero_half], axis=0)
        return jnp.concatenate([zero_half, qt[HEAD_DIM:]], axis=0)

    def bias_tile(kc, qh):
        if kc < n_kc:
            t = min(max(kc - qh, -2), 2) + 2
            return t * TK, TK, t in (0, N_BIAS_TILES - 1)
        return N_BIAS_TILES * TK + min(qh, 1) * N_META, N_META, qh >= 1

    def score_chunk(u, kc, qt, m_acc):
        hd, lu = divmod(u, units_per_head)
        qi = lu // 2
        head_lanes = slice(hd * V_DIM, (hd + 1) * V_DIM)
        if kc < n_kc:
            rows = pl.ds(pl.multiple_of(z + kc * TK, TK), TK)
            s = jnp.dot(k_ref[kc * TK:(kc + 1) * TK, head_lanes], qt,
                        preferred_element_type=F32)
        else:
            rows = pl.ds(pl.multiple_of(z + seq, N_META), N_META)
            s = jnp.dot(km_ref[0:N_META, head_lanes], qt, preferred_element_type=F32)
        m_acc = [dict(a) for a in m_acc] if m_acc else [{} for _ in range(halves)]
        for j in range(halves):
            cols = slice(j * TQ, (j + 1) * TQ)
            b0, nrows, constant = bias_tile(kc, qi * halves + j)
            sc = s[:, cols] if constant else s[:, cols] + bias_ref[hd, b0:b0 + nrows, :]
            s_scr[u % len(s_scr)][rows, cols] = sc
            cm = jnp.max(sc.reshape(nrows // 8, 8, TQ), axis=0)
            key = b0 if constant else None
            m_acc[j][key] = jnp.maximum(m_acc[j][key], cm) if key in m_acc[j] else cm
        return m_acc

    def col_max(u, m_acc):
        hd = u // units_per_head
        out = []
        for acc in m_acc:
            m = None
            for key, a in acc.items():
                v = jnp.max(a, axis=0, keepdims=True)
                if key is not None:
                    v = v + bias_ref[hd, key:key + 1, :]
                m = v if m is None else jnp.maximum(m, v)
            out.append(m)
        return out

    def exp_offset(u, kc, m, cache):
        hd, lu = divmod(u, units_per_head)
        tiles = tuple(bias_tile(kc, (lu // 2) * halves + j) for j in range(halves))
        key = tuple(b0 if constant else None for b0, _, constant in tiles)
        if key not in cache:
            cache[key] = jnp.concatenate(
                [m[j] if b0 is None else m[j] - bias_ref[hd, b0:b0 + 1, :]
                 for j, b0 in enumerate(key)], axis=1)
        return cache[key]

    def value_chunk(u, kc, off, oe):
        hd = u // units_per_head
        if kc < n_kc:
            rows = pl.ds(pl.multiple_of(z + kc * TK, TK), TK)
            lhs = vt_scr[hd, :, kc * TK:(kc + 1) * TK]
        else:
            rows = pl.ds(pl.multiple_of(z + seq, N_META), N_META)
            lhs = vmt_scr[hd]
        p = jnp.exp2(s_scr[u % len(s_scr)][rows, :] - off).astype(BF16)
        d = jnp.dot(lhs, p, preferred_element_type=F32)
        return d if oe is None else oe + d

    def finalize(u, oe0, oe1):
        hd, lu = divmod(u, units_per_head)
        qi = lu // 2
        o = (oe0[0:V_DIM] * (1.0 / oe0[V_DIM:V_DIM + 1])
             - oe1[0:V_DIM] * (lam / oe1[V_DIM:V_DIM + 1]))
        ms = jnp.mean(o * o, axis=0, keepdims=True)
        y = ((o * lax.rsqrt(ms + SUBLN_EPS)).T * g_ref[...]) * (1.0 - LAMBDA_INIT)
        o_ref[qi * TQM:(qi + 1) * TQM, hd * V_DIM:(hd + 1) * V_DIM] = y.astype(BF16)

    n_chunks = n_kc + 1
    total = n_units * n_chunks
    lag = n_chunks + CHUNK_SKEW
    score_order = [n_kc] + list(range(n_kc))
    meta_pos = -lag % n_chunks
    value_order = list(range(n_kc))
    value_order.insert(meta_pos, n_kc)
    col_max_of, offsets_of = {}, {}
    qt = m_acc = oe = None
    for g in range(total + lag):
        if g < total:
            u, i = divmod(g, n_chunks)
            if i == 0:
                qt, m_acc = masked_qt(u), None
            m_acc = score_chunk(u, score_order[i], qt, m_acc)
            if i == n_chunks - 1:
                col_max_of[u], offsets_of[u] = col_max(u, m_acc), {}
        if g >= lag:
            u, i = divmod(g - lag, n_chunks)
            off = exp_offset(u, value_order[i], col_max_of[u], offsets_of[u])
            oe = value_chunk(u, value_order[i], off, None if i == 0 else oe)
            if i == n_chunks - 1:
                if u % 2 == 0:
                    o0_scr[...] = oe
                else:
                    finalize(u, o0_scr[...], oe)


def _attention(lambdas, qt, k, vt, k_meta, vt_meta, bias, subln_g, batch, seq):
    hp = ATTN_HEADS_PER_STEP
    width = hp * V_DIM
    head_rows = lambda: pl.BlockSpec((None, seq, width), lambda h, b: (b, 0, h))
    return pl.pallas_call(
        functools.partial(_attn_kernel, seq=seq, hp=hp),
        grid=(N_HEADS // hp, batch),
        in_specs=[
            *[pl.BlockSpec((1, HEAD_DIM), lambda h, b: (0, 0)) for _ in lambdas],
            pl.BlockSpec(memory_space=pltpu.SMEM),
            pl.BlockSpec((width, seq), lambda h, b: (h, b)),
            head_rows(),
            pl.BlockSpec((width, seq), lambda h, b: (h, b)),
            pl.BlockSpec((META_PAD, width), lambda h, b: (0, h)),
            pl.BlockSpec((width, META_PAD), lambda h, b: (h, 0)),
            pl.BlockSpec((hp, BIAS_ROWS, TQ), lambda h, b: (h, 0, 0)),
            pl.BlockSpec((1, V_DIM), lambda h, b: (0, 0)),
        ],
        out_specs=head_rows(),
        out_shape=jax.ShapeDtypeStruct((batch, seq, N_HEADS * V_DIM), BF16),
        scratch_shapes=[
            pltpu.VMEM((seq + N_META, TQM), F32),
            pltpu.VMEM((seq + N_META, TQM), F32),
            pltpu.VMEM((seq + N_META, TQM), F32),
            pltpu.VMEM((hp, V_DIM + ONES_ROWS, seq), BF16),
            pltpu.VMEM((V_DIM + ONES_ROWS, TQM), F32),
            pltpu.VMEM((hp, V_DIM + ONES_ROWS, N_META), BF16),
        ],
        compiler_params=pltpu.CompilerParams(
            dimension_semantics=("arbitrary", "arbitrary"), vmem_limit_bytes=VMEM_LIMIT_BYTES),
        name="diff_attn",
    )(*lambdas, np.zeros((1, 1), np.int32), qt, k, vt, k_meta, vt_meta, bias, subln_g)


def _rms(x, g):
    ms = jnp.mean(x * x, axis=-1, keepdims=True)
    return (x * lax.rsqrt(ms + NORM_EPS)) * g


def _tail_kernel(x_ref, zp_ref, zprev_ref, znext_ref, zmeta_ref, ya_ref, pw_ref, ps_ref, wo_ref,
                 g2_ref, wg_ref, wu_ref, wd_ref, gf_ref, o_ref, e_scr, c_scr, *, tm, seq):
    tiles_per_seq = seq // tm
    t = lax.rem(pl.program_id(0), tiles_per_seq)
    p0 = t * tm
    prev = jnp.where(t == 0, zmeta_ref[...], zprev_ref[...])
    nxt = jnp.where(t == tiles_per_seq - 1, 0.0, znext_ref[...])
    zp = zp_ref[...]
    for g in range(len(POOL_WINDOWS)):
        lanes = slice(g * POOL_GROUP, (g + 1) * POOL_GROUP)
        e_scr[g, 0:HALO, :] = prev[:, lanes]
        e_scr[g, HALO:HALO + tm, :] = zp[:, lanes]
        e_scr[g, HALO + tm:2 * HALO + tm, :] = nxt[:, lanes]
        e_scr[g, 2 * HALO + tm:, :] = jnp.zeros((2 * HALO, POOL_GROUP), F32)

    rp = tm // TAIL_PARTS
    tail_row = lax.broadcasted_iota(jnp.int32, (HALO, POOL_GROUP), 0)

    def window_sum(g, w, r0):
        left = w // 2
        base = HALO + r0
        if w <= 4:
            win = e_scr[g, base - left:base - left + rp, :]
            for k in range(-left + 1, w - left):
                win = win + e_scr[g, base + k:base + k + rp, :]
            return win
        j0 = base - HALO
        src, width, n = e_scr.at[g], 1, rp + 3 * HALO
        level = 0
        while 2 * width < w:
            c_scr[level, j0:j0 + n, :] = src[j0:j0 + n, :] + src[j0 + width:j0 + width + n, :]
            src, width, n, level = c_scr.at[level], 2 * width, n - HALO, level + 1
        lo = base - left
        return src[lo:lo + rp, :] + src[lo + width:lo + width + rp, :]

    def pooled(r0):
        y_groups = []
        for g, w in enumerate(POOL_WINDOWS):
            right = w - 1 - w // 2
            win = window_sum(g, w, r0)
            center = zp_ref[r0:r0 + rp, g * POOL_GROUP:(g + 1) * POOL_GROUP]
            if r0 + rp < tm:
                d = win * (1.0 / w) - center
            else:
                body = win[0:rp - HALO] * (1.0 / w) - center[0:rp - HALO]
                pos = p0 + (tm - HALO) + tail_row
                over = jnp.maximum(pos + (right - (seq - 1)), 0)
                cnt = (w - over).astype(F32)
                d = jnp.concatenate([body, win[rp - HALO:] / cnt - center[rp - HALO:]], axis=0)
            yg = jnp.dot(d.astype(BF16), pw_ref[g], preferred_element_type=F32)
            y_groups.append((yg * ps_ref[:, g * POOL_GROUP:(g + 1) * POOL_GROUP]).astype(BF16))
        return jnp.concatenate(y_groups + [ya_ref[r0:r0 + rp, :]], axis=-1)

    def out_proj(r0, ycat):
        return x_ref[r0:r0 + rp, :] + jnp.dot(ycat, wo_ref[...], preferred_element_type=F32)

    def activation(f, cols):
        gate = jnp.dot(f, wg_ref[:, cols], preferred_element_type=F32)
        up = jnp.dot(f, wu_ref[:, cols], preferred_element_type=F32)
        return (gate * jax.nn.sigmoid(gate) * up).astype(BF16)

    def down(act, cols):
        return jnp.dot(act, wd_ref[cols, :], preferred_element_type=F32)

    parts = [slice(i * rp, (i + 1) * rp) for i in range(TAIL_PARTS)]
    chunks = [slice(c0, min(c0 + FF_CHUNK, D_FF)) for c0 in range(0, D_FF, FF_CHUNK)]
    f_parts, acc_parts = [], []
    for rows in parts:
        h1 = out_proj(rows.start, pooled(rows.start))
        f_parts.append(_rms(h1, g2_ref[...]).astype(BF16))
        acc_parts.append(h1)
    for i in range(TAIL_PARTS):
        acc_parts[i] = acc_parts[i] + down(activation(f_parts[i], chunks[0]), chunks[0])
    f = jnp.concatenate(f_parts, axis=0)
    acc = jnp.concatenate(acc_parts, axis=0)
    for cols in chunks[1:-1]:
        acc = acc + down(activation(f, cols), cols)
    act = activation(f, chunks[-1])
    for rows in parts:
        h2 = acc[rows] + down(act[rows], chunks[-1])
        o_ref[rows, :] = _rms(h2, gf_ref[...])


def _tail(x2d, zp, zp_meta, y_attn, pool_w, pool_scale, w_o, g2, w_gate, w_up, w_down, g_final,
          tm, seq):
    rows = x2d.shape[0]
    halo_blocks = tm // HALO
    last_halo_block = rows // HALO - 1
    const = lambda shape: pl.BlockSpec(shape, lambda i: (0,) * len(shape),
                                       pipeline_mode=pl.Buffered(1))
    return pl.pallas_call(
        functools.partial(_tail_kernel, tm=tm, seq=seq),
        grid=(rows // tm,),
        in_specs=[
            pl.BlockSpec((tm, D_MODEL), lambda i: (i, 0)),
            pl.BlockSpec((tm, POOL_WIDTH), lambda i: (i, 0)),
            pl.BlockSpec((HALO, POOL_WIDTH), lambda i: (jnp.maximum(i * halo_blocks - 1, 0), 0)),
            pl.BlockSpec((HALO, POOL_WIDTH),
                         lambda i: (jnp.minimum((i + 1) * halo_blocks, last_halo_block), 0)),
            pl.BlockSpec((HALO, POOL_WIDTH), lambda i: (N_META // HALO - 1, 0)),
            pl.BlockSpec((tm, N_HEADS * V_DIM), lambda i: (i, 0)),
            const((len(POOL_WINDOWS), POOL_GROUP, POOL_GROUP)),
            const((1, POOL_WIDTH)),
            const((D_MODEL, D_MODEL)),
            const((1, D_MODEL)),
            const((D_MODEL, D_FF)),
            const((D_MODEL, D_FF)),
            const((D_FF, D_MODEL)),
            const((1, D_MODEL)),
        ],
        out_specs=pl.BlockSpec((tm, D_MODEL), lambda i: (i, 0)),
        out_shape=jax.ShapeDtypeStruct((rows, D_MODEL), F32),
        scratch_shapes=[pltpu.VMEM((len(POOL_WINDOWS), tm + 4 * HALO, POOL_GROUP), F32),
                        pltpu.VMEM((3, tm + 4 * HALO, POOL_GROUP), F32)],
        compiler_params=pltpu.CompilerParams(
            dimension_semantics=("arbitrary",), vmem_limit_bytes=VMEM_LIMIT_BYTES),
        name="pool_oproj_ffn",
    )(x2d, zp, zp, zp, zp_meta, y_attn, pool_w, pool_scale, w_o, g2, w_gate, w_up, w_down, g_final)


def kernel(x, meta_tokens, rel_bias, norm1_g, w_in, pool_w, pool_scale, lambda_q1, lambda_k1,
           lambda_q2, lambda_k2, subln_g, w_o, norm2_g, w_gate, w_up, w_down, final_g):
    batch, seq, _ = x.shape
    layer = 0
    lambdas = [v[layer].astype(F32).reshape(1, HEAD_DIM)
               for v in (lambda_q1, lambda_k1, lambda_q2, lambda_k2)]

    col_scale = np.ones((1, IN_WIDTH), np.float32)
    col_scale[:, POOL_WIDTH:POOL_WIDTH + ATTN_WIDTH] = HEAD_DIM ** -0.5 * LOG2E
    g1 = norm1_g[layer].reshape(1, D_MODEL)

    x2d = x.reshape(batch * seq, D_MODEL)
    n_groups = len(POOL_WINDOWS)
    (zp, qt, k, vt, zp_meta, k_meta, vt_meta,
     w_o_b, w_gate_b, w_up_b, w_down_b, pool_w_b) = _inproj(
        x2d, meta_tokens, g1, w_in[layer], col_scale, tm=1024,
        cast_weights=(w_o[layer], w_gate[layer], w_up[layer], w_down[layer],
                      pool_w[layer].reshape(n_groups * POOL_GROUP, POOL_GROUP)))

    bias = _bias_tiles(rel_bias)
    y_attn = _attention(
        lambdas, qt, k.reshape(batch, seq, ATTN_WIDTH),
        vt, k_meta, vt_meta, bias, subln_g[layer].astype(F32).reshape(1, V_DIM), batch, seq)

    out = _tail(
        x2d, zp, zp_meta, y_attn.reshape(batch * seq, N_HEADS * V_DIM),
        pool_w_b.reshape(n_groups, POOL_GROUP, POOL_GROUP), pool_scale[layer].reshape(1, POOL_WIDTH),
        w_o_b, norm2_g[layer].reshape(1, D_MODEL), w_gate_b, w_up_b, w_down_b,
        final_g.reshape(1, D_MODEL), tm=1024, seq=seq)
    return out.reshape(batch, seq, D_MODEL)
```

```python
import functools
import math

import jax
import jax.numpy as jnp
import numpy as np
from jax import lax
from jax.experimental import pallas as pl
from jax.experimental.pallas import tpu as pltpu

F32 = jnp.float32
BF16 = jnp.bfloat16

D_MODEL = 1024
N_META = 16
META_PAD = 128
POOL_WIDTH = 512
POOL_WINDOWS = (2, 4, 8, 16)
POOL_GROUP = 128
ATTN_WIDTH = 512
HEAD_DIM = 64
N_HEADS = 4
V_DIM = 128
IN_WIDTH = POOL_WIDTH + 3 * ATTN_WIDTH
REL_BUCKETS = 32
REL_MAX_DIST = 128
D_FF = 2816
NORM_EPS = 1e-6
SUBLN_EPS = 1e-5
LAMBDA_INIT = 0.8 - 0.6 * math.exp(-0.3 * 0)
HALO = 8

VMEM_LIMIT_BYTES = 56 * 1024 * 1024
BF16_SUBLANES = 16

TQ = 256
TQM = 512
TK = 256
N_BIAS_TILES = 5
BIAS_ROWS = N_BIAS_TILES * TK + 2 * N_META
FF_CHUNK = 256
ONES_ROWS = 16
LOG2E = math.log2(math.e)
CHUNK_SKEW = 2
TAIL_PARTS = 2
INPROJ_PARTS = 2
ATTN_HEADS_PER_STEP = 1


def _inproj_kernel(x_ref, meta_ref, g_ref, w_ref, cs_ref, *refs, n_cast):
    cast_in = refs[:n_cast]
    zp_ref, qt_ref, k_ref, vt_ref, zpm_ref, km_ref, vmt_ref = refs[n_cast:n_cast + 7]
    cast_out = refs[n_cast + 7:-1]
    wb_scr = refs[-1]
    o_q = POOL_WIDTH
    o_k = o_q + ATTN_WIDTH
    o_v = o_k + ATTN_WIDTH

    def project(x, zp_out, qt_out, k_out, vt_out):
        ms = jnp.mean(x * x, axis=-1, keepdims=True)
        u = ((x * lax.rsqrt(ms + NORM_EPS)) * g_ref[...]).astype(BF16)
        proj = lambda lo, hi: jnp.dot(u, wb_scr[:, lo:hi], preferred_element_type=F32)
        zp_out[...] = proj(0, o_q)
        if qt_out is not None:
            qt_out[...] = proj(o_q, o_k).T.astype(BF16)
        k_out[...] = proj(o_k, o_v).astype(BF16)
        vt_out[...] = proj(o_v, IN_WIDTH).T.astype(BF16)

    @pl.when(pl.program_id(0) == 0)
    def _():
        wb_scr[...] = (w_ref[...] * cs_ref[...]).astype(BF16)
        meta = jnp.concatenate(
            [meta_ref[...], jnp.zeros((META_PAD - N_META, D_MODEL), F32)], axis=0)
        project(meta, zpm_ref, None, km_ref, vmt_ref)

    for src, dst in zip(cast_in, cast_out):
        dst[...] = src[...].astype(BF16)
    rp = x_ref.shape[0] // INPROJ_PARTS
    for i in range(INPROJ_PARTS):
        rows = slice(i * rp, (i + 1) * rp)
        project(x_ref[rows, :], zp_ref.at[rows, :], qt_ref.at[:, rows], k_ref.at[rows, :],
                vt_ref.at[:, rows])


def _inproj(x2d, meta_tokens, g, w_in, col_scale, tm, cast_weights=()):
    rows = x2d.shape[0]
    steps = rows // tm
    row_spec = lambda width: pl.BlockSpec((tm, width), lambda i: (i, 0))
    col_spec = lambda width: pl.BlockSpec((width, tm), lambda i: (0, i))
    const = lambda shape: pl.BlockSpec(shape, lambda i: (0, 0), pipeline_mode=pl.Buffered(1))
    fixed = lambda shape: pl.BlockSpec(shape, lambda i: (0, 0))

    def cast_spec(w):
        n_rows, n_cols = w.shape
        block = min(d for d in range(BF16_SUBLANES, n_rows + 1, BF16_SUBLANES)
                    if n_rows % d == 0 and d * steps >= n_rows)
        last = n_rows // block - 1
        return pl.BlockSpec((block, n_cols), lambda i: (jnp.minimum(i, last), 0))

    cast_specs = [cast_spec(w) for w in cast_weights]
    return pl.pallas_call(
        functools.partial(_inproj_kernel, n_cast=len(cast_weights)),
        grid=(steps,),
        in_specs=[
            row_spec(D_MODEL),
            const((N_META, D_MODEL)),
            const((1, D_MODEL)),
            const((D_MODEL, IN_WIDTH)),
            const((1, IN_WIDTH)),
            *cast_specs,
        ],
        out_specs=[row_spec(POOL_WIDTH), col_spec(ATTN_WIDTH), row_spec(ATTN_WIDTH),
                   col_spec(ATTN_WIDTH),
                   fixed((META_PAD, POOL_WIDTH)), fixed((META_PAD, ATTN_WIDTH)),
                   fixed((ATTN_WIDTH, META_PAD)), *cast_specs],
        out_shape=[
            jax.ShapeDtypeStruct((rows, POOL_WIDTH), F32),
            jax.ShapeDtypeStruct((ATTN_WIDTH, rows), BF16),
            jax.ShapeDtypeStruct((rows, ATTN_WIDTH), BF16),
            jax.ShapeDtypeStruct((ATTN_WIDTH, rows), BF16),
            jax.ShapeDtypeStruct((META_PAD, POOL_WIDTH), F32),
            jax.ShapeDtypeStruct((META_PAD, ATTN_WIDTH), BF16),
            jax.ShapeDtypeStruct((ATTN_WIDTH, META_PAD), BF16),
            *[jax.ShapeDtypeStruct(w.shape, BF16) for w in cast_weights],
        ],
        scratch_shapes=[pltpu.VMEM((D_MODEL, IN_WIDTH), BF16)],
        compiler_params=pltpu.CompilerParams(
            dimension_semantics=("arbitrary",), vmem_limit_bytes=VMEM_LIMIT_BYTES),
        name="inproj",
    )(x2d, meta_tokens, g, w_in, col_scale, *cast_weights)


def _t5_bucket(rel):
    nb = REL_BUCKETS // 2
    ret = jnp.where(rel > 0, nb, 0)
    n = jnp.abs(rel)
    max_exact = nb // 2
    nf = jnp.maximum(n, 1).astype(F32)
    large = max_exact + (jnp.log(nf / max_exact) / math.log(REL_MAX_DIST / max_exact)
                         * (nb - max_exact)).astype(jnp.int32)
    large = jnp.minimum(large, nb - 1)
    return ret + jnp.where(n < max_exact, n, large)


def _bias_buckets():
    r = lax.broadcasted_iota(jnp.int32, (TK, TQ), 0)
    c = lax.broadcasted_iota(jnp.int32, (TK, TQ), 1)
    tiles = [(t - 2) * TK + r - c for t in range(N_BIAS_TILES)]
    rm = lax.broadcasted_iota(jnp.int32, (N_META, TQ), 0)
    cm = lax.broadcasted_iota(jnp.int32, (N_META, TQ), 1)
    metas = [rm - N_META - m * TQ - cm for m in range(2)]
    return _t5_bucket(jnp.concatenate(tiles + metas, axis=0))


def _bias_kernel(rb_ref, bkt_ref, o_ref):
    h = pl.program_id(0)

    def lookup(bkt):
        level = [rb_ref[b, h] * LOG2E for b in range(REL_BUCKETS)]
        for bit in range(REL_BUCKETS.bit_length() - 1):
            odd = (bkt & (1 << bit)) != 0
            level = [jnp.where(odd, level[2 * j + 1], level[2 * j]) for j in range(len(level) // 2)]
        return level[0]

    def fill(r0, rows, constant):
        if constant:
            slab = lookup(bkt_ref[r0:r0 + 8, :])
            o_ref[r0:r0 + rows, :] = jnp.broadcast_to(slab[None], (rows // 8, 8, TQ)).reshape(rows, TQ)
        else:
            o_ref[r0:r0 + rows, :] = lookup(bkt_ref[r0:r0 + rows, :])

    for t in range(N_BIAS_TILES):
        fill(t * TK, TK, constant=t in (0, N_BIAS_TILES - 1))
    for m in range(2):
        fill(N_BIAS_TILES * TK + m * N_META, N_META, constant=m == 1)


def _bias_tiles(rel_bias):
    return pl.pallas_call(
        _bias_kernel,
        grid=(N_HEADS,),
        in_specs=[
            pl.BlockSpec(memory_space=pltpu.SMEM),
            pl.BlockSpec((BIAS_ROWS, TQ), lambda h: (0, 0)),
        ],
        out_specs=pl.BlockSpec((None, BIAS_ROWS, TQ), lambda h: (h, 0, 0)),
        out_shape=jax.ShapeDtypeStruct((N_HEADS, BIAS_ROWS, TQ), F32),
        compiler_params=pltpu.CompilerParams(dimension_semantics=("arbitrary",)),
        name="bias_tiles",
    )(rel_bias, _bias_buckets())


_NT = (((1,), (1,)), ((), ()))


def _attn_kernel(lq1_ref, lk1_ref, lq2_ref, lk2_ref, zero_ref, qt_ref, k_ref, vt_ref, km_ref, vmt_ref, bias_ref, g_ref, o_ref,
                 s0_scr, s1_scr, s2_scr, vt_scr, vmt_scr, *, seq, hp):
    n_kc = seq // TK
    units_per_head = seq // TQ
    n_units = hp * units_per_head
    halves = TQM // TQ
    s_scr = (s0_scr, s1_scr, s2_scr)

    for hd in range(hp):
        vt_scr[hd, 0:V_DIM, :] = vt_ref[hd * V_DIM:(hd + 1) * V_DIM, :]
        vt_scr[hd, V_DIM:, :] = jnp.ones((ONES_ROWS, seq), BF16)
        vmt_scr[hd, 0:V_DIM, :] = vmt_ref[hd * V_DIM:(hd + 1) * V_DIM, 0:N_META]
        vmt_scr[hd, V_DIM:, :] = jnp.ones((ONES_ROWS, N_META), BF16)
    lam = (jnp.exp(jnp.sum(lq1_ref[...] * lk1_ref[...], axis=1, keepdims=True))
           - jnp.exp(jnp.sum(lq2_ref[...] * lk2_ref[...], axis=1, keepdims=True))
           + LAMBDA_INIT)
    z = zero_ref[0, 0]
    zero_half = jnp.zeros((HEAD_DIM, TQ), BF16)

    def masked_qt(u):
        hd, lu = divmod(u, units_per_head)
        qt = qt_ref[hd * V_DIM:(hd + 1) * V_DIM, lu * TQ:(lu + 1) * TQ]
        return jnp.concatenate([jnp.concatenate([qt[0:HEAD_DIM], zero_half], axis=0),
                                jnp.concatenate([zero_half, qt[HEAD_DIM:]], axis=0)], axis=1)

    def bias_tile(kc, qh):
        if kc < n_kc:
            t = min(max(kc - qh, -2), 2) + 2
            return t * TK, TK, t in (0, N_BIAS_TILES - 1)
        return N_BIAS_TILES * TK + min(qh, 1) * N_META, N_META, qh >= 1

    def score_chunk(u, kc, qt, m_acc):
        hd, qi = divmod(u, units_per_head)
        head_lanes = slice(hd * V_DIM, (hd + 1) * V_DIM)
        if kc < n_kc:
            rows = pl.ds(pl.multiple_of(z + kc * TK, TK), TK)
            s = jnp.dot(k_ref[kc * TK:(kc + 1) * TK, head_lanes], qt,
                        preferred_element_type=F32)
        else:
            rows = pl.ds(pl.multiple_of(z + seq, N_META), N_META)
            s = jnp.dot(km_ref[0:N_META, head_lanes], qt, preferred_element_type=F32)
        m_acc = [dict(a) for a in m_acc] if m_acc else [{} for _ in range(halves)]
        for j in range(halves):
            cols = slice(j * TQ, (j + 1) * TQ)
            b0, nrows, constant = bias_tile(kc, qi)
            sc = s[:, cols] if constant else s[:, cols] + bias_ref[hd, b0:b0 + nrows, :]
            s_scr[u % len(s_scr)][rows, cols] = sc
            cm = jnp.max(sc.reshape(nrows // 8, 8, TQ), axis=0)
            key = b0 if constant else None
            m_acc[j][key] = jnp.maximum(m_acc[j][key], cm) if key in m_acc[j] else cm
        return m_acc

    def col_max(u, m_acc):
        hd = u // units_per_head
        out = []
        for acc in m_acc:
            m = None
            for key, a in acc.items():
                v = jnp.max(a, axis=0, keepdims=True)
                if key is not None:
                    v = v + bias_ref[hd, key:key + 1, :]
                m = v if m is None else jnp.maximum(m, v)
            out.append(m)
        return out

    def exp_offset(u, kc, m, cache):
        hd, lu = divmod(u, units_per_head)
        tiles = tuple(bias_tile(kc, lu) for j in range(halves))
        key = tuple(b0 if constant else None for b0, _, constant in tiles)
        if key not in cache:
            cache[key] = jnp.concatenate(
                [m[j] if b0 is None else m[j] - bias_ref[hd, b0:b0 + 1, :]
                 for j, b0 in enumerate(key)], axis=1)
        return cache[key]

    def value_chunk(u, kc, off, oe):
        hd = u // units_per_head
        if kc < n_kc:
            rows = pl.ds(pl.multiple_of(z + kc * TK, TK), TK)
            lhs = vt_scr[hd, :, kc * TK:(kc + 1) * TK]
        else:
            rows = pl.ds(pl.multiple_of(z + seq, N_META), N_META)
            lhs = vmt_scr[hd]
        p = jnp.exp2(s_scr[u % len(s_scr)][rows, :] - off).astype(BF16)
        d = jnp.dot(lhs, p, preferred_element_type=F32)
        return d if oe is None else oe + d

    def finalize(u, oe):
        hd, qi = divmod(u, units_per_head)
        oe0, oe1 = oe[:, 0:TQ], oe[:, TQ:]
        o = (oe0[0:V_DIM] * (1.0 / oe0[V_DIM:V_DIM + 1])
             - oe1[0:V_DIM] * (lam / oe1[V_DIM:V_DIM + 1]))
        ms = jnp.mean(o * o, axis=0, keepdims=True)
        y = ((o * lax.rsqrt(ms + SUBLN_EPS)).T * g_ref[...]) * (1.0 - LAMBDA_INIT)
        o_ref[qi * TQ:(qi + 1) * TQ, hd * V_DIM:(hd + 1) * V_DIM] = y.astype(BF16)

    n_chunks = n_kc + 1
    total = n_units * n_chunks
    lag = n_chunks + CHUNK_SKEW
    score_order = [n_kc] + list(range(n_kc))
    meta_pos = -lag % n_chunks
    value_order = list(range(n_kc))
    value_order.insert(meta_pos, n_kc)
    col_max_of, offsets_of = {}, {}
    qt = m_acc = oe = None
    for g in range(total + lag):
        if g < total:
            u, i = divmod(g, n_chunks)
            if i == 0:
                qt, m_acc = masked_qt(u), None
            m_acc = score_chunk(u, score_order[i], qt, m_acc)
            if i == n_chunks - 1:
                col_max_of[u], offsets_of[u] = col_max(u, m_acc), {}
        if g >= lag:
            u, i = divmod(g - lag, n_chunks)
            off = exp_offset(u, value_order[i], col_max_of[u], offsets_of[u])
            oe = value_chunk(u, value_order[i], off, None if i == 0 else oe)
            if i == n_chunks - 1:
                finalize(u, oe)


def _attention(lambdas, qt, k, vt, k_meta, vt_meta, bias, subln_g, batch, seq):
    hp = ATTN_HEADS_PER_STEP
    width = hp * V_DIM
    head_rows = lambda: pl.BlockSpec((None, seq, width), lambda h, b: (b, 0, h))
    return pl.pallas_call(
        functools.partial(_attn_kernel, seq=seq, hp=hp),
        grid=(N_HEADS // hp, batch),
        in_specs=[
            *[pl.BlockSpec((1, HEAD_DIM), lambda h, b: (0, 0)) for _ in lambdas],
            pl.BlockSpec(memory_space=pltpu.SMEM),
            pl.BlockSpec((width, seq), lambda h, b: (h, b)),
            head_rows(),
            pl.BlockSpec((width, seq), lambda h, b: (h, b)),
            pl.BlockSpec((META_PAD, width), lambda h, b: (0, h)),
            pl.BlockSpec((width, META_PAD), lambda h, b: (h, 0)),
            pl.BlockSpec((hp, BIAS_ROWS, TQ), lambda h, b: (h, 0, 0)),
            pl.BlockSpec((1, V_DIM), lambda h, b: (0, 0)),
        ],
        out_specs=head_rows(),
        out_shape=jax.ShapeDtypeStruct((batch, seq, N_HEADS * V_DIM), BF16),
        scratch_shapes=[
            pltpu.VMEM((seq + N_META, TQM), F32),
            pltpu.VMEM((seq + N_META, TQM), F32),
            pltpu.VMEM((seq + N_META, TQM), F32),
            pltpu.VMEM((hp, V_DIM + ONES_ROWS, seq), BF16),
            pltpu.VMEM((hp, V_DIM + ONES_ROWS, N_META), BF16),
        ],
        compiler_params=pltpu.CompilerParams(
            dimension_semantics=("arbitrary", "arbitrary"), vmem_limit_bytes=VMEM_LIMIT_BYTES),
        name="diff_attn",
    )(*lambdas, np.zeros((1, 1), np.int32), qt, k, vt, k_meta, vt_meta, bias, subln_g)


def _rms(x, g):
    ms = jnp.mean(x * x, axis=-1, keepdims=True)
    return (x * lax.rsqrt(ms + NORM_EPS)) * g


def _tail_kernel(x_ref, zp_ref, zprev_ref, znext_ref, zmeta_ref, ya_ref, pw_ref, ps_ref, wo_ref,
                 g2_ref, wg_ref, wu_ref, wd_ref, gf_ref, o_ref, e_scr, c_scr, *, tm, seq):
    tiles_per_seq = seq // tm
    t = lax.rem(pl.program_id(0), tiles_per_seq)
    p0 = t * tm
    prev = jnp.where(t == 0, zmeta_ref[...], zprev_ref[...])
    nxt = jnp.where(t == tiles_per_seq - 1, 0.0, znext_ref[...])
    zp = zp_ref[...]
    for g in range(len(POOL_WINDOWS)):
        lanes = slice(g * POOL_GROUP, (g + 1) * POOL_GROUP)
        e_scr[g, 0:HALO, :] = prev[:, lanes]
        e_scr[g, HALO:HALO + tm, :] = zp[:, lanes]
        e_scr[g, HALO + tm:2 * HALO + tm, :] = nxt[:, lanes]
        e_scr[g, 2 * HALO + tm:, :] = jnp.zeros((2 * HALO, POOL_GROUP), F32)

    rp = tm // TAIL_PARTS
    tail_row = lax.broadcasted_iota(jnp.int32, (HALO, POOL_GROUP), 0)

    def window_sum(g, w, r0):
        left = w // 2
        base = HALO + r0
        if w <= 4:
            win = e_scr[g, base - left:base - left + rp, :]
            for k in range(-left + 1, w - left):
                win = win + e_scr[g, base + k:base + k + rp, :]
            return win
        j0 = base - HALO
        src, width, n = e_scr.at[g], 1, rp + 3 * HALO
        level = 0
        while 2 * width < w:
            c_scr[level, j0:j0 + n, :] = src[j0:j0 + n, :] + src[j0 + width:j0 + width + n, :]
            src, width, n, level = c_scr.at[level], 2 * width, n - HALO, level + 1
        lo = base - left
        return src[lo:lo + rp, :] + src[lo + width:lo + width + rp, :]

    def pooled(r0):
        y_groups = []
        for g, w in enumerate(POOL_WINDOWS):
            right = w - 1 - w // 2
            win = window_sum(g, w, r0)
            center = zp_ref[r0:r0 + rp, g * POOL_GROUP:(g + 1) * POOL_GROUP]
            if r0 + rp < tm:
                d = win * (1.0 / w) - center
            else:
                body = win[0:rp - HALO] * (1.0 / w) - center[0:rp - HALO]
                pos = p0 + (tm - HALO) + tail_row
                over = jnp.maximum(pos + (right - (seq - 1)), 0)
                cnt = (w - over).astype(F32)
                d = jnp.concatenate([body, win[rp - HALO:] / cnt - center[rp - HALO:]], axis=0)
            yg = jnp.dot(d.astype(BF16), pw_ref[g], preferred_element_type=F32)
            y_groups.append((yg * ps_ref[:, g * POOL_GROUP:(g + 1) * POOL_GROUP]).astype(BF16))
        return jnp.concatenate(y_groups + [ya_ref[r0:r0 + rp, :]], axis=-1)

    def out_proj(r0, ycat):
        return x_ref[r0:r0 + rp, :] + jnp.dot(ycat, wo_ref[...], preferred_element_type=F32)

    def activation(f, cols):
        gate = jnp.dot(f, wg_ref[:, cols], preferred_element_type=F32)
        up = jnp.dot(f, wu_ref[:, cols], preferred_element_type=F32)
        return (gate * jax.nn.sigmoid(gate) * up).astype(BF16)

    def down(act, cols):
        return jnp.dot(act, wd_ref[cols, :], preferred_element_type=F32)

    parts = [slice(i * rp, (i + 1) * rp) for i in range(TAIL_PARTS)]
    chunks = [slice(c0, min(c0 + FF_CHUNK, D_FF)) for c0 in range(0, D_FF, FF_CHUNK)]
    f_parts, acc_parts = [], []
    for rows in parts:
        h1 = out_proj(rows.start, pooled(rows.start))
        f_parts.append(_rms(h1, g2_ref[...]).astype(BF16))
        acc_parts.append(h1)
    for i in range(TAIL_PARTS):
        acc_parts[i] = acc_parts[i] + down(activation(f_parts[i], chunks[0]), chunks[0])
    f = jnp.concatenate(f_parts, axis=0)
    acc = jnp.concatenate(acc_parts, axis=0)
    for cols in chunks[1:-1]:
        acc = acc + down(activation(f, cols), cols)
    act = activation(f, chunks[-1])
    for rows in parts:
        h2 = acc[rows] + down(act[rows], chunks[-1])
        o_ref[rows, :] = _rms(h2, gf_ref[...])


def _tail(x2d, zp, zp_meta, y_attn, pool_w, pool_scale, w_o, g2, w_gate, w_up, w_down, g_final,
          tm, seq):
    rows = x2d.shape[0]
    halo_blocks = tm // HALO
    last_halo_block = rows // HALO - 1
    const = lambda shape: pl.BlockSpec(shape, lambda i: (0,) * len(shape),
                                       pipeline_mode=pl.Buffered(1))
    return pl.pallas_call(
        functools.partial(_tail_kernel, tm=tm, seq=seq),
        grid=(rows // tm,),
        in_specs=[
            pl.BlockSpec((tm, D_MODEL), lambda i: (i, 0)),
            pl.BlockSpec((tm, POOL_WIDTH), lambda i: (i, 0)),
            pl.BlockSpec((HALO, POOL_WIDTH), lambda i: (jnp.maximum(i * halo_blocks - 1, 0), 0)),
            pl.BlockSpec((HALO, POOL_WIDTH),
                         lambda i: (jnp.minimum((i + 1) * halo_blocks, last_halo_block), 0)),
            pl.BlockSpec((HALO, POOL_WIDTH), lambda i: (N_META // HALO - 1, 0)),
            pl.BlockSpec((tm, N_HEADS * V_DIM), lambda i: (i, 0)),
            const((len(POOL_WINDOWS), POOL_GROUP, POOL_GROUP)),
            const((1, POOL_WIDTH)),
            const((D_MODEL, D_MODEL)),
            const((1, D_MODEL)),
            const((D_MODEL, D_FF)),
            const((D_MODEL, D_FF)),
            const((D_FF, D_MODEL)),
            const((1, D_MODEL)),
        ],
        out_specs=pl.BlockSpec((tm, D_MODEL), lambda i: (i, 0)),
        out_shape=jax.ShapeDtypeStruct((rows, D_MODEL), F32),
        scratch_shapes=[pltpu.VMEM((len(POOL_WINDOWS), tm + 4 * HALO, POOL_GROUP), F32),
                        pltpu.VMEM((3, tm + 4 * HALO, POOL_GROUP), F32)],
        compiler_params=pltpu.CompilerParams(
            dimension_semantics=("arbitrary",), vmem_limit_bytes=VMEM_LIMIT_BYTES),
        name="pool_oproj_ffn",
    )(x2d, zp, zp, zp, zp_meta, y_attn, pool_w, pool_scale, w_o, g2, w_gate, w_up, w_down, g_final)


def kernel(x, meta_tokens, rel_bias, norm1_g, w_in, pool_w, pool_scale, lambda_q1, lambda_k1,
           lambda_q2, lambda_k2, subln_g, w_o, norm2_g, w_gate, w_up, w_down, final_g):
    batch, seq, _ = x.shape
    layer = 0
    lambdas = [v[layer].astype(F32).reshape(1, HEAD_DIM)
               for v in (lambda_q1, lambda_k1, lambda_q2, lambda_k2)]

    col_scale = np.ones((1, IN_WIDTH), np.float32)
    col_scale[:, POOL_WIDTH:POOL_WIDTH + ATTN_WIDTH] = HEAD_DIM ** -0.5 * LOG2E
    g1 = norm1_g[layer].reshape(1, D_MODEL)

    x2d = x.reshape(batch * seq, D_MODEL)
    n_groups = len(POOL_WINDOWS)
    (zp, qt, k, vt, zp_meta, k_meta, vt_meta,
     w_o_b, w_gate_b, w_up_b, w_down_b, pool_w_b) = _inproj(
        x2d, meta_tokens, g1, w_in[layer], col_scale, tm=1024,
        cast_weights=(w_o[layer], w_gate[layer], w_up[layer], w_down[layer],
                      pool_w[layer].reshape(n_groups * POOL_GROUP, POOL_GROUP)))

    bias = _bias_tiles(rel_bias)
    y_attn = _attention(
        lambdas, qt, k.reshape(batch, seq, ATTN_WIDTH),
        vt, k_meta, vt_meta, bias, subln_g[layer].astype(F32).reshape(1, V_DIM), batch, seq)

    out = _tail(
        x2d, zp, zp_meta, y_attn.reshape(batch * seq, N_HEADS * V_DIM),
        pool_w_b.reshape(n_groups, POOL_GROUP, POOL_GROUP), pool_scale[layer].reshape(1, POOL_WIDTH),
        w_o_b, norm2_g[layer].reshape(1, D_MODEL), w_gate_b, w_up_b, w_down_b,
        final_g.reshape(1, D_MODEL), tm=1024, seq=seq)
    return out.reshape(batch, seq, D_MODEL)
```

```python
import functools
import math

import jax
import jax.numpy as jnp
import numpy as np
from jax import lax
from jax.experimental import pallas as pl
from jax.experimental.pallas import tpu as pltpu

F32 = jnp.float32
BF16 = jnp.bfloat16

D_MODEL = 1024
N_META = 16
META_PAD = 128
POOL_WIDTH = 512
POOL_WINDOWS = (2, 4, 8, 16)
POOL_GROUP = 128
ATTN_WIDTH = 512
HEAD_DIM = 64
N_HEADS = 4
V_DIM = 128
IN_WIDTH = POOL_WIDTH + 3 * ATTN_WIDTH
REL_BUCKETS = 32
REL_MAX_DIST = 128
D_FF = 2816
NORM_EPS = 1e-6
SUBLN_EPS = 1e-5
LAMBDA_INIT = 0.8 - 0.6 * math.exp(-0.3 * 0)
HALO = 8

VMEM_LIMIT_BYTES = 56 * 1024 * 1024
BF16_SUBLANES = 16

TQ = 256
TQM = 2 * TQ
TK = 256
N_BIAS_TILES = 5
BIAS_ROWS = N_BIAS_TILES * TK + 2 * N_META
FF_CHUNK = 256
ONES_ROWS = 16
LOG2E = math.log2(math.e)
CHUNK_SKEW = 2
TAIL_PARTS = 2
INPROJ_PARTS = 2
ATTN_HEADS_PER_STEP = 1


def _inproj_kernel(x_ref, meta_ref, g_ref, w_ref, cs_ref, *refs, n_cast):
    cast_in = refs[:n_cast]
    zp_ref, qt_ref, k_ref, vt_ref, zpm_ref, km_ref, vmt_ref = refs[n_cast:n_cast + 7]
    cast_out = refs[n_cast + 7:-1]
    wb_scr = refs[-1]
    o_q = POOL_WIDTH
    o_k = o_q + ATTN_WIDTH
    o_v = o_k + ATTN_WIDTH

    def project(x, zp_out, qt_out, k_out, vt_out):
        ms = jnp.mean(x * x, axis=-1, keepdims=True)
        u = ((x * lax.rsqrt(ms + NORM_EPS)) * g_ref[...]).astype(BF16)
        proj = lambda lo, hi: jnp.dot(u, wb_scr[:, lo:hi], preferred_element_type=F32)
        zp_out[...] = proj(0, o_q)
        if qt_out is not None:
            qt_out[...] = proj(o_q, o_k).T.astype(BF16)
        k_out[...] = proj(o_k, o_v).astype(BF16)
        vt_out[...] = proj(o_v, IN_WIDTH).T.astype(BF16)

    @pl.when(pl.program_id(0) == 0)
    def _():
        wb_scr[...] = (w_ref[...] * cs_ref[...]).astype(BF16)
        meta = jnp.concatenate(
            [meta_ref[...], jnp.zeros((META_PAD - N_META, D_MODEL), F32)], axis=0)
        project(meta, zpm_ref, None, km_ref, vmt_ref)

    for src, dst in zip(cast_in, cast_out):
        dst[...] = src[...].astype(BF16)
    rp = x_ref.shape[0] // INPROJ_PARTS
    for i in range(INPROJ_PARTS):
        rows = slice(i * rp, (i + 1) * rp)
        project(x_ref[rows, :], zp_ref.at[rows, :], qt_ref.at[:, rows], k_ref.at[rows, :],
                vt_ref.at[:, rows])


def _inproj(x2d, meta_tokens, g, w_in, col_scale, tm, cast_weights=()):
    rows = x2d.shape[0]
    steps = rows // tm
    row_spec = lambda width: pl.BlockSpec((tm, width), lambda i: (i, 0))
    col_spec = lambda width: pl.BlockSpec((width, tm), lambda i: (0, i))
    const = lambda shape: pl.BlockSpec(shape, lambda i: (0, 0), pipeline_mode=pl.Buffered(1))
    fixed = lambda shape: pl.BlockSpec(shape, lambda i: (0, 0))

    def cast_spec(w):
        n_rows, n_cols = w.shape
        block = min(d for d in range(BF16_SUBLANES, n_rows + 1, BF16_SUBLANES)
                    if n_rows % d == 0 and d * steps >= n_rows)
        last = n_rows // block - 1
        return pl.BlockSpec((block, n_cols), lambda i: (jnp.minimum(i, last), 0))

    cast_specs = [cast_spec(w) for w in cast_weights]
    return pl.pallas_call(
        functools.partial(_inproj_kernel, n_cast=len(cast_weights)),
        grid=(steps,),
        in_specs=[
            row_spec(D_MODEL),
            const((N_META, D_MODEL)),
            const((1, D_MODEL)),
            const((D_MODEL, IN_WIDTH)),
            const((1, IN_WIDTH)),
            *cast_specs,
        ],
        out_specs=[row_spec(POOL_WIDTH), col_spec(ATTN_WIDTH), row_spec(ATTN_WIDTH),
                   col_spec(ATTN_WIDTH),
                   fixed((META_PAD, POOL_WIDTH)), fixed((META_PAD, ATTN_WIDTH)),
                   fixed((ATTN_WIDTH, META_PAD)), *cast_specs],
        out_shape=[
            jax.ShapeDtypeStruct((rows, POOL_WIDTH), F32),
            jax.ShapeDtypeStruct((ATTN_WIDTH, rows), BF16),
            jax.ShapeDtypeStruct((rows, ATTN_WIDTH), BF16),
            jax.ShapeDtypeStruct((ATTN_WIDTH, rows), BF16),
            jax.ShapeDtypeStruct((META_PAD, POOL_WIDTH), F32),
            jax.ShapeDtypeStruct((META_PAD, ATTN_WIDTH), BF16),
            jax.ShapeDtypeStruct((ATTN_WIDTH, META_PAD), BF16),
            *[jax.ShapeDtypeStruct(w.shape, BF16) for w in cast_weights],
        ],
        scratch_shapes=[pltpu.VMEM((D_MODEL, IN_WIDTH), BF16)],
        compiler_params=pltpu.CompilerParams(
            dimension_semantics=("arbitrary",), vmem_limit_bytes=VMEM_LIMIT_BYTES),
        name="inproj",
    )(x2d, meta_tokens, g, w_in, col_scale, *cast_weights)


def _t5_bucket(rel):
    nb = REL_BUCKETS // 2
    ret = jnp.where(rel > 0, nb, 0)
    n = jnp.abs(rel)
    max_exact = nb // 2
    nf = jnp.maximum(n, 1).astype(F32)
    large = max_exact + (jnp.log(nf / max_exact) / math.log(REL_MAX_DIST / max_exact)
                         * (nb - max_exact)).astype(jnp.int32)
    large = jnp.minimum(large, nb - 1)
    return ret + jnp.where(n < max_exact, n, large)


def _bias_buckets():
    r = lax.broadcasted_iota(jnp.int32, (TK, TQ), 0)
    c = lax.broadcasted_iota(jnp.int32, (TK, TQ), 1)
    tiles = [(t - 2) * TK + r - c for t in range(N_BIAS_TILES)]
    rm = lax.broadcasted_iota(jnp.int32, (N_META, TQ), 0)
    cm = lax.broadcasted_iota(jnp.int32, (N_META, TQ), 1)
    metas = [rm - N_META - m * TQ - cm for m in range(2)]
    return _t5_bucket(jnp.concatenate(tiles + metas, axis=0))


def _bias_kernel(rb_ref, bkt_ref, o_ref):
    h = pl.program_id(0)

    def lookup(bkt):
        level = [rb_ref[b, h] * LOG2E for b in range(REL_BUCKETS)]
        for bit in range(REL_BUCKETS.bit_length() - 1):
            odd = (bkt & (1 << bit)) != 0
            level = [jnp.where(odd, level[2 * j + 1], level[2 * j]) for j in range(len(level) // 2)]
        return level[0]

    def fill(r0, rows, constant):
        if constant:
            slab = lookup(bkt_ref[r0:r0 + 8, :])
            o_ref[r0:r0 + rows, :] = jnp.broadcast_to(slab[None], (rows // 8, 8, TQ)).reshape(rows, TQ)
        else:
            o_ref[r0:r0 + rows, :] = lookup(bkt_ref[r0:r0 + rows, :])

    for t in range(N_BIAS_TILES):
        fill(t * TK, TK, constant=t in (0, N_BIAS_TILES - 1))
    for m in range(2):
        fill(N_BIAS_TILES * TK + m * N_META, N_META, constant=m == 1)


def _bias_tiles(rel_bias):
    return pl.pallas_call(
        _bias_kernel,
        grid=(N_HEADS,),
        in_specs=[
            pl.BlockSpec(memory_space=pltpu.SMEM),
            pl.BlockSpec((BIAS_ROWS, TQ), lambda h: (0, 0)),
        ],
        out_specs=pl.BlockSpec((None, BIAS_ROWS, TQ), lambda h: (h, 0, 0)),
        out_shape=jax.ShapeDtypeStruct((N_HEADS, BIAS_ROWS, TQ), F32),
        compiler_params=pltpu.CompilerParams(dimension_semantics=("arbitrary",)),
        name="bias_tiles",
    )(rel_bias, _bias_buckets())


_NT = (((1,), (1,)), ((), ()))


def _attn_kernel(lq1_ref, lk1_ref, lq2_ref, lk2_ref, zero_ref, qt_ref, k_ref, vt_ref, km_ref,
                 vmt_ref, bias_ref, g_ref, o_ref, s0_scr, s1_scr, s2_scr, vt_scr, vmt_scr,
                 *, seq, hp):
    n_kc = seq // TK
    units_per_head = seq // TQ
    n_units = hp * units_per_head
    halves = TQM // TQ
    s_scr = (s0_scr, s1_scr, s2_scr)

    for hd in range(hp):
        vt_scr[hd, 0:V_DIM, :] = vt_ref[hd * V_DIM:(hd + 1) * V_DIM, :]
        vt_scr[hd, V_DIM:, :] = jnp.ones((ONES_ROWS, seq), BF16)
        vmt_scr[hd, 0:V_DIM, :] = vmt_ref[hd * V_DIM:(hd + 1) * V_DIM, 0:N_META]
        vmt_scr[hd, V_DIM:, :] = jnp.ones((ONES_ROWS, N_META), BF16)
    lam = (jnp.exp(jnp.sum(lq1_ref[...] * lk1_ref[...], axis=1, keepdims=True))
           - jnp.exp(jnp.sum(lq2_ref[...] * lk2_ref[...], axis=1, keepdims=True))
           + LAMBDA_INIT)
    z = zero_ref[0, 0]
    zero_half = jnp.zeros((HEAD_DIM, TQ), BF16)

    def masked_qt(u):
        hd, lu = divmod(u, units_per_head)
        qt = qt_ref[hd * V_DIM:(hd + 1) * V_DIM, lu * TQ:(lu + 1) * TQ]
        return jnp.concatenate([jnp.concatenate([qt[0:HEAD_DIM], zero_half], axis=0),
                                jnp.concatenate([zero_half, qt[HEAD_DIM:]], axis=0)], axis=1)

    def bias_tile(kc, qh):
        if kc < n_kc:
            t = min(max(kc - qh, -2), 2) + 2
            return t * TK, TK, t in (0, N_BIAS_TILES - 1)
        return N_BIAS_TILES * TK + min(qh, 1) * N_META, N_META, qh >= 1

    def score_chunk(u, kc, qt, m_acc):
        hd, qi = divmod(u, units_per_head)
        head_lanes = slice(hd * V_DIM, (hd + 1) * V_DIM)
        if kc < n_kc:
            rows = pl.ds(pl.multiple_of(z + kc * TK, TK), TK)
            s = jnp.dot(k_ref[kc * TK:(kc + 1) * TK, head_lanes], qt,
                        preferred_element_type=F32)
        else:
            rows = pl.ds(pl.multiple_of(z + seq, N_META), N_META)
            s = jnp.dot(km_ref[0:N_META, head_lanes], qt, preferred_element_type=F32)
        m_acc = [dict(a) for a in m_acc] if m_acc else [{} for _ in range(halves)]
        for j in range(halves):
            cols = slice(j * TQ, (j + 1) * TQ)
            b0, nrows, constant = bias_tile(kc, qi)
            sc = s[:, cols] if constant else s[:, cols] + bias_ref[hd, b0:b0 + nrows, :]
            s_scr[u % len(s_scr)][rows, cols] = sc
            cm = jnp.max(sc.reshape(nrows // 8, 8, TQ), axis=0)
            key = b0 if constant else None
            m_acc[j][key] = jnp.maximum(m_acc[j][key], cm) if key in m_acc[j] else cm
        return m_acc

    def col_max(u, m_acc):
        hd = u // units_per_head
        out = []
        for acc in m_acc:
            m = None
            for key, a in acc.items():
                v = jnp.max(a, axis=0, keepdims=True)
                if key is not None:
                    v = v + bias_ref[hd, key:key + 1, :]
                m = v if m is None else jnp.maximum(m, v)
            out.append(m)
        return out

    def exp_offset(u, kc, m, cache):
        hd, lu = divmod(u, units_per_head)
        tiles = tuple(bias_tile(kc, lu) for j in range(halves))
        key = tuple(b0 if constant else None for b0, _, constant in tiles)
        if key not in cache:
            cache[key] = jnp.concatenate(
                [m[j] if b0 is None else m[j] - bias_ref[hd, b0:b0 + 1, :]
                 for j, b0 in enumerate(key)], axis=1)
        return cache[key]

    def value_chunk(u, kc, off, oe):
        hd = u // units_per_head
        if kc < n_kc:
            rows = pl.ds(pl.multiple_of(z + kc * TK, TK), TK)
            lhs = vt_scr[hd, :, kc * TK:(kc + 1) * TK]
        else:
            rows = pl.ds(pl.multiple_of(z + seq, N_META), N_META)
            lhs = vmt_scr[hd]
        p = jnp.exp2(s_scr[u % len(s_scr)][rows, :] - off).astype(BF16)
        d = jnp.dot(lhs, p, preferred_element_type=F32)
        return d if oe is None else oe + d

    def finalize(u, oe):
        hd, qi = divmod(u, units_per_head)
        oe0, oe1 = oe[:, 0:TQ], oe[:, TQ:]
        o = (oe0[0:V_DIM] * (1.0 / oe0[V_DIM:V_DIM + 1])
             - oe1[0:V_DIM] * (lam / oe1[V_DIM:V_DIM + 1]))
        ms = jnp.mean(o * o, axis=0, keepdims=True)
        y = ((o * lax.rsqrt(ms + SUBLN_EPS)).T * g_ref[...]) * (1.0 - LAMBDA_INIT)
        o_ref[qi * TQ:(qi + 1) * TQ, hd * V_DIM:(hd + 1) * V_DIM] = y.astype(BF16)

    n_chunks = n_kc + 1
    total = n_units * n_chunks
    lag = n_chunks + CHUNK_SKEW
    score_order = [n_kc] + list(range(n_kc))
    meta_pos = -lag % n_chunks
    value_order = list(range(n_kc))
    value_order.insert(meta_pos, n_kc)
    col_max_of, offsets_of = {}, {}
    qt = m_acc = oe = None
    for g in range(total + lag):
        if g < total:
            u, i = divmod(g, n_chunks)
            if i == 0:
                qt, m_acc = masked_qt(u), None
            m_acc = score_chunk(u, score_order[i], qt, m_acc)
            if i == n_chunks - 1:
                col_max_of[u], offsets_of[u] = col_max(u, m_acc), {}
        if g >= lag:
            u, i = divmod(g - lag, n_chunks)
            off = exp_offset(u, value_order[i], col_max_of[u], offsets_of[u])
            oe = value_chunk(u, value_order[i], off, None if i == 0 else oe)
            if i == n_chunks - 1:
                finalize(u, oe)


def _attention(lambdas, qt, k, vt, k_meta, vt_meta, bias, subln_g, batch, seq):
    hp = ATTN_HEADS_PER_STEP
    width = hp * V_DIM
    head_rows = lambda: pl.BlockSpec((None, seq, width), lambda h, b: (b, 0, h))
    return pl.pallas_call(
        functools.partial(_attn_kernel, seq=seq, hp=hp),
        grid=(N_HEADS // hp, batch),
        in_specs=[
            *[pl.BlockSpec((1, HEAD_DIM), lambda h, b: (0, 0)) for _ in lambdas],
            pl.BlockSpec(memory_space=pltpu.SMEM),
            pl.BlockSpec((width, seq), lambda h, b: (h, b)),
            head_rows(),
            pl.BlockSpec((width, seq), lambda h, b: (h, b)),
            pl.BlockSpec((META_PAD, width), lambda h, b: (0, h), pipeline_mode=pl.Buffered(1)),
            pl.BlockSpec((width, META_PAD), lambda h, b: (h, 0), pipeline_mode=pl.Buffered(1)),
            pl.BlockSpec((hp, BIAS_ROWS, TQ), lambda h, b: (h, 0, 0),
                         pipeline_mode=pl.Buffered(1)),
            pl.BlockSpec((1, V_DIM), lambda h, b: (0, 0), pipeline_mode=pl.Buffered(1)),
        ],
        out_specs=head_rows(),
        out_shape=jax.ShapeDtypeStruct((batch, seq, N_HEADS * V_DIM), BF16),
        scratch_shapes=[
            pltpu.VMEM((seq + N_META, TQM), F32),
            pltpu.VMEM((seq + N_META, TQM), F32),
            pltpu.VMEM((seq + N_META, TQM), F32),
            pltpu.VMEM((hp, V_DIM + ONES_ROWS, seq), BF16),
            pltpu.VMEM((hp, V_DIM + ONES_ROWS, N_META), BF16),
        ],
        compiler_params=pltpu.CompilerParams(
            dimension_semantics=("arbitrary", "arbitrary"), vmem_limit_bytes=VMEM_LIMIT_BYTES),
        name="diff_attn",
    )(*lambdas, np.zeros((1, 1), np.int32), qt, k, vt, k_meta, vt_meta, bias, subln_g)


def _rms(x, g):
    ms = jnp.mean(x * x, axis=-1, keepdims=True)
    return (x * lax.rsqrt(ms + NORM_EPS)) * g


def _tail_kernel(x_ref, zp_ref, zprev_ref, znext_ref, zmeta_ref, ya_ref, pw_ref, ps_ref, wo_ref,
                 g2_ref, wg_ref, wu_ref, wd_ref, gf_ref, o_ref, e_scr, c_scr, *, tm, seq):
    tiles_per_seq = seq // tm
    t = lax.rem(pl.program_id(0), tiles_per_seq)
    p0 = t * tm
    prev = jnp.where(t == 0, zmeta_ref[...], zprev_ref[...])
    nxt = jnp.where(t == tiles_per_seq - 1, 0.0, znext_ref[...])
    zp = zp_ref[...]
    for g in range(len(POOL_WINDOWS)):
        lanes = slice(g * POOL_GROUP, (g + 1) * POOL_GROUP)
        e_scr[g, 0:HALO, :] = prev[:, lanes]
        e_scr[g, HALO:HALO + tm, :] = zp[:, lanes]
        e_scr[g, HALO + tm:2 * HALO + tm, :] = nxt[:, lanes]
        e_scr[g, 2 * HALO + tm:, :] = jnp.zeros((2 * HALO, POOL_GROUP), F32)

    rp = tm // TAIL_PARTS
    tail_row = lax.broadcasted_iota(jnp.int32, (HALO, POOL_GROUP), 0)

    def window_sum(g, w, r0):
        left = w // 2
        base = HALO + r0
        if w <= 4:
            win = e_scr[g, base - left:base - left + rp, :]
            for k in range(-left + 1, w - left):
                win = win + e_scr[g, base + k:base + k + rp, :]
            return win
        j0 = base - HALO
        src, width, n = e_scr.at[g], 1, rp + 3 * HALO
        level = 0
        while 2 * width < w:
            c_scr[level, j0:j0 + n, :] = src[j0:j0 + n, :] + src[j0 + width:j0 + width + n, :]
            src, width, n, level = c_scr.at[level], 2 * width, n - HALO, level + 1
        lo = base - left
        return src[lo:lo + rp, :] + src[lo + width:lo + width + rp, :]

    def pooled(r0):
        y_groups = []
        for g, w in enumerate(POOL_WINDOWS):
            right = w - 1 - w // 2
            win = window_sum(g, w, r0)
            center = zp_ref[r0:r0 + rp, g * POOL_GROUP:(g + 1) * POOL_GROUP]
            if r0 + rp < tm:
                d = win * (1.0 / w) - center
            else:
                body = win[0:rp - HALO] * (1.0 / w) - center[0:rp - HALO]
                pos = p0 + (tm - HALO) + tail_row
                over = jnp.maximum(pos + (right - (seq - 1)), 0)
                cnt = (w - over).astype(F32)
                d = jnp.concatenate([body, win[rp - HALO:] / cnt - center[rp - HALO:]], axis=0)
            yg = jnp.dot(d.astype(BF16), pw_ref[g], preferred_element_type=F32)
            y_groups.append((yg * ps_ref[:, g * POOL_GROUP:(g + 1) * POOL_GROUP]).astype(BF16))
        return jnp.concatenate(y_groups + [ya_ref[r0:r0 + rp, :]], axis=-1)

    def out_proj(r0, ycat):
        return x_ref[r0:r0 + rp, :] + jnp.dot(ycat, wo_ref[...], preferred_element_type=F32)

    def activation(f, cols):
        gate = jnp.dot(f, wg_ref[:, cols], preferred_element_type=F32)
        up = jnp.dot(f, wu_ref[:, cols], preferred_element_type=F32)
        return (gate * jax.nn.sigmoid(gate) * up).astype(BF16)

    def down(act, cols):
        return jnp.dot(act, wd_ref[cols, :], preferred_element_type=F32)

    parts = [slice(i * rp, (i + 1) * rp) for i in range(TAIL_PARTS)]
    chunks = [slice(c0, min(c0 + FF_CHUNK, D_FF)) for c0 in range(0, D_FF, FF_CHUNK)]
    f_parts, acc_parts = [], []
    for rows in parts:
        h1 = out_proj(rows.start, pooled(rows.start))
        f_parts.append(_rms(h1, g2_ref[...]).astype(BF16))
        acc_parts.append(h1)
    for i in range(TAIL_PARTS):
        acc_parts[i] = acc_parts[i] + down(activation(f_parts[i], chunks[0]), chunks[0])
    f = jnp.concatenate(f_parts, axis=0)
    acc = jnp.concatenate(acc_parts, axis=0)
    for cols in chunks[1:-1]:
        acc = acc + down(activation(f, cols), cols)
    act = activation(f, chunks[-1])
    for rows in parts:
        h2 = acc[rows] + down(act[rows], chunks[-1])
        o_ref[rows, :] = _rms(h2, gf_ref[...])


def _tail(x2d, zp, zp_meta, y_attn, pool_w, pool_scale, w_o, g2, w_gate, w_up, w_down, g_final,
          tm, seq):
    rows = x2d.shape[0]
    halo_blocks = tm // HALO
    last_halo_block = rows // HALO - 1
    const = lambda shape: pl.BlockSpec(shape, lambda i: (0,) * len(shape),
                                       pipeline_mode=pl.Buffered(1))
    return pl.pallas_call(
        functools.partial(_tail_kernel, tm=tm, seq=seq),
        grid=(rows // tm,),
        in_specs=[
            pl.BlockSpec((tm, D_MODEL), lambda i: (i, 0)),
            pl.BlockSpec((tm, POOL_WIDTH), lambda i: (i, 0)),
            pl.BlockSpec((HALO, POOL_WIDTH), lambda i: (jnp.maximum(i * halo_blocks - 1, 0), 0)),
            pl.BlockSpec((HALO, POOL_WIDTH),
                         lambda i: (jnp.minimum((i + 1) * halo_blocks, last_halo_block), 0)),
            pl.BlockSpec((HALO, POOL_WIDTH), lambda i: (N_META // HALO - 1, 0)),
            pl.BlockSpec((tm, N_HEADS * V_DIM), lambda i: (i, 0)),
            const((len(POOL_WINDOWS), POOL_GROUP, POOL_GROUP)),
            const((1, POOL_WIDTH)),
            const((D_MODEL, D_MODEL)),
            const((1, D_MODEL)),
            const((D_MODEL, D_FF)),
            const((D_MODEL, D_FF)),
            const((D_FF, D_MODEL)),
            const((1, D_MODEL)),
        ],
        out_specs=pl.BlockSpec((tm, D_MODEL), lambda i: (i, 0)),
        out_shape=jax.ShapeDtypeStruct((rows, D_MODEL), F32),
        scratch_shapes=[pltpu.VMEM((len(POOL_WINDOWS), tm + 4 * HALO, POOL_GROUP), F32),
                        pltpu.VMEM((3, tm + 4 * HALO, POOL_GROUP), F32)],
        compiler_params=pltpu.CompilerParams(
            dimension_semantics=("arbitrary",), vmem_limit_bytes=VMEM_LIMIT_BYTES),
        name="pool_oproj_ffn",
    )(x2d, zp, zp, zp, zp_meta, y_attn, pool_w, pool_scale, w_o, g2, w_gate, w_up, w_down, g_final)


def kernel(x, meta_tokens, rel_bias, norm1_g, w_in, pool_w, pool_scale, lambda_q1, lambda_k1,
           lambda_q2, lambda_k2, subln_g, w_o, norm2_g, w_gate, w_up, w_down, final_g):
    batch, seq, _ = x.shape
    layer = 0
    lambdas = [v[layer].astype(F32).reshape(1, HEAD_DIM)
               for v in (lambda_q1, lambda_k1, lambda_q2, lambda_k2)]

    col_scale = np.ones((1, IN_WIDTH), np.float32)
    col_scale[:, POOL_WIDTH:POOL_WIDTH + ATTN_WIDTH] = HEAD_DIM ** -0.5 * LOG2E
    g1 = norm1_g[layer].reshape(1, D_MODEL)

    x2d = x.reshape(batch * seq, D_MODEL)
    n_groups = len(POOL_WINDOWS)
    (zp, qt, k, vt, zp_meta, k_meta, vt_meta,
     w_o_b, w_gate_b, w_up_b, w_down_b, pool_w_b) = _inproj(
        x2d, meta_tokens, g1, w_in[layer], col_scale, tm=1024,
        cast_weights=(w_o[layer], w_gate[layer], w_up[layer], w_down[layer],
                      pool_w[layer].reshape(n_groups * POOL_GROUP, POOL_GROUP)))

    bias = _bias_tiles(rel_bias)
    y_attn = _attention(
        lambdas, qt, k.reshape(batch, seq, ATTN_WIDTH),
        vt, k_meta, vt_meta, bias, subln_g[layer].astype(F32).reshape(1, V_DIM), batch, seq)

    out = _tail(
        x2d, zp, zp_meta, y_attn.reshape(batch * seq, N_HEADS * V_DIM),
        pool_w_b.reshape(n_groups, POOL_GROUP, POOL_GROUP), pool_scale[layer].reshape(1, POOL_WIDTH),
        w_o_b, norm2_g[layer].reshape(1, D_MODEL), w_gate_b, w_up_b, w_down_b,
        final_g.reshape(1, D_MODEL), tm=1024, seq=seq)
    return out.reshape(batch, seq, D_MODEL)
```
